```python
import math
import jax, jax.numpy as jnp
from jax import lax
import numpy as np

D_MODEL = 1024
BATCH = 8
SEQ = 4096
DEPTH = 2

A_HEADS = 4
A_HEAD_DIM = 64
A_WIDTH = A_HEADS * A_HEAD_DIM
IDX_HEADS = 4
IDX_DIM = 64
TOPK_MAX = 256
B_GROUPS = 4
B_WIDTH = 256
B_GROUP_DIM = B_WIDTH // B_GROUPS
CHUNK = 128
C_HEADS = 4
C_HEAD_DIM = 32
C_V_DIM = 2 * C_HEAD_DIM
C_WIDTH = C_HEADS * C_V_DIM
N_BRANCH = 3
D_FF = 4 * D_MODEL
Q_BLOCK = 128
EPS = 1e-6
NEG = -1e30

IN_SIZES = (
    A_WIDTH, A_WIDTH, A_WIDTH,
    IDX_HEADS * IDX_DIM, IDX_DIM, IDX_HEADS,
    2 * B_WIDTH,
    C_HEADS * 2 * C_HEAD_DIM,
    C_HEADS * 2 * C_HEAD_DIM,
    C_WIDTH,
    N_BRANCH * D_MODEL,
)
N_IN = sum(IN_SIZES)

kernel_name = "hybrid_dsa_gmlp_diffattn_gated"


def rms_norm(x, g):
    xf = x.astype(jnp.float32)
    y = xf * lax.rsqrt(jnp.mean(xf * xf, axis=-1, keepdims=True) + EPS)
    return (y * g.astype(jnp.float32)).astype(x.dtype)


def layer_norm(x, g, b):
    xf = x.astype(jnp.float32)
    mu = jnp.mean(xf, axis=-1, keepdims=True)
    xc = xf - mu
    y = xc * lax.rsqrt(jnp.mean(xc * xc, axis=-1, keepdims=True) + EPS)
    return (y * g.astype(jnp.float32) + b.astype(jnp.float32)).astype(x.dtype)


def split_cols(t):
    offs = np.cumsum(np.array(IN_SIZES))[:-1].tolist()
    return jnp.split(t, offs, axis=-1)


def dsa_attention(q, k, v, q_idx, k_idx, w_idx):
    bsz, seq = q.shape[0], q.shape[1]
    n_keys = seq
    top_k = min(TOPK_MAX, n_keys // 4)
    key_pos = jnp.arange(n_keys)
    idx_scale = IDX_DIM ** -0.5
    w_scale = IDX_HEADS ** -0.5
    attn_scale = A_HEAD_DIM ** -0.5

    def block(i):
        t0 = i * Q_BLOCK
        qb = lax.dynamic_slice_in_dim(q, t0, Q_BLOCK, axis=1)
        qib = lax.dynamic_slice_in_dim(q_idx, t0, Q_BLOCK, axis=1)
        wb = lax.dynamic_slice_in_dim(w_idx, t0, Q_BLOCK, axis=1)
        q_pos = t0 + jnp.arange(Q_BLOCK)
        causal = key_pos[None, :] <= q_pos[:, None]
        dots = jnp.einsum('bqhd,bsd->bqhs', qib, k_idx).astype(jnp.float32) * idx_scale
        score = jnp.einsum('bqh,bqhs->bqs', wb.astype(jnp.float32) * w_scale, jax.nn.relu(dots))
        score = jnp.where(causal[None], score, -jnp.inf)
        _, sel = lax.top_k(score, top_k)
        valid = sel <= q_pos[None, :, None]
        kg = jax.vmap(lambda kb, ib: kb[ib])(k, sel)
        vg = jax.vmap(lambda vb, ib: vb[ib])(v, sel)
        logits = jnp.einsum('bqhd,bqkhd->bqhk', qb, kg).astype(jnp.float32) * attn_scale
        logits = jnp.where(valid[:, :, None, :], logits, NEG)
        p = jax.nn.softmax(logits, axis=-1).astype(v.dtype)
        return jnp.einsum('bqhk,bqkhd->bqhd', p, vg)

    out = lax.map(block, jnp.arange(seq // Q_BLOCK))
    return out.transpose(1, 0, 2, 3, 4).reshape(bsz, seq, A_WIDTH)


def chunked_sgu(u, v, ln_g, ln_b, w_s, b_s):
    bsz, seq = u.shape[0], u.shape[1]
    vn = layer_norm(v, ln_g, ln_b)
    vc = vn.reshape(bsz, seq // CHUNK, CHUNK, B_GROUPS, B_GROUP_DIM)
    w_causal = w_s * jnp.tril(jnp.ones((CHUNK, CHUNK), w_s.dtype))
    s = jnp.einsum('gts,bcsgd->bctgd', w_causal, vc) + b_s.T[None, None, :, :, None]
    return u * s.reshape(bsz, seq, B_WIDTH)


def diff_attention(q, k, v, lam, lambda_init, subln_g):
    bsz, seq = q.shape[0], q.shape[1]
    key_pos = jnp.arange(seq)
    scale = C_HEAD_DIM ** -0.5

    def block(i):
        t0 = i * Q_BLOCK
        qb = lax.dynamic_slice_in_dim(q, t0, Q_BLOCK, axis=1)
        q_pos = t0 + jnp.arange(Q_BLOCK)
        causal = key_pos[None, :] <= q_pos[:, None]
        logits = jnp.einsum('bqhcd,bshcd->bhcqs', qb, k).astype(jnp.float32) * scale
        logits = jnp.where(causal[None, None, None], logits, NEG)
        p = jax.nn.softmax(logits, axis=-1)
        a = p[:, :, 0] - lam * p[:, :, 1]
        return jnp.einsum('bhqs,bshe->bqhe', a.astype(v.dtype), v)

    out = lax.map(block, jnp.arange(seq // Q_BLOCK))
    out = out.transpose(1, 0, 2, 3, 4).reshape(bsz, seq, C_HEADS, C_V_DIM)
    out = rms_norm(out, subln_g) * (1.0 - lambda_init)
    return out.reshape(bsz, seq, C_WIDTH)


def setup_inputs(seed: int = 0) -> dict:
    key = jax.random.key(seed)
    ks = jax.random.split(key, 20)
    f32 = jnp.float32

    def nrm(k, shape, scale):
        return jax.random.normal(k, shape, f32) * scale

    def gain(k, shape):
        return 1.0 + 0.01 * jax.random.normal(k, shape, f32)

    return {
        "x": nrm(ks[0], (BATCH, SEQ, D_MODEL), 1.0),
        "attn_norm_g": gain(ks[1], (DEPTH, D_MODEL)),
        "w_in": nrm(ks[2], (DEPTH, D_MODEL, N_IN), D_MODEL ** -0.5),
        "idx_k_norm_g": gain(ks[3], (DEPTH, IDX_DIM)),
        "idx_k_norm_b": nrm(ks[4], (DEPTH, IDX_DIM), 0.01),
        "sgu_norm_g": gain(ks[5], (DEPTH, B_WIDTH)),
        "sgu_norm_b": nrm(ks[6], (DEPTH, B_WIDTH), 0.01),
        "sgu_w_s": nrm(ks[7], (DEPTH, B_GROUPS, CHUNK, CHUNK), CHUNK ** -0.5),
        "sgu_b_s": gain(ks[8], (DEPTH, B_GROUPS, CHUNK)),
        "diff_lambda": nrm(ks[9], (DEPTH, 4, C_HEAD_DIM), 0.1),
        "diff_subln_g": gain(ks[10], (DEPTH, C_V_DIM)),
        "w_branch_a": nrm(ks[11], (DEPTH, A_WIDTH, D_MODEL), A_WIDTH ** -0.5),
        "w_branch_b": nrm(ks[12], (DEPTH, B_WIDTH, D_MODEL), B_WIDTH ** -0.5),
        "w_branch_c": nrm(ks[13], (DEPTH, C_WIDTH, D_MODEL), C_WIDTH ** -0.5),
        "w_out": nrm(ks[14], (DEPTH, D_MODEL, D_MODEL), D_MODEL ** -0.5),
        "mlp_norm_g": gain(ks[15], (DEPTH, D_MODEL)),
        "w_ff1": nrm(ks[16], (DEPTH, D_MODEL, D_FF), D_MODEL ** -0.5),
        "w_ff2": nrm(ks[17], (DEPTH, D_FF, D_MODEL), D_FF ** -0.5),
        "final_norm_g": gain(ks[18], (D_MODEL,)),
    }


def reference(x, attn_norm_g, w_in, idx_k_norm_g, idx_k_norm_b, sgu_norm_g, sgu_norm_b,
              sgu_w_s, sgu_b_s, diff_lambda, diff_subln_g, w_branch_a, w_branch_b,
              w_branch_c, w_out, mlp_norm_g, w_ff1, w_ff2, final_norm_g):
    bsz, seq = x.shape[0], x.shape[1]
    for l in range(DEPTH):
        lambda_init = 0.8 - 0.6 * math.exp(-0.3 * l)
        h = rms_norm(x, attn_norm_g[l])
        proj = h @ w_in[l]
        (a_q, a_k, a_v, i_q, i_k, i_w, b_uv, c_q, c_k, c_v, gates) = split_cols(proj)

        y_a = dsa_attention(
            a_q.reshape(bsz, seq, A_HEADS, A_HEAD_DIM),
            a_k.reshape(bsz, seq, A_HEADS, A_HEAD_DIM),
            a_v.reshape(bsz, seq, A_HEADS, A_HEAD_DIM),
            i_q.reshape(bsz, seq, IDX_HEADS, IDX_DIM),
            layer_norm(i_k, idx_k_norm_g[l], idx_k_norm_b[l]),
            i_w)

        b_act = jax.nn.gelu(b_uv)
        u, v = jnp.split(b_act, 2, axis=-1)
        y_b = chunked_sgu(u, v, sgu_norm_g[l], sgu_norm_b[l], sgu_w_s[l], sgu_b_s[l])

        lq = diff_lambda[l].astype(jnp.float32)
        lam = jnp.exp(jnp.sum(lq[0] * lq[1])) - jnp.exp(jnp.sum(lq[2] * lq[3])) + lambda_init
        y_c = diff_attention(
            c_q.reshape(bsz, seq, C_HEADS, 2, C_HEAD_DIM),
            c_k.reshape(bsz, seq, C_HEADS, 2, C_HEAD_DIM),
            c_v.reshape(bsz, seq, C_HEADS, C_V_DIM),
            lam, lambda_init, diff_subln_g[l])

        g = jax.nn.sigmoid(gates).reshape(bsz, seq, N_BRANCH, D_MODEL)
        merged = (g[:, :, 0] * (y_a @ w_branch_a[l])
                  + g[:, :, 1] * (y_b @ w_branch_b[l])
                  + g[:, :, 2] * (y_c @ w_branch_c[l]))
        x = x + merged @ w_out[l]

        h2 = rms_norm(x, mlp_norm_g[l])
        x = x + jnp.square(jax.nn.relu(h2 @ w_ff1[l])) @ w_ff2[l]
    return rms_norm(x, final_norm_g)
```

```python
import functools
import math

import jax
import jax.numpy as jnp
from jax import lax
from jax.experimental import pallas as pl
from jax.experimental.pallas import tpu as pltpu

F32 = jnp.float32
BF16 = jnp.bfloat16
I32 = jnp.int32

A_HEADS = 4
A_HEAD_DIM = 64
IDX_HEADS = 4
IDX_DIM = 64
TOPK_MAX = 256
B_GROUPS = 4
B_WIDTH = 256
CHUNK = 128
C_HEADS = 4
C_HEAD_DIM = 32
C_V_DIM = 64
WIDTH = 256
EPS = 1e-6
NEG = -1e30
INT_MIN = -(2 ** 31)

LANES = 128
VMEM_LIMIT_BYTES = 56 * 1024 * 1024

IN_SIZES = (256, 256, 256, 256, 64, 4, 512, 256, 256, 256, 3072)


def _nt_dot(a, b):
    return lax.dot_general(a, b, (((1,), (1,)), ((), ())), preferred_element_type=F32)


def _dot(a, b):
    return jnp.dot(a, b, preferred_element_type=F32)


def _lane_group_select(vals, width, group):
    rows = vals[0].shape[0]
    lane = lax.broadcasted_iota(I32, (rows, width), 1)
    out = jnp.broadcast_to(vals[-1], (rows, width))
    for g in range(len(vals) - 2, -1, -1):
        out = jnp.where(lane < (g + 1) * group, jnp.broadcast_to(vals[g], (rows, width)), out)
    return out


def _proj_kernel(x_ref, g_ref, wm_ref, wkt_ref, wiw_ref, wg_ref, ikg_ref, ikb_ref,
                 pm_ref, akt_ref, ckt_ref, ikt_ref, iw_ref, gt_ref, *, gate_chunk):
    x = x_ref[0]
    ms = jnp.mean(x * x, axis=-1, keepdims=True)
    h = (x * lax.rsqrt(ms + EPS) * g_ref[...]).astype(BF16)
    n_main = wm_ref.shape[1]
    for c0 in range(0, n_main, 256):
        pm_ref[0, :, c0:c0 + 256] = _dot(h, wm_ref[:, c0:c0 + 256]).astype(BF16)
    n_gate = wg_ref.shape[1]
    for c0 in range(0, n_gate, gate_chunk):
        gt_ref[0, :, c0:c0 + gate_chunk] = _dot(h, wg_ref[:, c0:c0 + gate_chunk]).astype(BF16)
    iw_ref[0] = _dot(h, wiw_ref[...])
    kt = _nt_dot(wkt_ref[...], h)
    akt_ref[0] = kt[0:256].astype(BF16)
    ckt_ref[0] = kt[256:512].astype(BF16)
    ik = kt[512:576]
    mu = jnp.mean(ik, axis=0, keepdims=True)
    xc = ik - mu
    var = jnp.mean(xc * xc, axis=0, keepdims=True)
    ikt_ref[0] = (xc * lax.rsqrt(var + EPS) * ikg_ref[...] + ikb_ref[...]).astype(BF16)


def _proj(x, g, wm, wkt, wiw, wg, ikg, ikb, tm):
    bsz, seq, d = x.shape
    n_main, n_gate = wm.shape[1], wg.shape[1]
    grid = (bsz, seq // tm)
    const = lambda b, i: (0, 0)
    return pl.pallas_call(
        functools.partial(_proj_kernel, gate_chunk=min(n_gate, 768)),
        grid=grid,
        in_specs=[
            pl.BlockSpec((1, tm, d), lambda b, i: (b, i, 0)),
            pl.BlockSpec((1, d), const),
            pl.BlockSpec(wm.shape, const),
            pl.BlockSpec(wkt.shape, const),
            pl.BlockSpec(wiw.shape, const),
            pl.BlockSpec(wg.shape, const),
            pl.BlockSpec(ikg.shape, const),
            pl.BlockSpec(ikb.shape, const),
        ],
        out_specs=[
            pl.BlockSpec((1, tm, n_main), lambda b, i: (b, i, 0)),
            pl.BlockSpec((1, 256, tm), lambda b, i: (b, 0, i)),
            pl.BlockSpec((1, 256, tm), lambda b, i: (b, 0, i)),
            pl.BlockSpec((1, 64, tm), lambda b, i: (b, 0, i)),
            pl.BlockSpec((1, tm, LANES), lambda b, i: (b, i, 0)),
            pl.BlockSpec((1, tm, n_gate), lambda b, i: (b, i, 0)),
        ],
        out_shape=[
            jax.ShapeDtypeStruct((bsz, seq, n_main), BF16),
            jax.ShapeDtypeStruct((bsz, 256, seq), BF16),
            jax.ShapeDtypeStruct((bsz, 256, seq), BF16),
            jax.ShapeDtypeStruct((bsz, 64, seq), BF16),
            jax.ShapeDtypeStruct((bsz, seq, LANES), F32),
            jax.ShapeDtypeStruct((bsz, seq, n_gate), BF16),
        ],
        compiler_params=pltpu.CompilerParams(
            dimension_semantics=("parallel", "parallel"), vmem_limit_bytes=VMEM_LIMIT_BYTES),
        name="proj",
    )(x, g, wm, wkt, wiw, wg, ikg, ikb)


def _dsa_kernel(q_ref, iq_ref, iw_ref, ikt_ref, kt_ref, v_ref, o_ref, key_s, bias_s,
                *, tq, top_k, seq):
    i = pl.program_id(1)
    n_chunks = i + 1
    n_sub = tq // LANES
    row_pos = i * tq + lax.broadcasted_iota(I32, (tq, 1), 0)
    lane_iota = lax.broadcasted_iota(I32, (tq, LANES), 1)

    w_all = iw_ref[0][:, 0:IDX_HEADS] * (IDX_DIM ** -0.5 * IDX_HEADS ** -0.5)
    w_cols = [jnp.broadcast_to(w_all[:, h:h + 1], (tq, tq)) for h in range(IDX_HEADS)]
    iq = iq_ref[0].astype(F32)
    iq_h = [iq[:, h * IDX_DIM:(h + 1) * IDX_DIM].astype(BF16) for h in range(IDX_HEADS)]

    def score_body(c, carry):
        off = pl.multiple_of(c * tq, tq)
        ik = ikt_ref[0, :, pl.ds(off, tq)]
        sc = jnp.zeros((tq, tq), F32)
        for h in range(IDX_HEADS):
            sc = sc + jnp.maximum(_dot(iq_h[h], ik), 0.0) * w_cols[h]
        col_pos = off + lax.broadcasted_iota(I32, (tq, tq), 1)
        sc = jnp.where(col_pos <= row_pos, sc, -jnp.inf)
        bits = pltpu.bitcast(sc, I32)
        key_s[:, pl.ds(off, tq)] = bits ^ ((bits >> 31) & 0x7FFFFFFF)
        return carry

    lax.fori_loop(0, n_chunks, score_body, 0)

    def count(pred):
        def body(c, acc):
            off = pl.multiple_of(c * tq, tq)
            for j in range(n_sub):
                kk = key_s[:, pl.ds(off + j * LANES, LANES)]
                acc = acc + pred(kk, off + j * LANES)
            return acc
        acc = lax.fori_loop(0, n_chunks, body, jnp.zeros((tq, LANES), I32))
        return jnp.sum(acc, axis=1, keepdims=True)

    def count_ge(cand):
        cb = jnp.broadcast_to(cand, (tq, LANES))
        return count(lambda kk, base: jnp.where(kk >= cb, 1, 0))

    c0 = count_ge(jnp.zeros((tq, 1), I32))
    thr = jnp.where(c0 >= top_k, 0, INT_MIN).astype(I32)
    cge = jnp.where(c0 >= top_k, c0, n_chunks * tq)

    def bit_body(j, carry):
        thr, cge = carry
        cand = thr | (jnp.int32(1) << (30 - j))
        cnt = count_ge(cand)
        ok = cnt >= top_k
        return jnp.where(ok, cand, thr), jnp.where(ok, cnt, cge)

    thr, cge = lax.fori_loop(0, 31, bit_body, (thr, cge))
    thr_b = jnp.broadcast_to(thr, (tq, LANES))

    bias_s[:, 0:LANES] = jnp.full((tq, LANES), seq, F32)

    @pl.when(jnp.max(cge) > top_k)
    def _():
        c_gt = count(lambda kk, base: jnp.where(kk > thr_b, 1, 0))
        need = top_k - c_gt

        def idx_body(j, m):
            cand = m | (jnp.int32(1) << (11 - j))
            cb = jnp.broadcast_to(cand, (tq, LANES))
            cnt = count(lambda kk, base: jnp.where(kk == thr_b,
                                                   jnp.where(lane_iota + base < cb, 1, 0), 0))
            return jnp.where(cnt < need, cand, m)

        n_bits = max(1, (seq - 1).bit_length())
        m = lax.fori_loop(12 - n_bits, 12, idx_body, jnp.zeros((tq, 1), I32))
        bias_s[:, 0:LANES] = jnp.broadcast_to((m + 1).astype(F32), (tq, LANES))

    idx_bound = bias_s[:, 0:LANES].astype(I32)

    def bias_body(c, carry):
        off = pl.multiple_of(c * tq, tq)
        for j in range(n_sub):
            base = off + j * LANES
            kk = key_s[:, pl.ds(base, LANES)]
            col = lane_iota + base
            keep_tie = jnp.where(col < idx_bound, 0.0, NEG)
            b = jnp.where(kk > thr_b, 0.0, jnp.where(kk == thr_b, keep_tie, NEG))
            key_s[:, pl.ds(base, LANES)] = pltpu.bitcast(
                jnp.where(col <= row_pos, b, NEG).astype(F32), I32)
        return carry

    lax.fori_loop(0, n_chunks, bias_body, 0)

    q = q_ref[0].astype(F32)
    q_h = [(q[:, h * A_HEAD_DIM:(h + 1) * A_HEAD_DIM]).astype(BF16) for h in range(A_HEADS)]
    scale = A_HEAD_DIM ** -0.5
    lane_w = lax.broadcasted_iota(I32, (tq, WIDTH), 1)
    head_of_lane = lane_w // A_HEAD_DIM

    def attn_body(c, carry):
        ms, ls, acc = carry
        off = pl.multiple_of(c * tq, tq)
        bias = pltpu.bitcast(key_s[:, pl.ds(off, tq)], F32)
        v_c = v_ref[0, pl.ds(off, tq), :]
        new_ms, new_ls, alphas = [], [], []
        upd = jnp.zeros((tq, WIDTH), F32)
        for h in range(A_HEADS):
            k_t = kt_ref[0, h * A_HEAD_DIM:(h + 1) * A_HEAD_DIM, pl.ds(off, tq)]
            s = _dot(q_h[h], k_t) * scale + bias
            m_new = jnp.maximum(ms[h], jnp.max(s, axis=1, keepdims=True))
            alpha = jnp.exp(ms[h] - m_new)
            p = jnp.exp(s - m_new)
            new_ls.append(alpha * ls[h] + jnp.sum(p, axis=1, keepdims=True))
            new_ms.append(m_new)
            alphas.append(alpha)
            v_h = jnp.where(head_of_lane == h, v_c, jnp.zeros_like(v_c))
            upd = upd + _dot(p.astype(BF16), v_h)
        acc = acc * _lane_group_select(alphas, WIDTH, A_HEAD_DIM) + upd
        return tuple(new_ms), tuple(new_ls), acc

    init = (tuple(jnp.full((tq, 1), NEG, F32) for _ in range(A_HEADS)),
            tuple(jnp.zeros((tq, 1), F32) for _ in range(A_HEADS)),
            jnp.zeros((tq, WIDTH), F32))
    ms, ls, acc = lax.fori_loop(0, n_chunks, attn_body, init)
    inv = _lane_group_select([1.0 / l for l in ls], WIDTH, A_HEAD_DIM)
    o_ref[0] = (acc * inv).astype(BF16)


def _dsa(pm, ikt, akt, iw, tq):
    bsz, seq, _ = pm.shape
    top_k = min(TOPK_MAX, seq // 4)
    grid = (bsz, seq // tq)
    return pl.pallas_call(
        functools.partial(_dsa_kernel, tq=tq, top_k=top_k, seq=seq),
        grid=grid,
        in_specs=[
            pl.BlockSpec((1, tq, WIDTH), lambda b, i: (b, i, 0)),
            pl.BlockSpec((1, tq, WIDTH), lambda b, i: (b, i, 2)),
            pl.BlockSpec((1, tq, LANES), lambda b, i: (b, i, 0)),
            pl.BlockSpec((1, 64, seq), lambda b, i: (b, 0, 0)),
            pl.BlockSpec((1, 256, seq), lambda b, i: (b, 0, 0)),
            pl.BlockSpec((1, seq, WIDTH), lambda b, i: (b, 0, 1)),
        ],
        out_specs=pl.BlockSpec((1, tq, WIDTH), lambda b, i: (b, i, 0)),
        out_shape=jax.ShapeDtypeStruct((bsz, seq, WIDTH), BF16),
        scratch_shapes=[pltpu.VMEM((tq, seq), I32), pltpu.VMEM((tq, LANES), F32)],
        compiler_params=pltpu.CompilerParams(
            dimension_semantics=("parallel", "arbitrary"), vmem_limit_bytes=VMEM_LIMIT_BYTES),
        name="dsa",
    )(pm, pm, iw, ikt, akt, pm)


def _diff_kernel(q_ref, kt_ref, v_ref, lam_ref, sg_ref, o_ref, *, tq, lambda_init):
    i = pl.program_id(1)
    n_pairs = 2 * C_HEADS
    scale = C_HEAD_DIM ** -0.5
    q = q_ref[0].astype(F32)
    q_p = [q[:, p * C_HEAD_DIM:(p + 1) * C_HEAD_DIM].astype(BF16) for p in range(n_pairs)]
    lane_w = lax.broadcasted_iota(I32, (tq, WIDTH), 1)
    head_of_lane = lane_w // C_V_DIM
    row_iota = lax.broadcasted_iota(I32, (tq, tq), 0)
    col_iota = lax.broadcasted_iota(I32, (tq, tq), 1)

    def chunk(c, carry, masked):
        ms, ls, acc1, acc2 = carry
        off = pl.multiple_of(c * tq, tq)
        v_c = v_ref[0, pl.ds(off, tq), :]
        new_ms, new_ls, alphas = [], [], []
        upd = [jnp.zeros((tq, WIDTH), F32), jnp.zeros((tq, WIDTH), F32)]
        for h in range(C_HEADS):
            v_h = jnp.where(head_of_lane == h, v_c, jnp.zeros_like(v_c))
            for comp in range(2):
                p_i = 2 * h + comp
                k_t = kt_ref[0, p_i * C_HEAD_DIM:(p_i + 1) * C_HEAD_DIM, pl.ds(off, tq)]
                s = _dot(q_p[p_i], k_t) * scale
                if masked:
                    s = jnp.where(col_iota <= row_iota, s, NEG)
                m_new = jnp.maximum(ms[p_i], jnp.max(s, axis=1, keepdims=True))
                alpha = jnp.exp(ms[p_i] - m_new)
                p = jnp.exp(s - m_new)
                new_ls.append(alpha * ls[p_i] + jnp.sum(p, axis=1, keepdims=True))
                new_ms.append(m_new)
                alphas.append(alpha)
                upd[comp] = upd[comp] + _dot(p.astype(BF16), v_h)
        a1 = _lane_group_select(alphas[0::2], WIDTH, C_V_DIM)
        a2 = _lane_group_select(alphas[1::2], WIDTH, C_V_DIM)
        return tuple(new_ms), tuple(new_ls), acc1 * a1 + upd[0], acc2 * a2 + upd[1]

    init = (tuple(jnp.full((tq, 1), NEG, F32) for _ in range(n_pairs)),
            tuple(jnp.zeros((tq, 1), F32) for _ in range(n_pairs)),
            jnp.zeros((tq, WIDTH), F32), jnp.zeros((tq, WIDTH), F32))
    carry = lax.fori_loop(0, i, lambda c, cr: chunk(c, cr, False), init)
    ms, ls, acc1, acc2 = chunk(i, carry, True)

    lq = lam_ref[...]
    lam = (jnp.exp(jnp.sum(lq[0:1] * lq[1:2], axis=1, keepdims=True))
           - jnp.exp(jnp.sum(lq[2:3] * lq[3:4], axis=1, keepdims=True)) + lambda_init)
    inv1 = _lane_group_select([1.0 / l for l in ls[0::2]], WIDTH, C_V_DIM)
    inv2 = _lane_group_select([1.0 / l for l in ls[1::2]], WIDTH, C_V_DIM)
    out = acc1 * inv1 - lam * (acc2 * inv2)
    sq = out * out
    ms_parts = []
    for h in range(C_HEADS):
        part = jnp.sum(jnp.where(head_of_lane == h, sq, 0.0), axis=1, keepdims=True) / C_V_DIM
        ms_parts.append(lax.rsqrt(part + EPS))
    rs = _lane_group_select(ms_parts, WIDTH, C_V_DIM)
    o_ref[0] = (out * rs * sg_ref[...] * (1.0 - lambda_init)).astype(BF16)


def _diff(pm, ckt, lam_p, sg_full, tq, lambda_init):
    bsz, seq, _ = pm.shape
    grid = (bsz, seq // tq)
    return pl.pallas_call(
        functools.partial(_diff_kernel, tq=tq, lambda_init=lambda_init),
        grid=grid,
        in_specs=[
            pl.BlockSpec((1, tq, WIDTH), lambda b, i: (b, i, 5)),
            pl.BlockSpec((1, 256, seq), lambda b, i: (b, 0, 0)),
            pl.BlockSpec((1, seq, WIDTH), lambda b, i: (b, 0, 6)),
            pl.BlockSpec(lam_p.shape, lambda b, i: (0, 0)),
            pl.BlockSpec(sg_full.shape, lambda b, i: (0, 0)),
        ],
        out_specs=pl.BlockSpec((1, tq, WIDTH), lambda b, i: (b, i, 0)),
        out_shape=jax.ShapeDtypeStruct((bsz, seq, WIDTH), BF16),
        compiler_params=pltpu.CompilerParams(
            dimension_semantics=("parallel", "arbitrary"), vmem_limit_bytes=VMEM_LIMIT_BYTES),
        name="diff",
    )(pm, ckt, pm, lam_p, sg_full)


def _gelu_tanh(x):
    return x * (0.5 * (1.0 + jnp.tanh(math.sqrt(2.0 / math.pi) * (x + 0.044715 * (x * x * x)))))


def _sigmoid(x):
    return 1.0 / (1.0 + jnp.exp(-x))


def _merge_kernel(x_ref, ya_ref, bu_ref, bv_ref, yc_ref, gt_ref, lg_ref, lb_ref, ws_ref, bs_ref,
                  wa_ref, wb_ref, wc_ref, wo_ref, o_ref, yb_s, *, tm):
    d = x_ref.shape[-1]
    lane_w = lax.broadcasted_iota(I32, (CHUNK, B_WIDTH), 1)
    group_of_lane = lane_w // (B_WIDTH // B_GROUPS)
    r_i = lax.broadcasted_iota(I32, (CHUNK, CHUNK), 0)
    c_i = lax.broadcasted_iota(I32, (CHUNK, CHUNK), 1)
    w_tril = [jnp.where(c_i <= r_i, ws_ref[g], 0.0).astype(BF16) for g in range(B_GROUPS)]
    for c in range(tm // CHUNK):
        rows = slice(c * CHUNK, (c + 1) * CHUNK)
        u = _gelu_tanh(bu_ref[0, rows, :].astype(F32))
        v = _gelu_tanh(bv_ref[0, rows, :].astype(F32))
        mu = jnp.mean(v, axis=-1, keepdims=True)
        vc = v - mu
        var = jnp.mean(vc * vc, axis=-1, keepdims=True)
        vn = (vc * lax.rsqrt(var + EPS) * lg_ref[...] + lb_ref[...]).astype(BF16)
        s = bs_ref[...]
        for g in range(B_GROUPS):
            s = s + _dot(w_tril[g], jnp.where(group_of_lane == g, vn, jnp.zeros_like(vn)))
        yb_s[rows, :] = (u * s).astype(BF16)

    merged = jnp.zeros((tm, d), F32)
    branches = ((ya_ref[0], wa_ref), (yb_s[...], wb_ref), (yc_ref[0], wc_ref))
    for n, (y, w_ref) in enumerate(branches):
        gate = _sigmoid(gt_ref[0, :, n * d:(n + 1) * d].astype(F32))
        merged = merged + gate * _dot(y, w_ref[...])
    o_ref[0] = x_ref[0] + _dot(merged.astype(BF16), wo_ref[...])


def _merge(x, pm, ya, yc, gates, lg, lb, ws, bs_full, wa, wb, wc, wo, tm):
    bsz, seq, d = x.shape
    grid = (bsz, seq // tm)
    const2 = lambda b, i: (0, 0)
    return pl.pallas_call(
        functools.partial(_merge_kernel, tm=tm),
        grid=grid,
        in_specs=[
            pl.BlockSpec((1, tm, d), lambda b, i: (b, i, 0)),
            pl.BlockSpec((1, tm, WIDTH), lambda b, i: (b, i, 0)),
            pl.BlockSpec((1, tm, WIDTH), lambda b, i: (b, i, 3)),
            pl.BlockSpec((1, tm, WIDTH), lambda b, i: (b, i, 4)),
            pl.BlockSpec((1, tm, WIDTH), lambda b, i: (b, i, 0)),
            pl.BlockSpec((1, tm, 3 * d), lambda b, i: (b, i, 0)),
            pl.BlockSpec(lg.shape, const2),
            pl.BlockSpec(lb.shape, const2),
            pl.BlockSpec(ws.shape, lambda b, i: (0, 0, 0)),
            pl.BlockSpec(bs_full.shape, const2),
            pl.BlockSpec(wa.shape, const2),
            pl.BlockSpec(wb.shape, const2),
            pl.BlockSpec(wc.shape, const2),
            pl.BlockSpec(wo.shape, const2),
        ],
        out_specs=pl.BlockSpec((1, tm, d), lambda b, i: (b, i, 0)),
        out_shape=jax.ShapeDtypeStruct((bsz, seq, d), F32),
        scratch_shapes=[pltpu.VMEM((tm, B_WIDTH), BF16)],
        compiler_params=pltpu.CompilerParams(
            dimension_semantics=("parallel", "parallel"), vmem_limit_bytes=VMEM_LIMIT_BYTES),
        name="merge",
    )(x, ya, pm, pm, yc, gates, lg, lb, ws, bs_full, wa, wb, wc, wo)


def _ffn_kernel(x_ref, g_ref, w1_ref, w2_ref, fg_ref, o_ref, h_s, acc_s, *, final_norm):
    j = pl.program_id(1)

    @pl.when(j == 0)
    def _():
        x = x_ref[...]
        ms = jnp.mean(x * x, axis=-1, keepdims=True)
        h_s[...] = (x * lax.rsqrt(ms + EPS) * g_ref[...]).astype(BF16)
        acc_s[...] = jnp.zeros_like(acc_s)

    a = jnp.maximum(_dot(h_s[...], w1_ref[...]), 0.0)
    acc_s[...] += _dot((a * a).astype(BF16), w2_ref[...])

    @pl.when(j == pl.num_programs(1) - 1)
    def _():
        y = x_ref[...] + acc_s[...]
        if final_norm:
            ms = jnp.mean(y * y, axis=-1, keepdims=True)
            y = y * lax.rsqrt(ms + EPS) * fg_ref[...]
        o_ref[...] = y


def _ffn(x2d, g, w1, w2, fg, tm, tf, final_norm):
    m, d = x2d.shape
    dff = w1.shape[1]
    grid = (m // tm, dff // tf)
    return pl.pallas_call(
        functools.partial(_ffn_kernel, final_norm=final_norm),
        grid=grid,
        in_specs=[
            pl.BlockSpec((tm, d), lambda i, j: (i, 0)),
            pl.BlockSpec((1, d), lambda i, j: (0, 0)),
            pl.BlockSpec((d, tf), lambda i, j: (0, j)),
            pl.BlockSpec((tf, d), lambda i, j: (j, 0)),
            pl.BlockSpec((1, d), lambda i, j: (0, 0)),
        ],
        out_specs=pl.BlockSpec((tm, d), lambda i, j: (i, 0)),
        out_shape=jax.ShapeDtypeStruct((m, d), F32),
        scratch_shapes=[pltpu.VMEM((tm, d), BF16), pltpu.VMEM((tm, d), F32)],
        compiler_params=pltpu.CompilerParams(
            dimension_semantics=("parallel", "arbitrary"), vmem_limit_bytes=VMEM_LIMIT_BYTES),
        name="ffn",
    )(x2d, g, w1, w2, fg)


def _tile(n, pref):
    t = min(n, pref)
    assert n % t == 0, (n, t)
    return t


def kernel(x, attn_norm_g, w_in, idx_k_norm_g, idx_k_norm_b, sgu_norm_g, sgu_norm_b, sgu_w_s, sgu_b_s,
           diff_lambda, diff_subln_g, w_branch_a, w_branch_b, w_branch_c, w_out, mlp_norm_g, w_ff1,
           w_ff2, final_norm_g):
    bsz, seq, d = x.shape
    depth = w_in.shape[0]
    offs = [0]
    for s in IN_SIZES:
        offs.append(offs[-1] + s)
    (o_aq, o_ak, o_av, o_iq, o_ik, o_iw, o_buv, o_cq, o_ck, o_cv, o_g, o_end) = offs

    tm_proj = _tile(seq, 512)
    tq = _tile(seq, 256)
    tm_merge = _tile(seq, 512)
    tm_ffn = _tile(bsz * seq, 1024)
    tf = _tile(w_ff1.shape[2], 512)

    for l in range(depth):
        lambda_init = 0.8 - 0.6 * math.exp(-0.3 * l)
        w = w_in[l]
        cols = lambda a, b: w[:, a:b]
        wm = jnp.concatenate([cols(o_aq, o_ak), cols(o_av, o_iq), cols(o_iq, o_ik),
                              cols(o_buv, o_cq), cols(o_cq, o_ck), cols(o_cv, o_g)], axis=1).astype(BF16)
        wkt = jnp.concatenate([cols(o_ak, o_av), cols(o_ck, o_cv), cols(o_ik, o_iw)], axis=1).T.astype(BF16)
        wiw = jnp.pad(cols(o_iw, o_buv), ((0, 0), (0, LANES - IDX_HEADS))).astype(BF16)
        wg = cols(o_g, o_end).astype(BF16)

        pm, akt, ckt, ikt, iw, gates = _proj(
            x, attn_norm_g[l][None, :], wm, wkt, wiw, wg,
            idx_k_norm_g[l][:, None], idx_k_norm_b[l][:, None], tm_proj)

        ya = _dsa(pm, ikt, akt, iw, tq)
        sg_full = jnp.tile(diff_subln_g[l], C_HEADS)[None, :]
        yc = _diff(pm, ckt, diff_lambda[l], sg_full, tq, lambda_init)

        bs_full = jnp.repeat(sgu_b_s[l].T, B_WIDTH // B_GROUPS, axis=1)
        x = _merge(x, pm, ya, yc, gates, sgu_norm_g[l][None, :], sgu_norm_b[l][None, :], sgu_w_s[l],
                   bs_full, w_branch_a[l].astype(BF16), w_branch_b[l].astype(BF16),
                   w_branch_c[l].astype(BF16), w_out[l].astype(BF16), tm_merge)

        x = _ffn(x.reshape(bsz * seq, d), mlp_norm_g[l][None, :], w_ff1[l].astype(BF16),
                 w_ff2[l].astype(BF16), final_norm_g[None, :], tm_ffn, tf,
                 final_norm=(l == depth - 1)).reshape(bsz, seq, d)
    return x
```

```python
import functools
import math

import jax
import jax.numpy as jnp
from jax import lax
from jax.experimental import pallas as pl
from jax.experimental.pallas import tpu as pltpu

F32 = jnp.float32
BF16 = jnp.bfloat16
I32 = jnp.int32

A_HEADS = 4
A_HEAD_DIM = 64
IDX_HEADS = 4
IDX_DIM = 64
TOPK_MAX = 256
B_GROUPS = 4
B_WIDTH = 256
CHUNK = 128
C_HEADS = 4
C_HEAD_DIM = 32
C_V_DIM = 64
WIDTH = 256
EPS = 1e-6
NEG = -1e30
INT_MIN = -(2 ** 31)

LANES = 128
SUBLANES = 8
VMEM_LIMIT_BYTES = 56 * 1024 * 1024

IN_SIZES = (256, 256, 256, 256, 64, 4, 512, 256, 256, 256, 3072)
N_COUNT_ACC = 4


def _nt_dot(a, b):
    return lax.dot_general(a, b, (((1,), (1,)), ((), ())), preferred_element_type=F32)


def _dot(a, b):
    return jnp.dot(a, b, preferred_element_type=F32)


def _proj_kernel(x_ref, g_ref, wn_ref, wik_ref, wg_ref, wt_ref, ikg_ref, ikb_ref,
                 pn_ref, ikn_ref, gt_ref, tt_ref, iwt_ref, *, gate_chunk):
    x = x_ref[0]
    ms = jnp.mean(x * x, axis=-1, keepdims=True)
    h = (x * lax.rsqrt(ms + EPS) * g_ref[...]).astype(BF16)
    for c0 in range(0, wn_ref.shape[1], 256):
        pn_ref[0, :, c0:c0 + 256] = _dot(h, wn_ref[:, c0:c0 + 256]).astype(BF16)
    for c0 in range(0, wg_ref.shape[1], gate_chunk):
        gt_ref[0, :, c0:c0 + gate_chunk] = _dot(h, wg_ref[:, c0:c0 + gate_chunk]).astype(BF16)
    n_t = tt_ref.shape[1]
    for r0 in range(0, n_t, 256):
        tt_ref[0, r0:r0 + 256, :] = _nt_dot(wt_ref[r0:r0 + 256, :], h).astype(BF16)
    iwt_ref[0] = _nt_dot(wt_ref[n_t:n_t + SUBLANES, :], h)
    ik = _dot(h, wik_ref[...])
    mu = jnp.mean(ik, axis=-1, keepdims=True)
    xc = ik - mu
    var = jnp.mean(xc * xc, axis=-1, keepdims=True)
    ikn_ref[0] = (xc * lax.rsqrt(var + EPS) * ikg_ref[...] + ikb_ref[...]).astype(BF16)


def _proj(x, g, wn, wik, wg, wt, ikg, ikb, tm):
    bsz, seq, d = x.shape
    n_nat, n_gate, n_t = wn.shape[1], wg.shape[1], wt.shape[0] - SUBLANES
    grid = (bsz, seq // tm)
    const = lambda b, i: (0, 0)
    return pl.pallas_call(
        functools.partial(_proj_kernel, gate_chunk=min(n_gate, 768)),
        grid=grid,
        in_specs=[
            pl.BlockSpec((1, tm, d), lambda b, i: (b, i, 0)),
            pl.BlockSpec((1, d), const),
            pl.BlockSpec(wn.shape, const),
            pl.BlockSpec(wik.shape, const),
            pl.BlockSpec(wg.shape, const),
            pl.BlockSpec(wt.shape, const),
            pl.BlockSpec(ikg.shape, const),
            pl.BlockSpec(ikb.shape, const),
        ],
        out_specs=[
            pl.BlockSpec((1, tm, n_nat), lambda b, i: (b, i, 0)),
            pl.BlockSpec((1, tm, IDX_DIM), lambda b, i: (b, i, 0)),
            pl.BlockSpec((1, tm, n_gate), lambda b, i: (b, i, 0)),
            pl.BlockSpec((1, n_t, tm), lambda b, i: (b, 0, i)),
            pl.BlockSpec((1, SUBLANES, tm), lambda b, i: (b, 0, i)),
        ],
        out_shape=[
            jax.ShapeDtypeStruct((bsz, seq, n_nat), BF16),
            jax.ShapeDtypeStruct((bsz, seq, IDX_DIM), BF16),
            jax.ShapeDtypeStruct((bsz, seq, n_gate), BF16),
            jax.ShapeDtypeStruct((bsz, n_t, seq), BF16),
            jax.ShapeDtypeStruct((bsz, SUBLANES, seq), F32),
        ],
        compiler_params=pltpu.CompilerParams(
            dimension_semantics=("parallel", "parallel"), vmem_limit_bytes=VMEM_LIMIT_BYTES),
        name="proj",
    )(x, g, wn, wik, wg, wt, ikg, ikb)


def _dsa_kernel(qt_ref, iqt_ref, iwt_ref, ikn_ref, k_ref, vt_ref, o_ref, key_s, qm_s, acc_s, ml_s, ib_s,
                *, tq, top_k, seq):
    i = pl.program_id(1)
    kc = tq
    n_chunks = i + 1
    n_grp = kc // SUBLANES
    q_pos = i * tq + lax.broadcasted_iota(I32, (1, tq), 1)
    sub_iota = lax.broadcasted_iota(I32, (SUBLANES, tq), 0)

    w_all = iwt_ref[0] * (IDX_DIM ** -0.5 * IDX_HEADS ** -0.5)
    w_rows = [w_all[h:h + 1, :] for h in range(IDX_HEADS)]
    for h in range(IDX_HEADS):
        qm_s[0:IDX_DIM, h * tq:(h + 1) * tq] = iqt_ref[0, h * IDX_DIM:(h + 1) * IDX_DIM, :]

    def score_body(c, carry):
        off = pl.multiple_of(c * kc, kc)
        ik = ikn_ref[0, pl.ds(off, kc), :]
        d = _dot(ik, qm_s[0:IDX_DIM, :])
        sc = jnp.zeros((kc, tq), F32)
        for h in range(IDX_HEADS):
            sc = sc + jnp.maximum(d[:, h * tq:(h + 1) * tq], 0.0) * w_rows[h]
        key_pos = off + lax.broadcasted_iota(I32, (kc, tq), 0)
        sc = jnp.where(key_pos <= q_pos, sc, -jnp.inf)
        bits = pltpu.bitcast(sc, I32)
        key_s[pl.ds(off, kc), :] = bits ^ ((bits >> 31) & 0x7FFFFFFF)
        return carry

    lax.fori_loop(0, n_chunks, score_body, 0)

    def count(pred):
        def body(c, accs):
            off = pl.multiple_of(c * kc, kc)
            slab = key_s[pl.ds(off, kc), :]
            accs = list(accs)
            for r in range(n_grp):
                kk = slab[r * SUBLANES:(r + 1) * SUBLANES, :]
                accs[r % N_COUNT_ACC] = accs[r % N_COUNT_ACC] + pred(kk, off + r * SUBLANES)
            return tuple(accs)
        zero = jnp.zeros((SUBLANES, tq), I32)
        accs = lax.fori_loop(0, n_chunks, body, (zero,) * N_COUNT_ACC)
        tot = accs[0]
        for a in accs[1:]:
            tot = tot + a
        return jnp.sum(tot, axis=0, keepdims=True)

    def count_ge(cand):
        cb = jnp.broadcast_to(cand, (SUBLANES, tq))
        return count(lambda kk, base: jnp.where(kk >= cb, 1, 0))

    c0 = count_ge(jnp.zeros((1, tq), I32))
    thr = jnp.where(c0 >= top_k, 0, INT_MIN).astype(I32)
    cge = jnp.where(c0 >= top_k, c0, n_chunks * kc)

    def bit_body(j, carry):
        thr, cge = carry
        cand = thr | (jnp.int32(1) << (30 - j))
        cnt = count_ge(cand)
        ok = cnt >= top_k
        return jnp.where(ok, cand, thr), jnp.where(ok, cnt, cge)

    thr, cge = lax.fori_loop(0, 31, bit_body, (thr, cge))
    thr_b = jnp.broadcast_to(thr, (SUBLANES, tq))

    ib_s[0:1, :] = jnp.full((1, tq), seq, I32)

    @pl.when(jnp.max(cge) > top_k)
    def _():
        c_gt = count(lambda kk, base: jnp.where(kk > thr_b, 1, 0))
        need = top_k - c_gt

        def idx_body(j, m):
            cand = m | (jnp.int32(1) << (11 - j))
            cb = jnp.broadcast_to(cand, (SUBLANES, tq))
            cnt = count(lambda kk, base: jnp.where(kk == thr_b,
                                                   jnp.where(sub_iota + base < cb, 1, 0), 0))
            return jnp.where(cnt < need, cand, m)

        n_bits = max(1, (seq - 1).bit_length())
        m = lax.fori_loop(12 - n_bits, 12, idx_body, jnp.zeros((1, tq), I32))
        ib_s[0:1, :] = m + 1

    idx_bound = jnp.broadcast_to(ib_s[0:1, :], (SUBLANES, tq))

    def bias_body(c, carry):
        off = pl.multiple_of(c * kc, kc)
        slab = key_s[pl.ds(off, kc), :]
        out = []
        for r in range(n_grp):
            kk = slab[r * SUBLANES:(r + 1) * SUBLANES, :]
            key_pos = sub_iota + (off + r * SUBLANES)
            keep_tie = jnp.where(key_pos < idx_bound, 0.0, NEG)
            b = jnp.where(kk > thr_b, 0.0, jnp.where(kk == thr_b, keep_tie, NEG))
            out.append(jnp.where(key_pos <= q_pos, b, NEG))
        key_s[pl.ds(off, kc), :] = pltpu.bitcast(jnp.concatenate(out, axis=0).astype(F32), I32)
        return carry

    lax.fori_loop(0, n_chunks, bias_body, 0)

    feat = lax.broadcasted_iota(I32, (WIDTH, tq), 0)
    qt = qt_ref[0] * (A_HEAD_DIM ** -0.5)
    for h in range(A_HEADS):
        qm_s[:, h * tq:(h + 1) * tq] = jnp.where(feat // A_HEAD_DIM == h, qt, jnp.zeros_like(qt))
    acc_s[...] = jnp.zeros_like(acc_s)
    ml_s[0:1, :] = jnp.full((1, A_HEADS * tq), NEG, F32)
    ml_s[1:2, :] = jnp.zeros((1, A_HEADS * tq), F32)

    def attn_body(c, carry):
        off = pl.multiple_of(c * kc, kc)
        bias = pltpu.bitcast(key_s[pl.ds(off, kc), :], F32)
        k_c = k_ref[0, pl.ds(off, kc), :]
        s = _dot(k_c, qm_s[...]) + jnp.concatenate([bias] * A_HEADS, axis=1)
        m_old = ml_s[0:1, :]
        m_new = jnp.maximum(m_old, jnp.max(s, axis=0, keepdims=True))
        alpha = jnp.exp(m_old - m_new)
        p = jnp.exp(s - m_new)
        ml_s[1:2, :] = alpha * ml_s[1:2, :] + jnp.sum(p, axis=0, keepdims=True)
        ml_s[0:1, :] = m_new
        pb = p.astype(BF16)
        for h in range(A_HEADS):
            rows = slice(h * A_HEAD_DIM, (h + 1) * A_HEAD_DIM)
            cols = slice(h * tq, (h + 1) * tq)
            pv = _dot(vt_ref[0, rows, pl.ds(off, kc)], pb[:, cols])
            acc_s[rows, :] = acc_s[rows, :] * alpha[:, cols] + pv
        return carry

    lax.fori_loop(0, n_chunks, attn_body, 0)
    for h in range(A_HEADS):
        rows = slice(h * A_HEAD_DIM, (h + 1) * A_HEAD_DIM)
        acc_s[rows, :] = acc_s[rows, :] * (1.0 / ml_s[1:2, h * tq:(h + 1) * tq])
    o_ref[0] = acc_s[...].T.astype(BF16)


def _dsa(tt, pn, ikn, iwt, tq):
    bsz, seq, _ = pn.shape
    top_k = min(TOPK_MAX, seq // 4)
    grid = (bsz, seq // tq)
    return pl.pallas_call(
        functools.partial(_dsa_kernel, tq=tq, top_k=top_k, seq=seq),
        grid=grid,
        in_specs=[
            pl.BlockSpec((1, WIDTH, tq), lambda b, i: (b, 0, i)),
            pl.BlockSpec((1, WIDTH, tq), lambda b, i: (b, 2, i)),
            pl.BlockSpec((1, SUBLANES, tq), lambda b, i: (b, 0, i)),
            pl.BlockSpec((1, seq, IDX_DIM), lambda b, i: (b, 0, 0)),
            pl.BlockSpec((1, seq, WIDTH), lambda b, i: (b, 0, 0)),
            pl.BlockSpec((1, WIDTH, seq), lambda b, i: (b, 1, 0)),
        ],
        out_specs=pl.BlockSpec((1, tq, WIDTH), lambda b, i: (b, i, 0)),
        out_shape=jax.ShapeDtypeStruct((bsz, seq, WIDTH), BF16),
        scratch_shapes=[pltpu.VMEM((seq, tq), I32),
                        pltpu.VMEM((WIDTH, A_HEADS * tq), BF16),
                        pltpu.VMEM((WIDTH, tq), F32),
                        pltpu.VMEM((2, A_HEADS * tq), F32),
                        pltpu.VMEM((SUBLANES, tq), I32)],
        compiler_params=pltpu.CompilerParams(
            dimension_semantics=("parallel", "arbitrary"), vmem_limit_bytes=VMEM_LIMIT_BYTES),
        name="dsa",
    )(tt, tt, iwt, ikn, pn, tt)


def _diff_kernel(qt_ref, k_ref, vt_ref, lam_ref, sg_ref, o_ref, qm_s, acc_s, ml_s, *, tq, lambda_init):
    i = pl.program_id(1)
    kc = tq
    n_pairs = 2 * C_HEADS
    scale = C_HEAD_DIM ** -0.5
    grp = n_pairs // 2
    feat = lax.broadcasted_iota(I32, (WIDTH, tq), 0)
    qt = qt_ref[0]
    for p_i in range(n_pairs):
        qm_s[p_i // grp, :, (p_i % grp) * tq:(p_i % grp + 1) * tq] = jnp.where(
            feat // C_HEAD_DIM == p_i, qt, jnp.zeros_like(qt))
    acc_s[...] = jnp.zeros_like(acc_s)
    ml_s[0:1, :] = jnp.full((1, n_pairs * tq), NEG, F32)
    ml_s[1:2, :] = jnp.zeros((1, n_pairs * tq), F32)
    diag_ok = lax.broadcasted_iota(I32, (kc, tq), 0) <= lax.broadcasted_iota(I32, (kc, tq), 1)
    diag_ok = jnp.concatenate([diag_ok] * grp, axis=1)

    def chunk(c, masked):
        off = pl.multiple_of(c * kc, kc)
        k_c = k_ref[0, pl.ds(off, kc), :]
        for g in range(2):
            st = slice(g * grp * tq, (g + 1) * grp * tq)
            s = _dot(k_c, qm_s[g]) * scale
            if masked:
                s = jnp.where(diag_ok, s, NEG)
            m_old = ml_s[0:1, st]
            m_new = jnp.maximum(m_old, jnp.max(s, axis=0, keepdims=True))
            alpha = jnp.exp(m_old - m_new)
            p = jnp.exp(s - m_new)
            ml_s[1:2, st] = alpha * ml_s[1:2, st] + jnp.sum(p, axis=0, keepdims=True)
            ml_s[0:1, st] = m_new
            pb = p.astype(BF16)
            for j in range(grp):
                h, comp = (g * grp + j) // 2, (g * grp + j) % 2
                rows = slice(h * C_V_DIM, (h + 1) * C_V_DIM)
                cols = slice(j * tq, (j + 1) * tq)
                pv = _dot(vt_ref[0, rows, pl.ds(off, kc)], pb[:, cols])
                acc_s[comp, rows, :] = acc_s[comp, rows, :] * alpha[:, cols] + pv

    def body(c, carry):
        chunk(c, False)
        return carry

    lax.fori_loop(0, i, body, 0)
    chunk(i, True)

    lq = lam_ref[...]
    lam = (jnp.exp(jnp.sum(lq[0:1] * lq[1:2], axis=1, keepdims=True))
           - jnp.exp(jnp.sum(lq[2:3] * lq[3:4], axis=1, keepdims=True)) + lambda_init)
    for h in range(C_HEADS):
        rows = slice(h * C_V_DIM, (h + 1) * C_V_DIM)
        inv1 = 1.0 / ml_s[1:2, (2 * h) * tq:(2 * h + 1) * tq]
        inv2 = 1.0 / ml_s[1:2, (2 * h + 1) * tq:(2 * h + 2) * tq]
        out = acc_s[0, rows, :] * inv1 - lam * (acc_s[1, rows, :] * inv2)
        rs = lax.rsqrt(jnp.mean(out * out, axis=0, keepdims=True) + EPS)
        acc_s[0, rows, :] = out * rs * sg_ref[...] * (1.0 - lambda_init)
    o_ref[0] = acc_s[0].T.astype(BF16)


def _diff(tt, pn, lam_p, sg_col, tq, lambda_init):
    bsz, seq, _ = pn.shape
    grid = (bsz, seq // tq)
    n_pairs = 2 * C_HEADS
    return pl.pallas_call(
        functools.partial(_diff_kernel, tq=tq, lambda_init=lambda_init),
        grid=grid,
        in_specs=[
            pl.BlockSpec((1, WIDTH, tq), lambda b, i: (b, 3, i)),
            pl.BlockSpec((1, seq, WIDTH), lambda b, i: (b, 0, 3)),
            pl.BlockSpec((1, WIDTH, seq), lambda b, i: (b, 4, 0)),
            pl.BlockSpec(lam_p.shape, lambda b, i: (0, 0)),
            pl.BlockSpec(sg_col.shape, lambda b, i: (0, 0)),
        ],
        out_specs=pl.BlockSpec((1, tq, WIDTH), lambda b, i: (b, i, 0)),
        out_shape=jax.ShapeDtypeStruct((bsz, seq, WIDTH), BF16),
        scratch_shapes=[pltpu.VMEM((2, WIDTH, (n_pairs // 2) * tq), BF16),
                        pltpu.VMEM((2, WIDTH, tq), F32),
                        pltpu.VMEM((2, n_pairs * tq), F32)],
        compiler_params=pltpu.CompilerParams(
            dimension_semantics=("parallel", "arbitrary"), vmem_limit_bytes=VMEM_LIMIT_BYTES),
        name="diff",
    )(tt, pn, tt, lam_p, sg_col)


def _gelu_tanh(x):
    return x * (0.5 * (1.0 + jnp.tanh(math.sqrt(2.0 / math.pi) * (x + 0.044715 * (x * x * x)))))


def _sigmoid(x):
    return 1.0 / (1.0 + jnp.exp(-x))


def _merge_kernel(x_ref, ya_ref, bu_ref, bv_ref, yc_ref, gt_ref, lg_ref, lb_ref, ws_ref, bs_ref,
                  wa_ref, wb_ref, wc_ref, wo_ref, o_ref, yb_s, *, tm):
    d = x_ref.shape[-1]
    lane_w = lax.broadcasted_iota(I32, (CHUNK, B_WIDTH), 1)
    group_of_lane = lane_w // (B_WIDTH // B_GROUPS)
    r_i = lax.broadcasted_iota(I32, (CHUNK, CHUNK), 0)
    c_i = lax.broadcasted_iota(I32, (CHUNK, CHUNK), 1)
    w_tril = [jnp.where(c_i <= r_i, ws_ref[g], 0.0).astype(BF16) for g in range(B_GROUPS)]
    for c in range(tm // CHUNK):
        rows = slice(c * CHUNK, (c + 1) * CHUNK)
        u = _gelu_tanh(bu_ref[0, rows, :].astype(F32))
        v = _gelu_tanh(bv_ref[0, rows, :].astype(F32))
        mu = jnp.mean(v, axis=-1, keepdims=True)
        vc = v - mu
        var = jnp.mean(vc * vc, axis=-1, keepdims=True)
        vn = (vc * lax.rsqrt(var + EPS) * lg_ref[...] + lb_ref[...]).astype(BF16)
        s = bs_ref[...]
        for g in range(B_GROUPS):
            s = s + _dot(w_tril[g], jnp.where(group_of_lane == g, vn, jnp.zeros_like(vn)))
        yb_s[rows, :] = (u * s).astype(BF16)

    merged = jnp.zeros((tm, d), F32)
    branches = ((ya_ref[0], wa_ref), (yb_s[...], wb_ref), (yc_ref[0], wc_ref))
    for n, (y, w_ref) in enumerate(branches):
        gate = _sigmoid(gt_ref[0, :, n * d:(n + 1) * d].astype(F32))
        merged = merged + gate * _dot(y, w_ref[...])
    o_ref[0] = x_ref[0] + _dot(merged.astype(BF16), wo_ref[...])


def _merge(x, pn, ya, yc, gates, lg, lb, ws, bs_full, wa, wb, wc, wo, tm):
    bsz, seq, d = x.shape
    grid = (bsz, seq // tm)
    const2 = lambda b, i: (0, 0)
    return pl.pallas_call(
        functools.partial(_merge_kernel, tm=tm),
        grid=grid,
        in_specs=[
            pl.BlockSpec((1, tm, d), lambda b, i: (b, i, 0)),
            pl.BlockSpec((1, tm, WIDTH), lambda b, i: (b, i, 0)),
            pl.BlockSpec((1, tm, WIDTH), lambda b, i: (b, i, 1)),
            pl.BlockSpec((1, tm, WIDTH), lambda b, i: (b, i, 2)),
            pl.BlockSpec((1, tm, WIDTH), lambda b, i: (b, i, 0)),
            pl.BlockSpec((1, tm, 3 * d), lambda b, i: (b, i, 0)),
            pl.BlockSpec(lg.shape, const2),
            pl.BlockSpec(lb.shape, const2),
            pl.BlockSpec(ws.shape, lambda b, i: (0, 0, 0)),
            pl.BlockSpec(bs_full.shape, const2),
            pl.BlockSpec(wa.shape, const2),
            pl.BlockSpec(wb.shape, const2),
            pl.BlockSpec(wc.shape, const2),
            pl.BlockSpec(wo.shape, const2),
        ],
        out_specs=pl.BlockSpec((1, tm, d), lambda b, i: (b, i, 0)),
        out_shape=jax.ShapeDtypeStruct((bsz, seq, d), F32),
        scratch_shapes=[pltpu.VMEM((tm, B_WIDTH), BF16)],
        compiler_params=pltpu.CompilerParams(
            dimension_semantics=("parallel", "parallel"), vmem_limit_bytes=VMEM_LIMIT_BYTES),
        name="merge",
    )(x, ya, pn, pn, yc, gates, lg, lb, ws, bs_full, wa, wb, wc, wo)


def _ffn_kernel(x_ref, g_ref, w1_ref, w2_ref, fg_ref, o_ref, h_s, acc_s, *, final_norm):
    j = pl.program_id(1)

    @pl.when(j == 0)
    def _():
        x = x_ref[...]
        ms = jnp.mean(x * x, axis=-1, keepdims=True)
        h_s[...] = (x * lax.rsqrt(ms + EPS) * g_ref[...]).astype(BF16)
        acc_s[...] = jnp.zeros_like(acc_s)

    a = jnp.maximum(_dot(h_s[...], w1_ref[...]), 0.0)
    acc_s[...] += _dot((a * a).astype(BF16), w2_ref[...])

    @pl.when(j == pl.num_programs(1) - 1)
    def _():
        y = x_ref[...] + acc_s[...]
        if final_norm:
            ms = jnp.mean(y * y, axis=-1, keepdims=True)
            y = y * lax.rsqrt(ms + EPS) * fg_ref[...]
        o_ref[...] = y


def _ffn(x2d, g, w1, w2, fg, tm, tf, final_norm):
    m, d = x2d.shape
    dff = w1.shape[1]
    grid = (m // tm, dff // tf)
    return pl.pallas_call(
        functools.partial(_ffn_kernel, final_norm=final_norm),
        grid=grid,
        in_specs=[
            pl.BlockSpec((tm, d), lambda i, j: (i, 0)),
            pl.BlockSpec((1, d), lambda i, j: (0, 0)),
            pl.BlockSpec((d, tf), lambda i, j: (0, j)),
            pl.BlockSpec((tf, d), lambda i, j: (j, 0)),
            pl.BlockSpec((1, d), lambda i, j: (0, 0)),
        ],
        out_specs=pl.BlockSpec((tm, d), lambda i, j: (i, 0)),
        out_shape=jax.ShapeDtypeStruct((m, d), F32),
        scratch_shapes=[pltpu.VMEM((tm, d), BF16), pltpu.VMEM((tm, d), F32)],
        compiler_params=pltpu.CompilerParams(
            dimension_semantics=("parallel", "arbitrary"), vmem_limit_bytes=VMEM_LIMIT_BYTES),
        name="ffn",
    )(x2d, g, w1, w2, fg)


def _tile(n, pref):
    t = min(n, pref)
    assert n % t == 0, (n, t)
    return t


def kernel(x, attn_norm_g, w_in, idx_k_norm_g, idx_k_norm_b, sgu_norm_g, sgu_norm_b, sgu_w_s, sgu_b_s,
           diff_lambda, diff_subln_g, w_branch_a, w_branch_b, w_branch_c, w_out, mlp_norm_g, w_ff1,
           w_ff2, final_norm_g):
    bsz, seq, d = x.shape
    depth = w_in.shape[0]
    offs = [0]
    for s in IN_SIZES:
        offs.append(offs[-1] + s)
    (o_aq, o_ak, o_av, o_iq, o_ik, o_iw, o_buv, o_cq, o_ck, o_cv, o_g, o_end) = offs

    tm_proj = _tile(seq, 512)
    tq = _tile(seq, 256)
    tm_merge = _tile(seq, 512)
    tm_ffn = _tile(bsz * seq, 1024)
    tf = _tile(w_ff1.shape[2], 512)

    for l in range(depth):
        lambda_init = 0.8 - 0.6 * math.exp(-0.3 * l)
        w = w_in[l]
        cols = lambda a, b: w[:, a:b]
        wn = jnp.concatenate([cols(o_ak, o_av), cols(o_buv, o_cq), cols(o_ck, o_cv)], axis=1).astype(BF16)
        wt = jnp.concatenate([cols(o_aq, o_ak), cols(o_av, o_iq), cols(o_iq, o_ik), cols(o_cq, o_ck),
                              cols(o_cv, o_g), cols(o_iw, o_buv),
                              jnp.zeros((d, SUBLANES - IDX_HEADS), w.dtype)], axis=1).T.astype(BF16)
        wik = cols(o_ik, o_iw).astype(BF16)
        wg = cols(o_g, o_end).astype(BF16)

        pn, ikn, gates, tt, iwt = _proj(
            x, attn_norm_g[l][None, :], wn, wik, wg, wt,
            idx_k_norm_g[l][None, :], idx_k_norm_b[l][None, :], tm_proj)

        ya = _dsa(tt, pn, ikn, iwt, tq)
        sg_col = diff_subln_g[l][:, None]
        yc = _diff(tt, pn, diff_lambda[l], sg_col, tq, lambda_init)

        bs_full = jnp.repeat(sgu_b_s[l].T, B_WIDTH // B_GROUPS, axis=1)
        x = _merge(x, pn, ya, yc, gates, sgu_norm_g[l][None, :], sgu_norm_b[l][None, :], sgu_w_s[l],
                   bs_full, w_branch_a[l].astype(BF16), w_branch_b[l].astype(BF16),
                   w_branch_c[l].astype(BF16), w_out[l].astype(BF16), tm_merge)

        x = _ffn(x.reshape(bsz * seq, d), mlp_norm_g[l][None, :], w_ff1[l].astype(BF16),
                 w_ff2[l].astype(BF16), final_norm_g[None, :], tm_ffn, tf,
                 final_norm=(l == depth - 1)).reshape(bsz, seq, d)
    return x
```

```python
import functools
import math

import jax
import jax.numpy as jnp
from jax import lax
from jax.experimental import pallas as pl
from jax.experimental.pallas import tpu as pltpu

F32 = jnp.float32
BF16 = jnp.bfloat16
I32 = jnp.int32
I16 = jnp.int16

A_HEADS = 4
A_HEAD_DIM = 64
IDX_HEADS = 4
IDX_DIM = 64
TOPK_MAX = 256
B_GROUPS = 4
B_WIDTH = 256
CHUNK = 128
C_HEADS = 4
C_HEAD_DIM = 32
C_V_DIM = 64
WIDTH = 256
EPS = 1e-6
NEG = -1e30
INT16_MIN = -(2 ** 15)

LANES = 128
SUBLANES = 8
PACK16_ROWS = 2 * SUBLANES

LOG2E = math.log2(math.e)
VT_HEAD_DIM = 64
VT_HEAD_ROWS = VT_HEAD_DIM + PACK16_ROWS
VT_ROWS = (WIDTH // VT_HEAD_DIM) * VT_HEAD_ROWS
VMEM_LIMIT_BYTES = 56 * 1024 * 1024

IN_SIZES = (256, 256, 256, 256, 64, 4, 512, 256, 256, 256, 3072)
N_COUNT_ACC = 4


def _nt_dot(a, b):
    return lax.dot_general(a, b, (((1,), (1,)), ((), ())), preferred_element_type=F32)


def _dot(a, b):
    return jnp.dot(a, b, preferred_element_type=F32)


def _for_chunks(n, body):
    def pair(j, carry):
        body(2 * j)
        body(2 * j + 1)
        return carry

    lax.fori_loop(0, n // 2, pair, 0)

    @pl.when(n % 2 == 1)
    def _():
        body(n - 1)


def _proj_kernel(x_ref, g_ref, wn_ref, wik_ref, wg_ref, wt_ref, ikg_ref, ikb_ref,
                 pn_ref, ikn_ref, gt_ref, qt_ref, vt_ref, iwt_ref, *, gate_chunk):
    x = x_ref[0]
    tm = x.shape[0]
    ms = jnp.mean(x * x, axis=-1, keepdims=True)
    h = (x * lax.rsqrt(ms + EPS) * g_ref[...]).astype(BF16)
    for c0 in range(0, wn_ref.shape[1], WIDTH):
        pn_ref[0, :, c0:c0 + WIDTH] = _dot(h, wn_ref[:, c0:c0 + WIDTH]).astype(BF16)
    for c0 in range(0, wg_ref.shape[1], gate_chunk):
        gt_ref[0, :, c0:c0 + gate_chunk] = _dot(h, wg_ref[:, c0:c0 + gate_chunk]).astype(BF16)
    for blk, q_scale in enumerate((A_HEAD_DIM ** -0.5 * LOG2E, None, C_HEAD_DIM ** -0.5 * LOG2E)):
        r = _nt_dot(wt_ref[blk * WIDTH:(blk + 1) * WIDTH, :], h)
        if q_scale is not None:
            r = r * q_scale
        qt_ref[0, blk * WIDTH:(blk + 1) * WIDTH, :] = r.astype(BF16)
    for blk in range(2):
        r0 = (3 + blk) * WIDTH
        r = _nt_dot(wt_ref[r0:r0 + WIDTH, :], h).astype(BF16)
        for hd in range(WIDTH // VT_HEAD_DIM):
            o0 = hd * VT_HEAD_ROWS
            vt_ref[0, blk, o0:o0 + VT_HEAD_DIM, :] = r[hd * VT_HEAD_DIM:(hd + 1) * VT_HEAD_DIM, :]
            vt_ref[0, blk, o0 + VT_HEAD_DIM:o0 + VT_HEAD_ROWS, :] = jnp.ones(
                (VT_HEAD_ROWS - VT_HEAD_DIM, tm), BF16)
    iwt_ref[0] = _nt_dot(wt_ref[5 * WIDTH:5 * WIDTH + SUBLANES, :], h)
    ik = _dot(h, wik_ref[...])
    mu = jnp.mean(ik, axis=-1, keepdims=True)
    xc = ik - mu
    var = jnp.mean(xc * xc, axis=-1, keepdims=True)
    ikn_ref[0] = (xc * lax.rsqrt(var + EPS) * ikg_ref[...] + ikb_ref[...]).astype(BF16)


def _proj(x, g, wn, wik, wg, wt, ikg, ikb, tm):
    bsz, seq, d = x.shape
    n_nat, n_gate = wn.shape[1], wg.shape[1]
    assert wt.shape[0] == 5 * WIDTH + SUBLANES
    grid = (bsz, seq // tm)
    const = lambda b, i: (0, 0)
    return pl.pallas_call(
        functools.partial(_proj_kernel, gate_chunk=min(n_gate, 768)),
        grid=grid,
        in_specs=[
            pl.BlockSpec((1, tm, d), lambda b, i: (b, i, 0)),
            pl.BlockSpec((1, d), const),
            pl.BlockSpec(wn.shape, const),
            pl.BlockSpec(wik.shape, const),
            pl.BlockSpec(wg.shape, const),
            pl.BlockSpec(wt.shape, const),
            pl.BlockSpec(ikg.shape, const),
            pl.BlockSpec(ikb.shape, const),
        ],
        out_specs=[
            pl.BlockSpec((1, tm, n_nat), lambda b, i: (b, i, 0)),
            pl.BlockSpec((1, tm, IDX_DIM), lambda b, i: (b, i, 0)),
            pl.BlockSpec((1, tm, n_gate), lambda b, i: (b, i, 0)),
            pl.BlockSpec((1, 3 * WIDTH, tm), lambda b, i: (b, 0, i)),
            pl.BlockSpec((1, 2, VT_ROWS, tm), lambda b, i: (b, 0, 0, i)),
            pl.BlockSpec((1, SUBLANES, tm), lambda b, i: (b, 0, i)),
        ],
        out_shape=[
            jax.ShapeDtypeStruct((bsz, seq, n_nat), BF16),
            jax.ShapeDtypeStruct((bsz, seq, IDX_DIM), BF16),
            jax.ShapeDtypeStruct((bsz, seq, n_gate), BF16),
            jax.ShapeDtypeStruct((bsz, 3 * WIDTH, seq), BF16),
            jax.ShapeDtypeStruct((bsz, 2, VT_ROWS, seq), BF16),
            jax.ShapeDtypeStruct((bsz, SUBLANES, seq), F32),
        ],
        compiler_params=pltpu.CompilerParams(
            dimension_semantics=("parallel", "parallel"), vmem_limit_bytes=VMEM_LIMIT_BYTES),
        name="proj",
    )(x, g, wn, wik, wg, wt, ikg, ikb)


def _dsa_kernel(qt_ref, iqt_ref, iwt_ref, ikn_ref, k_ref, vt_ref, o_ref,
                hi_s, lo_s, bias_s, qm_s, acc_s, m_s, ib_s, *, tq, top_k, seq):
    i = pl.program_id(1)
    kc = tq
    n_chunks = i + 1
    n_grp = kc // PACK16_ROWS
    q_pos = i * tq + lax.broadcasted_iota(I32, (1, tq), 1)
    row_iota = lax.broadcasted_iota(I32, (PACK16_ROWS, tq), 0)
    one16 = jnp.ones((PACK16_ROWS, tq), I16)
    zero16 = jnp.zeros((PACK16_ROWS, tq), I16)

    def rows16(v):
        return jnp.broadcast_to(v, (PACK16_ROWS, tq)).astype(I16)

    w_all = iwt_ref[0] * (IDX_DIM ** -0.5 * IDX_HEADS ** -0.5)
    w_rows = [w_all[h:h + 1, :] for h in range(IDX_HEADS)]
    for h in range(IDX_HEADS):
        qm_s[0:IDX_DIM, h * tq:(h + 1) * tq] = iqt_ref[0, h * IDX_DIM:(h + 1) * IDX_DIM, :]

    def score_body(c, diagonal):
        off = pl.multiple_of(c * kc, kc)
        ik = ikn_ref[0, pl.ds(off, kc), :]
        d = _dot(ik, qm_s[0:IDX_DIM, :])
        sc = jnp.zeros((kc, tq), F32)
        for h in range(IDX_HEADS):
            sc = sc + jnp.maximum(d[:, h * tq:(h + 1) * tq], 0.0) * w_rows[h]
        if diagonal:
            key_pos = off + lax.broadcasted_iota(I32, (kc, tq), 0)
            sc = jnp.where(key_pos <= q_pos, sc, -jnp.inf)
        bits = pltpu.bitcast(sc, I32)
        key = bits ^ ((bits >> 31) & 0x7FFFFFFF)
        hi_s[pl.ds(off, kc), :] = (key >> 16).astype(I16)
        lo_s[pl.ds(off, kc), :] = ((key & 0xFFFF) - 0x8000).astype(I16)

    _for_chunks(i, lambda c: score_body(c, False))
    score_body(i, True)

    def count(pred):
        def body(c, accs):
            off = pl.multiple_of(c * kc, kc)
            hi_slab = hi_s[pl.ds(off, kc), :]
            lo_slab = lo_s[pl.ds(off, kc), :]
            accs = list(accs)
            for r in range(n_grp):
                rows = slice(r * PACK16_ROWS, (r + 1) * PACK16_ROWS)
                accs[r % N_COUNT_ACC] = accs[r % N_COUNT_ACC] + pred(hi_slab[rows, :], lo_slab[rows, :],
                                                                     off + r * PACK16_ROWS)
            return tuple(accs)
        accs = lax.fori_loop(0, n_chunks, body, (zero16,) * N_COUNT_ACC)
        tot = accs[0]
        for a in accs[1:]:
            tot = tot + a
        return jnp.sum(tot.astype(I32), axis=0, keepdims=True)

    def descend(pick, need):
        def count_ge(cand):
            cb = rows16(cand)
            return count(lambda hh, ll, base: jnp.where(pick(hh, ll) >= cb, one16, zero16))
        c0 = count_ge(jnp.zeros((1, tq), I32))
        thr = jnp.where(c0 >= need, 0, INT16_MIN).astype(I32)
        cge = jnp.where(c0 >= need, c0, n_chunks * kc)

        def bit_body(j, carry):
            thr, cge = carry
            cand = thr | (jnp.int32(1) << (14 - j))
            cnt = count_ge(cand)
            ok = cnt >= need
            return jnp.where(ok, cand, thr), jnp.where(ok, cnt, cge)

        return lax.fori_loop(0, 15, bit_body, (thr, cge))

    thr_hi, _ = descend(lambda hh, ll: hh, top_k)
    thr_hi16 = rows16(thr_hi)
    need_lo = top_k - count(lambda hh, ll, base: jnp.where(hh > thr_hi16, one16, zero16))

    def restrict_body(c, carry):
        off = pl.multiple_of(c * kc, kc)
        in_bucket = hi_s[pl.ds(off, kc), :] == jnp.concatenate([thr_hi16] * n_grp, axis=0)
        lo_s[pl.ds(off, kc), :] = jnp.where(in_bucket, lo_s[pl.ds(off, kc), :], jnp.int16(INT16_MIN))
        return carry

    lax.fori_loop(0, n_chunks, restrict_body, 0)
    thr_lo, cge_lo = descend(lambda hh, ll: ll, need_lo)
    thr_lo16 = rows16(thr_lo)

    ib_s[0:1, :] = jnp.full((1, tq), seq, I32)

    @pl.when(jnp.max(cge_lo - need_lo) > 0)
    def _():
        c_gt = count(lambda hh, ll, base: jnp.where(ll > thr_lo16, one16, zero16))
        need = need_lo - c_gt

        def idx_body(j, m):
            cand = m | (jnp.int32(1) << (11 - j))
            cb = rows16(cand)

            def pred(hh, ll, base):
                below = jnp.where((row_iota + base).astype(I16) < cb, one16, zero16)
                return jnp.where(hh == thr_hi16, jnp.where(ll == thr_lo16, below, zero16), zero16)

            return jnp.where(count(pred) < need, cand, m)

        n_bits = max(1, (seq - 1).bit_length())
        m = lax.fori_loop(12 - n_bits, 12, idx_body, jnp.zeros((1, tq), I32))
        ib_s[0:1, :] = m + 1

    bound16 = rows16(ib_s[0:1, :])
    q_pos16 = rows16(q_pos)

    def bias_body(c, diagonal):
        off = pl.multiple_of(c * kc, kc)
        hi_slab = hi_s[pl.ds(off, kc), :]
        lo_slab = lo_s[pl.ds(off, kc), :]
        out = []
        for r in range(n_grp):
            rows = slice(r * PACK16_ROWS, (r + 1) * PACK16_ROWS)
            hh, ll = hi_slab[rows, :], lo_slab[rows, :]
            key_pos = (row_iota + (off + r * PACK16_ROWS)).astype(I16)
            tie = jnp.where(key_pos < bound16, one16, zero16)
            in_bucket = jnp.where(ll > thr_lo16, one16, jnp.where(ll == thr_lo16, tie, zero16))
            sel = jnp.where(hh > thr_hi16, one16, jnp.where(hh == thr_hi16, in_bucket, zero16))
            if diagonal:
                sel = jnp.where(key_pos <= q_pos16, sel, zero16)
            out.append(sel)
        sel_all = jnp.concatenate(out, axis=0).astype(I32)
        bias_s[pl.ds(off, kc), :] = jnp.where(sel_all > 0, 0.0, NEG)

    _for_chunks(i, lambda c: bias_body(c, False))
    bias_body(i, True)

    feat = lax.broadcasted_iota(I32, (WIDTH, tq), 0)
    qt = qt_ref[0]
    for h in range(A_HEADS):
        qm_s[:, h * tq:(h + 1) * tq] = jnp.where(feat // A_HEAD_DIM == h, qt, jnp.zeros_like(qt))
    acc_s[...] = jnp.zeros_like(acc_s)
    m_s[0:1, :] = jnp.full((1, A_HEADS * tq), NEG, F32)

    def attn_body(c):
        off = pl.multiple_of(c * kc, kc)
        bias = bias_s[pl.ds(off, kc), :]
        k_c = k_ref[0, pl.ds(off, kc), :]
        s = _dot(k_c, qm_s[...]) + jnp.concatenate([bias] * A_HEADS, axis=1)
        m_old = m_s[0:1, :]
        m_new = jnp.maximum(m_old, jnp.max(s, axis=0, keepdims=True))
        alpha = jnp.exp2(m_old - m_new)
        pb = jnp.exp2(s - m_new).astype(BF16)
        m_s[0:1, :] = m_new
        for h in range(A_HEADS):
            rows = slice(h * VT_HEAD_ROWS, (h + 1) * VT_HEAD_ROWS)
            cols = slice(h * tq, (h + 1) * tq)
            pv = _dot(vt_ref[0, 0, rows, pl.ds(off, kc)], pb[:, cols])
            acc_s[rows, :] = acc_s[rows, :] * alpha[:, cols] + pv

    _for_chunks(n_chunks, attn_body)
    out = []
    for h in range(A_HEADS):
        r0 = h * VT_HEAD_ROWS
        out.append(acc_s[r0:r0 + VT_HEAD_DIM, :] * (1.0 / acc_s[r0 + VT_HEAD_DIM:r0 + VT_HEAD_DIM + 1, :]))
    o_ref[0] = jnp.concatenate(out, axis=0).T.astype(BF16)


def _dsa(qt, vt, pn, ikn, iwt, tq):
    bsz, seq, _ = pn.shape
    top_k = min(TOPK_MAX, seq // 4)
    grid = (bsz, seq // tq)
    return pl.pallas_call(
        functools.partial(_dsa_kernel, tq=tq, top_k=top_k, seq=seq),
        grid=grid,
        in_specs=[
            pl.BlockSpec((1, WIDTH, tq), lambda b, i: (b, 0, i)),
            pl.BlockSpec((1, WIDTH, tq), lambda b, i: (b, 1, i)),
            pl.BlockSpec((1, SUBLANES, tq), lambda b, i: (b, 0, i)),
            pl.BlockSpec((1, seq, IDX_DIM), lambda b, i: (b, 0, 0)),
            pl.BlockSpec((1, seq, WIDTH), lambda b, i: (b, 0, 0)),
            pl.BlockSpec((1, 1, VT_ROWS, seq), lambda b, i: (b, 0, 0, 0)),
        ],
        out_specs=pl.BlockSpec((1, tq, WIDTH), lambda b, i: (b, i, 0)),
        out_shape=jax.ShapeDtypeStruct((bsz, seq, WIDTH), BF16),
        scratch_shapes=[pltpu.VMEM((seq, tq), I16),
                        pltpu.VMEM((seq, tq), I16),
                        pltpu.VMEM((seq, tq), F32),
                        pltpu.VMEM((WIDTH, A_HEADS * tq), BF16),
                        pltpu.VMEM((VT_ROWS, tq), F32),
                        pltpu.VMEM((SUBLANES, A_HEADS * tq), F32),
                        pltpu.VMEM((SUBLANES, tq), I32)],
        compiler_params=pltpu.CompilerParams(
            dimension_semantics=("parallel", "arbitrary"), vmem_limit_bytes=VMEM_LIMIT_BYTES),
        name="dsa",
    )(qt, qt, iwt, ikn, pn, vt)


def _diff_kernel(qt_ref, k_ref, vt_ref, lam_ref, sg_ref, o_ref, qm_s, acc_s, m_s, *, tq, lambda_init):
    i = pl.program_id(1)
    kc = tq
    n_pairs = 2 * C_HEADS
    grp = n_pairs // 2
    feat = lax.broadcasted_iota(I32, (WIDTH, tq), 0)
    qt = qt_ref[0]
    for p_i in range(n_pairs):
        qm_s[p_i // grp, :, (p_i % grp) * tq:(p_i % grp + 1) * tq] = jnp.where(
            feat // C_HEAD_DIM == p_i, qt, jnp.zeros_like(qt))
    acc_s[...] = jnp.zeros_like(acc_s)
    m_s[0:1, :] = jnp.full((1, n_pairs * tq), NEG, F32)
    diag_ok = lax.broadcasted_iota(I32, (kc, tq), 0) <= lax.broadcasted_iota(I32, (kc, tq), 1)
    diag_ok = jnp.concatenate([diag_ok] * grp, axis=1)

    def chunk(c, masked):
        off = pl.multiple_of(c * kc, kc)
        k_c = k_ref[0, pl.ds(off, kc), :]
        for g in range(2):
            st = slice(g * grp * tq, (g + 1) * grp * tq)
            s = _dot(k_c, qm_s[g])
            if masked:
                s = jnp.where(diag_ok, s, NEG)
            m_old = m_s[0:1, st]
            m_new = jnp.maximum(m_old, jnp.max(s, axis=0, keepdims=True))
            alpha = jnp.exp2(m_old - m_new)
            pb = jnp.exp2(s - m_new).astype(BF16)
            m_s[0:1, st] = m_new
            for j in range(grp):
                h, comp = (g * grp + j) // 2, (g * grp + j) % 2
                rows = slice(h * VT_HEAD_ROWS, (h + 1) * VT_HEAD_ROWS)
                cols = slice(j * tq, (j + 1) * tq)
                pv = _dot(vt_ref[0, 0, rows, pl.ds(off, kc)], pb[:, cols])
                acc_s[comp, rows, :] = acc_s[comp, rows, :] * alpha[:, cols] + pv

    _for_chunks(i, lambda c: chunk(c, False))
    chunk(i, True)

    lq = lam_ref[...]
    lam = (jnp.exp(jnp.sum(lq[0:1] * lq[1:2], axis=1, keepdims=True))
           - jnp.exp(jnp.sum(lq[2:3] * lq[3:4], axis=1, keepdims=True)) + lambda_init)
    outs = []
    for h in range(C_HEADS):
        r0 = h * VT_HEAD_ROWS
        l_row = slice(r0 + VT_HEAD_DIM, r0 + VT_HEAD_DIM + 1)
        a1 = acc_s[0, r0:r0 + VT_HEAD_DIM, :] * (1.0 / acc_s[0, l_row, :])
        a2 = acc_s[1, r0:r0 + VT_HEAD_DIM, :] * (1.0 / acc_s[1, l_row, :])
        out = a1 - lam * a2
        rs = lax.rsqrt(jnp.mean(out * out, axis=0, keepdims=True) + EPS)
        outs.append(out * rs * sg_ref[...] * (1.0 - lambda_init))
    o_ref[0] = jnp.concatenate(outs, axis=0).T.astype(BF16)


def _diff(qt, vt, pn, lam_p, sg_col, tq, lambda_init):
    bsz, seq, _ = pn.shape
    grid = (bsz, seq // tq)
    n_pairs = 2 * C_HEADS
    return pl.pallas_call(
        functools.partial(_diff_kernel, tq=tq, lambda_init=lambda_init),
        grid=grid,
        in_specs=[
            pl.BlockSpec((1, WIDTH, tq), lambda b, i: (b, 2, i)),
            pl.BlockSpec((1, seq, WIDTH), lambda b, i: (b, 0, 3)),
            pl.BlockSpec((1, 1, VT_ROWS, seq), lambda b, i: (b, 1, 0, 0)),
            pl.BlockSpec(lam_p.shape, lambda b, i: (0, 0)),
            pl.BlockSpec(sg_col.shape, lambda b, i: (0, 0)),
        ],
        out_specs=pl.BlockSpec((1, tq, WIDTH), lambda b, i: (b, i, 0)),
        out_shape=jax.ShapeDtypeStruct((bsz, seq, WIDTH), BF16),
        scratch_shapes=[pltpu.VMEM((2, WIDTH, (n_pairs // 2) * tq), BF16),
                        pltpu.VMEM((2, VT_ROWS, tq), F32),
                        pltpu.VMEM((SUBLANES, n_pairs * tq), F32)],
        compiler_params=pltpu.CompilerParams(
            dimension_semantics=("parallel", "arbitrary"), vmem_limit_bytes=VMEM_LIMIT_BYTES),
        name="diff",
    )(qt, pn, vt, lam_p, sg_col)


def _gelu_tanh(x):
    return x * (0.5 * (1.0 + jnp.tanh(math.sqrt(2.0 / math.pi) * (x + 0.044715 * (x * x * x)))))


def _sigmoid(x):
    return 1.0 / (1.0 + jnp.exp(-x))


def _merge_kernel(x_ref, ya_ref, bu_ref, bv_ref, yc_ref, gt_ref, lg_ref, lb_ref, ws_ref, bs_ref,
                  wa_ref, wb_ref, wc_ref, wo_ref, o_ref, yb_s, *, tm):
    d = x_ref.shape[-1]
    lane_w = lax.broadcasted_iota(I32, (CHUNK, B_WIDTH), 1)
    group_of_lane = lane_w // (B_WIDTH // B_GROUPS)
    r_i = lax.broadcasted_iota(I32, (CHUNK, CHUNK), 0)
    c_i = lax.broadcasted_iota(I32, (CHUNK, CHUNK), 1)
    w_tril = [jnp.where(c_i <= r_i, ws_ref[g], 0.0).astype(BF16) for g in range(B_GROUPS)]
    for c in range(tm // CHUNK):
        rows = slice(c * CHUNK, (c + 1) * CHUNK)
        u = _gelu_tanh(bu_ref[0, rows, :].astype(F32))
        v = _gelu_tanh(bv_ref[0, rows, :].astype(F32))
        mu = jnp.mean(v, axis=-1, keepdims=True)
        vc = v - mu
        var = jnp.mean(vc * vc, axis=-1, keepdims=True)
        vn = (vc * lax.rsqrt(var + EPS) * lg_ref[...] + lb_ref[...]).astype(BF16)
        s = bs_ref[...]
        for g in range(B_GROUPS):
            s = s + _dot(w_tril[g], jnp.where(group_of_lane == g, vn, jnp.zeros_like(vn)))
        yb_s[rows, :] = (u * s).astype(BF16)

    merged = jnp.zeros((tm, d), F32)
    branches = ((ya_ref[0], wa_ref), (yb_s[...], wb_ref), (yc_ref[0], wc_ref))
    for n, (y, w_ref) in enumerate(branches):
        gate = _sigmoid(gt_ref[0, :, n * d:(n + 1) * d].astype(F32))
        merged = merged + gate * _dot(y, w_ref[...])
    o_ref[0] = x_ref[0] + _dot(merged.astype(BF16), wo_ref[...])


def _merge(x, pn, ya, yc, gates, lg, lb, ws, bs_full, wa, wb, wc, wo, tm):
    bsz, seq, d = x.shape
    grid = (bsz, seq // tm)
    const2 = lambda b, i: (0, 0)
    return pl.pallas_call(
        functools.partial(_merge_kernel, tm=tm),
        grid=grid,
        in_specs=[
            pl.BlockSpec((1, tm, d), lambda b, i: (b, i, 0)),
            pl.BlockSpec((1, tm, WIDTH), lambda b, i: (b, i, 0)),
            pl.BlockSpec((1, tm, WIDTH), lambda b, i: (b, i, 1)),
            pl.BlockSpec((1, tm, WIDTH), lambda b, i: (b, i, 2)),
            pl.BlockSpec((1, tm, WIDTH), lambda b, i: (b, i, 0)),
            pl.BlockSpec((1, tm, 3 * d), lambda b, i: (b, i, 0)),
            pl.BlockSpec(lg.shape, const2),
            pl.BlockSpec(lb.shape, const2),
            pl.BlockSpec(ws.shape, lambda b, i: (0, 0, 0)),
            pl.BlockSpec(bs_full.shape, const2),
            pl.BlockSpec(wa.shape, const2),
            pl.BlockSpec(wb.shape, const2),
            pl.BlockSpec(wc.shape, const2),
            pl.BlockSpec(wo.shape, const2),
        ],
        out_specs=pl.BlockSpec((1, tm, d), lambda b, i: (b, i, 0)),
        out_shape=jax.ShapeDtypeStruct((bsz, seq, d), F32),
        scratch_shapes=[pltpu.VMEM((tm, B_WIDTH), BF16)],
        compiler_params=pltpu.CompilerParams(
            dimension_semantics=("parallel", "parallel"), vmem_limit_bytes=VMEM_LIMIT_BYTES),
        name="merge",
    )(x, ya, pn, pn, yc, gates, lg, lb, ws, bs_full, wa, wb, wc, wo)


def _ffn_kernel(x_ref, g_ref, w1_ref, w2_ref, fg_ref, o_ref, h_s, acc_s, *, final_norm):
    j = pl.program_id(1)

    @pl.when(j == 0)
    def _():
        x = x_ref[...]
        ms = jnp.mean(x * x, axis=-1, keepdims=True)
        h_s[...] = (x * lax.rsqrt(ms + EPS) * g_ref[...]).astype(BF16)
        acc_s[...] = jnp.zeros_like(acc_s)

    a = jnp.maximum(_dot(h_s[...], w1_ref[...]), 0.0)
    acc_s[...] += _dot((a * a).astype(BF16), w2_ref[...])

    @pl.when(j == pl.num_programs(1) - 1)
    def _():
        y = x_ref[...] + acc_s[...]
        if final_norm:
            ms = jnp.mean(y * y, axis=-1, keepdims=True)
            y = y * lax.rsqrt(ms + EPS) * fg_ref[...]
        o_ref[...] = y


def _ffn(x2d, g, w1, w2, fg, tm, tf, final_norm):
    m, d = x2d.shape
    dff = w1.shape[1]
    grid = (m // tm, dff // tf)
    return pl.pallas_call(
        functools.partial(_ffn_kernel, final_norm=final_norm),
        grid=grid,
        in_specs=[
            pl.BlockSpec((tm, d), lambda i, j: (i, 0)),
            pl.BlockSpec((1, d), lambda i, j: (0, 0)),
            pl.BlockSpec((d, tf), lambda i, j: (0, j)),
            pl.BlockSpec((tf, d), lambda i, j: (j, 0)),
            pl.BlockSpec((1, d), lambda i, j: (0, 0)),
        ],
        out_specs=pl.BlockSpec((tm, d), lambda i, j: (i, 0)),
        out_shape=jax.ShapeDtypeStruct((m, d), F32),
        scratch_shapes=[pltpu.VMEM((tm, d), BF16), pltpu.VMEM((tm, d), F32)],
        compiler_params=pltpu.CompilerParams(
            dimension_semantics=("parallel", "arbitrary"), vmem_limit_bytes=VMEM_LIMIT_BYTES),
        name="ffn",
    )(x2d, g, w1, w2, fg)


def _tile(n, pref):
    t = min(n, pref)
    assert n % t == 0, (n, t)
    return t


def kernel(x, attn_norm_g, w_in, idx_k_norm_g, idx_k_norm_b, sgu_norm_g, sgu_norm_b, sgu_w_s, sgu_b_s,
           diff_lambda, diff_subln_g, w_branch_a, w_branch_b, w_branch_c, w_out, mlp_norm_g, w_ff1,
           w_ff2, final_norm_g):
    bsz, seq, d = x.shape
    depth = w_in.shape[0]
    offs = [0]
    for s in IN_SIZES:
        offs.append(offs[-1] + s)
    (o_aq, o_ak, o_av, o_iq, o_ik, o_iw, o_buv, o_cq, o_ck, o_cv, o_g, o_end) = offs

    tm_proj = _tile(seq, 512)
    tq = _tile(seq, 256)
    tm_merge = _tile(seq, 512)
    tm_ffn = _tile(bsz * seq, 1024)
    tf = _tile(w_ff1.shape[2], 512)

    for l in range(depth):
        lambda_init = 0.8 - 0.6 * math.exp(-0.3 * l)
        w = w_in[l]
        cols = lambda a, b: w[:, a:b]
        wn = jnp.concatenate([cols(o_ak, o_av), cols(o_buv, o_cq), cols(o_ck, o_cv)], axis=1).astype(BF16)
        wt = jnp.concatenate([cols(o_aq, o_ak), cols(o_iq, o_ik), cols(o_cq, o_ck), cols(o_av, o_iq),
                              cols(o_cv, o_g), cols(o_iw, o_buv),
                              jnp.zeros((d, SUBLANES - IDX_HEADS), w.dtype)], axis=1).T.astype(BF16)
        wik = cols(o_ik, o_iw).astype(BF16)
        wg = cols(o_g, o_end).astype(BF16)

        pn, ikn, gates, qt, vt, iwt = _proj(
            x, attn_norm_g[l][None, :], wn, wik, wg, wt,
            idx_k_norm_g[l][None, :], idx_k_norm_b[l][None, :], tm_proj)

        ya = _dsa(qt, vt, pn, ikn, iwt, tq)
        sg_col = diff_subln_g[l][:, None]
        yc = _diff(qt, vt, pn, diff_lambda[l], sg_col, tq, lambda_init)

        bs_full = jnp.repeat(sgu_b_s[l].T, B_WIDTH // B_GROUPS, axis=1)
        x = _merge(x, pn, ya, yc, gates, sgu_norm_g[l][None, :], sgu_norm_b[l][None, :], sgu_w_s[l],
                   bs_full, w_branch_a[l].astype(BF16), w_branch_b[l].astype(BF16),
                   w_branch_c[l].astype(BF16), w_out[l].astype(BF16), tm_merge)

        x = _ffn(x.reshape(bsz * seq, d), mlp_norm_g[l][None, :], w_ff1[l].astype(BF16),
                 w_ff2[l].astype(BF16), final_norm_g[None, :], tm_ffn, tf,
                 final_norm=(l == depth - 1)).reshape(bsz, seq, d)
    return x
```

```python
import functools
import math

import jax
import jax.numpy as jnp
from jax import lax
from jax.experimental import pallas as pl
from jax.experimental.pallas import tpu as pltpu

F32 = jnp.float32
BF16 = jnp.bfloat16
I32 = jnp.int32
I16 = jnp.int16

A_HEADS = 4
A_HEAD_DIM = 64
IDX_HEADS = 4
IDX_DIM = 64
TOPK_MAX = 256
B_GROUPS = 4
B_WIDTH = 256
CHUNK = 128
C_HEADS = 4
C_HEAD_DIM = 32
C_V_DIM = 64
WIDTH = 256
EPS = 1e-6
NEG = -1e30
INT16_MIN = -(2 ** 15)

LANES = 128
SUBLANES = 8
PACK16_ROWS = 2 * SUBLANES

LOG2E = math.log2(math.e)
VT_HEAD_DIM = 64
VT_HEAD_ROWS = VT_HEAD_DIM + PACK16_ROWS
VT_ROWS = (WIDTH // VT_HEAD_DIM) * VT_HEAD_ROWS
VMEM_LIMIT_BYTES = 56 * 1024 * 1024

IN_SIZES = (256, 256, 256, 256, 64, 4, 512, 256, 256, 256, 3072)
N_COUNT_ACC = 4


def _nt_dot(a, b):
    return lax.dot_general(a, b, (((1,), (1,)), ((), ())), preferred_element_type=F32)


def _dot(a, b):
    return jnp.dot(a, b, preferred_element_type=F32)


def _for_chunks(n, body):
    def pair(j, carry):
        body(2 * j)
        body(2 * j + 1)
        return carry

    lax.fori_loop(0, n // 2, pair, 0)

    @pl.when(n % 2 == 1)
    def _():
        body(n - 1)


def _for_spans(n, kc, body):
    def pair(j, carry):
        body(pl.multiple_of(j * (2 * kc), 2 * kc), 2 * kc)
        return carry

    lax.fori_loop(0, n // 2, pair, 0)

    @pl.when(n % 2 == 1)
    def _():
        body(pl.multiple_of((n - 1) * kc, kc), kc)


def _proj_kernel(x_ref, g_ref, wn_ref, wik_ref, wg_ref, wt_ref, ikg_ref, ikb_ref,
                 pn_ref, ikn_ref, gt_ref, qt_ref, vt_ref, iwt_ref, *, gate_chunk):
    x = x_ref[0]
    tm = x.shape[0]
    ms = jnp.mean(x * x, axis=-1, keepdims=True)
    h = (x * lax.rsqrt(ms + EPS) * g_ref[...]).astype(BF16)
    for c0 in range(0, wn_ref.shape[1], WIDTH):
        pn_ref[0, :, c0:c0 + WIDTH] = _dot(h, wn_ref[:, c0:c0 + WIDTH]).astype(BF16)
    for c0 in range(0, wg_ref.shape[1], gate_chunk):
        gt_ref[0, :, c0:c0 + gate_chunk] = _dot(h, wg_ref[:, c0:c0 + gate_chunk]).astype(BF16)
    for blk, q_scale in enumerate((A_HEAD_DIM ** -0.5 * LOG2E, None, C_HEAD_DIM ** -0.5 * LOG2E)):
        r = _nt_dot(wt_ref[blk * WIDTH:(blk + 1) * WIDTH, :], h)
        if q_scale is not None:
            r = r * q_scale
        qt_ref[0, blk * WIDTH:(blk + 1) * WIDTH, :] = r.astype(BF16)
    for blk in range(2):
        r0 = (3 + blk) * WIDTH
        r = _nt_dot(wt_ref[r0:r0 + WIDTH, :], h).astype(BF16)
        for hd in range(WIDTH // VT_HEAD_DIM):
            o0 = hd * VT_HEAD_ROWS
            vt_ref[0, blk, o0:o0 + VT_HEAD_DIM, :] = r[hd * VT_HEAD_DIM:(hd + 1) * VT_HEAD_DIM, :]
            vt_ref[0, blk, o0 + VT_HEAD_DIM:o0 + VT_HEAD_ROWS, :] = jnp.ones(
                (VT_HEAD_ROWS - VT_HEAD_DIM, tm), BF16)
    iwt_ref[0] = _nt_dot(wt_ref[5 * WIDTH:5 * WIDTH + SUBLANES, :], h)
    ik = _dot(h, wik_ref[...])
    mu = jnp.mean(ik, axis=-1, keepdims=True)
    xc = ik - mu
    var = jnp.mean(xc * xc, axis=-1, keepdims=True)
    ikn_ref[0] = (xc * lax.rsqrt(var + EPS) * ikg_ref[...] + ikb_ref[...]).astype(BF16)


def _proj(x, g, wn, wik, wg, wt, ikg, ikb, tm):
    bsz, seq, d = x.shape
    n_nat, n_gate = wn.shape[1], wg.shape[1]
    assert wt.shape[0] == 5 * WIDTH + SUBLANES
    grid = (bsz, seq // tm)
    const = lambda b, i: (0, 0)
    return pl.pallas_call(
        functools.partial(_proj_kernel, gate_chunk=min(n_gate, 768)),
        grid=grid,
        in_specs=[
            pl.BlockSpec((1, tm, d), lambda b, i: (b, i, 0)),
            pl.BlockSpec((1, d), const),
            pl.BlockSpec(wn.shape, const),
            pl.BlockSpec(wik.shape, const),
            pl.BlockSpec(wg.shape, const),
            pl.BlockSpec(wt.shape, const),
            pl.BlockSpec(ikg.shape, const),
            pl.BlockSpec(ikb.shape, const),
        ],
        out_specs=[
            pl.BlockSpec((1, tm, n_nat), lambda b, i: (b, i, 0)),
            pl.BlockSpec((1, tm, IDX_DIM), lambda b, i: (b, i, 0)),
            pl.BlockSpec((1, tm, n_gate), lambda b, i: (b, i, 0)),
            pl.BlockSpec((1, 3 * WIDTH, tm), lambda b, i: (b, 0, i)),
            pl.BlockSpec((1, 2, VT_ROWS, tm), lambda b, i: (b, 0, 0, i)),
            pl.BlockSpec((1, SUBLANES, tm), lambda b, i: (b, 0, i)),
        ],
        out_shape=[
            jax.ShapeDtypeStruct((bsz, seq, n_nat), BF16),
            jax.ShapeDtypeStruct((bsz, seq, IDX_DIM), BF16),
            jax.ShapeDtypeStruct((bsz, seq, n_gate), BF16),
            jax.ShapeDtypeStruct((bsz, 3 * WIDTH, seq), BF16),
            jax.ShapeDtypeStruct((bsz, 2, VT_ROWS, seq), BF16),
            jax.ShapeDtypeStruct((bsz, SUBLANES, seq), F32),
        ],
        compiler_params=pltpu.CompilerParams(
            dimension_semantics=("parallel", "parallel"), vmem_limit_bytes=VMEM_LIMIT_BYTES),
        name="proj",
    )(x, g, wn, wik, wg, wt, ikg, ikb)


def _dsa_kernel(qt_ref, iqt_ref, iwt_ref, ikn_ref, k_ref, vt_ref, o_ref,
                hi_s, lo_s, bias_s, qm_s, acc_s, m_s, ib_s, s0_s, s1_s, lm_s, *, tq, top_k, seq):
    i = pl.program_id(1)
    kc = tq
    n_chunks = i + 1
    n_grp = kc // PACK16_ROWS
    q_pos = i * tq + lax.broadcasted_iota(I32, (1, tq), 1)
    row_iota = lax.broadcasted_iota(I32, (PACK16_ROWS, tq), 0)
    one16 = jnp.ones((PACK16_ROWS, tq), I16)
    zero16 = jnp.zeros((PACK16_ROWS, tq), I16)

    def rows16(v):
        return jnp.broadcast_to(v, (PACK16_ROWS, tq)).astype(I16)

    w_all = iwt_ref[0] * (IDX_DIM ** -0.5 * IDX_HEADS ** -0.5)
    w_rows = [w_all[h:h + 1, :] for h in range(IDX_HEADS)]
    for h in range(IDX_HEADS):
        qm_s[0:IDX_DIM, h * tq:(h + 1) * tq] = iqt_ref[0, h * IDX_DIM:(h + 1) * IDX_DIM, :]

    def score_body(c, diagonal):
        off = pl.multiple_of(c * kc, kc)
        ik = ikn_ref[0, pl.ds(off, kc), :]
        d = _dot(ik, qm_s[0:IDX_DIM, :])
        sc = jnp.zeros((kc, tq), F32)
        for h in range(IDX_HEADS):
            sc = sc + jnp.maximum(d[:, h * tq:(h + 1) * tq], 0.0) * w_rows[h]
        if diagonal:
            key_pos = off + lax.broadcasted_iota(I32, (kc, tq), 0)
            sc = jnp.where(key_pos <= q_pos, sc, -jnp.inf)
        bits = pltpu.bitcast(sc, I32)
        key = bits ^ ((bits >> 31) & 0x7FFFFFFF)
        hi_s[pl.ds(off, kc), :] = (key >> 16).astype(I16)
        lo_s[pl.ds(off, kc), :] = ((key & 0xFFFF) - 0x8000).astype(I16)

    _for_chunks(i, lambda c: score_body(c, False))
    score_body(i, True)

    def count(pred):
        def body(c, accs):
            off = pl.multiple_of(c * kc, kc)
            hi_slab = hi_s[pl.ds(off, kc), :]
            lo_slab = lo_s[pl.ds(off, kc), :]
            accs = list(accs)
            for r in range(n_grp):
                rows = slice(r * PACK16_ROWS, (r + 1) * PACK16_ROWS)
                accs[r % N_COUNT_ACC] = accs[r % N_COUNT_ACC] + pred(hi_slab[rows, :], lo_slab[rows, :],
                                                                     off + r * PACK16_ROWS)
            return tuple(accs)
        accs = lax.fori_loop(0, n_chunks, body, (zero16,) * N_COUNT_ACC)
        tot = accs[0]
        for a in accs[1:]:
            tot = tot + a
        return jnp.sum(tot.astype(I32), axis=0, keepdims=True)

    def descend(pick, need):
        def count_ge(cand):
            cb = rows16(cand)
            return count(lambda hh, ll, base: jnp.where(pick(hh, ll) >= cb, one16, zero16))
        c0 = count_ge(jnp.zeros((1, tq), I32))
        thr = jnp.where(c0 >= need, 0, INT16_MIN).astype(I32)
        cge = jnp.where(c0 >= need, c0, n_chunks * kc)

        def bit_body(j, carry):
            thr, cge = carry
            cand = thr | (jnp.int32(1) << (14 - j))
            cnt = count_ge(cand)
            ok = cnt >= need
            return jnp.where(ok, cand, thr), jnp.where(ok, cnt, cge)

        return lax.fori_loop(0, 15, bit_body, (thr, cge))

    thr_hi, _ = descend(lambda hh, ll: hh, top_k)
    thr_hi16 = rows16(thr_hi)
    need_lo = top_k - count(lambda hh, ll, base: jnp.where(hh > thr_hi16, one16, zero16))

    def restrict_body(c, carry):
        off = pl.multiple_of(c * kc, kc)
        in_bucket = hi_s[pl.ds(off, kc), :] == jnp.concatenate([thr_hi16] * n_grp, axis=0)
        lo_s[pl.ds(off, kc), :] = jnp.where(in_bucket, lo_s[pl.ds(off, kc), :], jnp.int16(INT16_MIN))
        return carry

    lax.fori_loop(0, n_chunks, restrict_body, 0)
    thr_lo, cge_lo = descend(lambda hh, ll: ll, need_lo)
    thr_lo16 = rows16(thr_lo)

    ib_s[0:1, :] = jnp.full((1, tq), seq, I32)

    @pl.when(jnp.max(cge_lo - need_lo) > 0)
    def _():
        c_gt = count(lambda hh, ll, base: jnp.where(ll > thr_lo16, one16, zero16))
        need = need_lo - c_gt

        def idx_body(j, m):
            cand = m | (jnp.int32(1) << (11 - j))
            cb = rows16(cand)

            def pred(hh, ll, base):
                below = jnp.where((row_iota + base).astype(I16) < cb, one16, zero16)
                return jnp.where(hh == thr_hi16, jnp.where(ll == thr_lo16, below, zero16), zero16)

            return jnp.where(count(pred) < need, cand, m)

        n_bits = max(1, (seq - 1).bit_length())
        m = lax.fori_loop(12 - n_bits, 12, idx_body, jnp.zeros((1, tq), I32))
        ib_s[0:1, :] = m + 1

    bound16 = rows16(ib_s[0:1, :])
    q_pos16 = rows16(q_pos)

    def bias_body(c, diagonal):
        off = pl.multiple_of(c * kc, kc)
        hi_slab = hi_s[pl.ds(off, kc), :]
        lo_slab = lo_s[pl.ds(off, kc), :]
        out = []
        for r in range(n_grp):
            rows = slice(r * PACK16_ROWS, (r + 1) * PACK16_ROWS)
            hh, ll = hi_slab[rows, :], lo_slab[rows, :]
            key_pos = (row_iota + (off + r * PACK16_ROWS)).astype(I16)
            tie = jnp.where(key_pos < bound16, one16, zero16)
            in_bucket = jnp.where(ll > thr_lo16, one16, jnp.where(ll == thr_lo16, tie, zero16))
            sel = jnp.where(hh > thr_hi16, one16, jnp.where(hh == thr_hi16, in_bucket, zero16))
            if diagonal:
                sel = jnp.where(key_pos <= q_pos16, sel, zero16)
            out.append(sel)
        sel_all = jnp.concatenate(out, axis=0).astype(I32)
        bias_s[pl.ds(off, kc), :] = jnp.where(sel_all > 0, 0.0, NEG)

    _for_chunks(i, lambda c: bias_body(c, False))
    bias_body(i, True)

    feat = lax.broadcasted_iota(I32, (WIDTH, tq), 0)
    qt = qt_ref[0]
    for h in range(A_HEADS):
        qm_s[:, h * tq:(h + 1) * tq] = jnp.where(feat // A_HEAD_DIM == h, qt, jnp.zeros_like(qt))
    acc_s[...] = jnp.zeros_like(acc_s)
    m_s[0:1, :] = jnp.full((1, A_HEADS * tq), NEG, F32)

    span = 2 * kc

    def logits_block(off):
        bias = bias_s[pl.ds(off, kc), :]
        return _dot(k_ref[0, pl.ds(off, kc), :], qm_s[...]) + jnp.concatenate([bias] * A_HEADS, axis=1)

    def pv_block(off, pb, h):
        rows = slice(h * VT_HEAD_ROWS, (h + 1) * VT_HEAD_ROWS)
        return _dot(vt_ref[0, 0, rows, pl.ds(off, kc)], pb[:, h * tq:(h + 1) * tq])

    def rescale_and_add(alpha, pv):
        for h in range(A_HEADS):
            rows = slice(h * VT_HEAD_ROWS, (h + 1) * VT_HEAD_ROWS)
            acc_s[rows, :] = acc_s[rows, :] * alpha[:, h * tq:(h + 1) * tq] + pv[h]

    def attend(off, n_blk):
        s = [logits_block(off + r * kc) for r in range(n_blk)]
        m_old = m_s[0:1, :]
        m_new = m_old
        for r in range(n_blk):
            m_new = jnp.maximum(m_new, jnp.max(s[r], axis=0, keepdims=True))
        m_s[0:1, :] = m_new
        pv = [None] * A_HEADS
        for r in range(n_blk):
            pb = jnp.exp2(s[r] - m_new).astype(BF16)
            for h in range(A_HEADS):
                d = pv_block(off + r * kc, pb, h)
                pv[h] = d if pv[h] is None else pv[h] + d
        rescale_and_add(jnp.exp2(m_old - m_new), pv)

    def logits_span(off, buf):
        s_ref = (s0_s, s1_s)[buf]
        lm = None
        for r in range(span // kc):
            s = logits_block(off + r * kc)
            s_ref[r * kc:(r + 1) * kc, :] = s
            lm_r = jnp.max(s, axis=0, keepdims=True)
            lm = lm_r if lm is None else jnp.maximum(lm, lm_r)
        lm_s[buf:buf + 1, :] = lm

    def step(off_next, off_cur, buf_next):
        buf_cur = 1 - buf_next
        s_next, s_cur = (s0_s, s1_s)[buf_next], (s0_s, s1_s)[buf_cur]
        m_old = m_s[0:1, :]
        m_new = jnp.maximum(m_old, lm_s[buf_cur:buf_cur + 1, :])
        m_s[0:1, :] = m_new
        lm, pv = None, [None] * A_HEADS
        for r in range(span // kc):
            blk = slice(r * kc, (r + 1) * kc)
            s = logits_block(off_next + r * kc)
            s_next[blk, :] = s
            lm_r = jnp.max(s, axis=0, keepdims=True)
            lm = lm_r if lm is None else jnp.maximum(lm, lm_r)
            pb = jnp.exp2(s_cur[blk, :] - m_new).astype(BF16)
            for h in range(A_HEADS):
                d = pv_block(off_cur + r * kc, pb, h)
                pv[h] = d if pv[h] is None else pv[h] + d
        lm_s[buf_next:buf_next + 1, :] = lm
        rescale_and_add(jnp.exp2(m_old - m_new), pv)

    n_full = n_chunks // 2
    n_piped = n_full - n_full % 2

    @pl.when(n_chunks % 2 == 1)
    def _():
        attend(pl.multiple_of(i * kc, kc), 1)

    @pl.when(n_full % 2 == 1)
    def _():
        attend(pl.multiple_of((n_full - 1) * span, span), 2)

    def drain(off_cur, buf_cur):
        s_cur = (s0_s, s1_s)[buf_cur]
        m_old = m_s[0:1, :]
        m_new = jnp.maximum(m_old, lm_s[buf_cur:buf_cur + 1, :])
        m_s[0:1, :] = m_new
        pv = [None] * A_HEADS
        for r in range(span // kc):
            pb = jnp.exp2(s_cur[r * kc:(r + 1) * kc, :] - m_new).astype(BF16)
            for h in range(A_HEADS):
                d = pv_block(off_cur + r * kc, pb, h)
                pv[h] = d if pv[h] is None else pv[h] + d
        rescale_and_add(jnp.exp2(m_old - m_new), pv)

    @pl.when(n_piped > 0)
    def _():
        logits_span(0, 0)

        def body(jj, carry):
            off0 = pl.multiple_of(2 * jj * span, span)
            step(off0 + span, off0, 1)
            step(off0 + 2 * span, off0 + span, 0)
            return carry

        lax.fori_loop(0, n_piped // 2 - 1, body, 0)
        off0 = pl.multiple_of((n_piped - 2) * span, span)
        step(off0 + span, off0, 1)
        drain(off0 + span, 1)

    out = []
    for h in range(A_HEADS):
        r0 = h * VT_HEAD_ROWS
        out.append(acc_s[r0:r0 + VT_HEAD_DIM, :] * (1.0 / acc_s[r0 + VT_HEAD_DIM:r0 + VT_HEAD_DIM + 1, :]))
    o_ref[0] = jnp.concatenate(out, axis=0).T.astype(BF16)


def _dsa(qt, vt, pn, ikn, iwt, tq):
    bsz, seq, _ = pn.shape
    top_k = min(TOPK_MAX, seq // 4)
    grid = (bsz, seq // tq)
    return pl.pallas_call(
        functools.partial(_dsa_kernel, tq=tq, top_k=top_k, seq=seq),
        grid=grid,
        in_specs=[
            pl.BlockSpec((1, WIDTH, tq), lambda b, i: (b, 0, i)),
            pl.BlockSpec((1, WIDTH, tq), lambda b, i: (b, 1, i)),
            pl.BlockSpec((1, SUBLANES, tq), lambda b, i: (b, 0, i)),
            pl.BlockSpec((1, seq, IDX_DIM), lambda b, i: (b, 0, 0)),
            pl.BlockSpec((1, seq, WIDTH), lambda b, i: (b, 0, 0)),
            pl.BlockSpec((1, 1, VT_ROWS, seq), lambda b, i: (b, 0, 0, 0)),
        ],
        out_specs=pl.BlockSpec((1, tq, WIDTH), lambda b, i: (b, i, 0)),
        out_shape=jax.ShapeDtypeStruct((bsz, seq, WIDTH), BF16),
        scratch_shapes=[pltpu.VMEM((seq, tq), I16),
                        pltpu.VMEM((seq, tq), I16),
                        pltpu.VMEM((seq, tq), F32),
                        pltpu.VMEM((WIDTH, A_HEADS * tq), BF16),
                        pltpu.VMEM((VT_ROWS, tq), F32),
                        pltpu.VMEM((SUBLANES, A_HEADS * tq), F32),
                        pltpu.VMEM((SUBLANES, tq), I32),
                        pltpu.VMEM((2 * tq, A_HEADS * tq), F32),
                        pltpu.VMEM((2 * tq, A_HEADS * tq), F32),
                        pltpu.VMEM((SUBLANES, A_HEADS * tq), F32)],
        compiler_params=pltpu.CompilerParams(
            dimension_semantics=("parallel", "arbitrary"), vmem_limit_bytes=VMEM_LIMIT_BYTES),
        name="dsa",
    )(qt, qt, iwt, ikn, pn, vt)


def _diff_kernel(qt_ref, k_ref, vt_ref, lam_ref, sg_ref, o_ref, qm_s, acc_s, m_s, s0_s, s1_s, lm_s,
                 *, tq, seq, lambda_init):
    i = pl.program_id(1)
    span = 2 * tq
    n_pairs = 2 * C_HEADS
    grp = n_pairs // 2
    feat = lax.broadcasted_iota(I32, (WIDTH, tq), 0)
    qt = qt_ref[0]
    for p_i in range(n_pairs):
        qm_s[p_i // grp, :, (p_i % grp) * tq:(p_i % grp + 1) * tq] = jnp.where(
            feat // C_HEAD_DIM == p_i, qt, jnp.zeros_like(qt))
    acc_s[...] = jnp.zeros_like(acc_s)
    m_s[0:1, :] = jnp.full((1, n_pairs * tq), NEG, F32)
    diag_ok = lax.broadcasted_iota(I32, (tq, tq), 0) <= lax.broadcasted_iota(I32, (tq, tq), 1)
    diag_ok = jnp.concatenate([diag_ok] * grp, axis=1)

    def logits(off, width, buf, mask_tail):
        s_ref = (s0_s, s1_s)[buf]
        k_c = k_ref[0, pl.ds(off, width), :]
        for g in range(2):
            s = _dot(k_c, qm_s[g])
            if mask_tail:
                head = s[0:width - tq, :]
                tail = jnp.where(diag_ok, s[width - tq:width, :], NEG)
                s = jnp.concatenate([head, tail], axis=0) if width > tq else tail
            s_ref[g, 0:width, :] = s
            lm_s[buf:buf + 1, g * grp * tq:(g + 1) * grp * tq] = jnp.max(s, axis=0, keepdims=True)

    def accumulate(off, width, buf):
        s_ref = (s0_s, s1_s)[buf]
        for g in range(2):
            st = slice(g * grp * tq, (g + 1) * grp * tq)
            m_old = m_s[0:1, st]
            m_new = jnp.maximum(m_old, lm_s[buf:buf + 1, st])
            alpha = jnp.exp2(m_old - m_new)
            pb = jnp.exp2(s_ref[g, 0:width, :] - m_new).astype(BF16)
            m_s[0:1, st] = m_new
            for j in range(grp):
                h, comp = (g * grp + j) // 2, (g * grp + j) % 2
                rows = slice(h * VT_HEAD_ROWS, (h + 1) * VT_HEAD_ROWS)
                cols = slice(j * tq, (j + 1) * tq)
                pv = _dot(vt_ref[0, 0, rows, pl.ds(off, width)], pb[:, cols])
                acc_s[comp, rows, :] = acc_s[comp, rows, :] * alpha[:, cols] + pv

    @pl.when(i % 2 == 0)
    def _():
        off = pl.multiple_of(i * tq, tq)
        logits(off, tq, 0, True)
        accumulate(off, tq, 0)

    @pl.when(i % 2 == 1)
    def _():
        off = pl.multiple_of((i - 1) * tq, span)
        logits(off, span, 0, True)
        accumulate(off, span, 0)

    n_full = i // 2
    n_piped = n_full - n_full % 2

    @pl.when(n_full % 2 == 1)
    def _():
        off = pl.multiple_of((n_full - 1) * span, span)
        logits(off, span, 0, False)
        accumulate(off, span, 0)

    def step(off_next, off_cur, buf_next):
        buf_cur = 1 - buf_next
        s_next, s_cur = (s0_s, s1_s)[buf_next], (s0_s, s1_s)[buf_cur]
        m_new, alpha, lm_run = [], [], []
        for g in range(2):
            st = slice(g * grp * tq, (g + 1) * grp * tq)
            m_old = m_s[0:1, st]
            m_new.append(jnp.maximum(m_old, lm_s[buf_cur:buf_cur + 1, st]))
            alpha.append(jnp.exp2(m_old - m_new[g]))
            m_s[0:1, st] = m_new[g]
            lm_run.append(jnp.full((1, grp * tq), NEG, F32))
        pv = [None] * n_pairs
        for r in range(span // tq):
            blk = slice(r * tq, (r + 1) * tq)
            k_blk = k_ref[0, pl.ds(off_next + r * tq, tq), :]
            for g in range(2):
                s = _dot(k_blk, qm_s[g])
                s_next[g, blk, :] = s
                lm_run[g] = jnp.maximum(lm_run[g], jnp.max(s, axis=0, keepdims=True))
                pb = jnp.exp2(s_cur[g, blk, :] - m_new[g]).astype(BF16)
                for j in range(grp):
                    p_i = g * grp + j
                    rows = slice((p_i // 2) * VT_HEAD_ROWS, (p_i // 2 + 1) * VT_HEAD_ROWS)
                    d = _dot(vt_ref[0, 0, rows, pl.ds(off_cur + r * tq, tq)], pb[:, j * tq:(j + 1) * tq])
                    pv[p_i] = d if pv[p_i] is None else pv[p_i] + d
        for g in range(2):
            lm_s[buf_next:buf_next + 1, g * grp * tq:(g + 1) * grp * tq] = lm_run[g]
            for j in range(grp):
                p_i = g * grp + j
                rows = slice((p_i // 2) * VT_HEAD_ROWS, (p_i // 2 + 1) * VT_HEAD_ROWS)
                acc_s[p_i % 2, rows, :] = acc_s[p_i % 2, rows, :] * alpha[g][:, j * tq:(j + 1) * tq] + pv[p_i]

    @pl.when(n_piped > 0)
    def _():
        logits(0, span, 0, False)

        def body(jj, carry):
            off0 = pl.multiple_of(2 * jj * span, span)
            step(off0 + span, off0, 1)
            step(off0 + 2 * span, off0 + span, 0)
            return carry

        lax.fori_loop(0, n_piped // 2 - 1, body, 0)
        off0 = pl.multiple_of((n_piped - 2) * span, span)
        step(off0 + span, off0, 1)
        accumulate(off0 + span, span, 1)

    lq = lam_ref[...]
    lam = (jnp.exp(jnp.sum(lq[0:1] * lq[1:2], axis=1, keepdims=True))
           - jnp.exp(jnp.sum(lq[2:3] * lq[3:4], axis=1, keepdims=True)) + lambda_init)
    outs = []
    for h in range(C_HEADS):
        r0 = h * VT_HEAD_ROWS
        l_row = slice(r0 + VT_HEAD_DIM, r0 + VT_HEAD_DIM + 1)
        a1 = acc_s[0, r0:r0 + VT_HEAD_DIM, :] * (1.0 / acc_s[0, l_row, :])
        a2 = acc_s[1, r0:r0 + VT_HEAD_DIM, :] * (1.0 / acc_s[1, l_row, :])
        out = a1 - lam * a2
        rs = lax.rsqrt(jnp.mean(out * out, axis=0, keepdims=True) + EPS)
        outs.append(out * rs * sg_ref[...] * (1.0 - lambda_init))
    o_ref[0] = jnp.concatenate(outs, axis=0).T.astype(BF16)


def _diff(qt, vt, pn, lam_p, sg_col, tq, lambda_init):
    bsz, seq, _ = pn.shape
    grid = (bsz, seq // tq)
    n_pairs = 2 * C_HEADS
    return pl.pallas_call(
        functools.partial(_diff_kernel, tq=tq, seq=seq, lambda_init=lambda_init),
        grid=grid,
        in_specs=[
            pl.BlockSpec((1, WIDTH, tq), lambda b, i: (b, 2, i)),
            pl.BlockSpec((1, seq, WIDTH), lambda b, i: (b, 0, 3)),
            pl.BlockSpec((1, 1, VT_ROWS, seq), lambda b, i: (b, 1, 0, 0)),
            pl.BlockSpec(lam_p.shape, lambda b, i: (0, 0)),
            pl.BlockSpec(sg_col.shape, lambda b, i: (0, 0)),
        ],
        out_specs=pl.BlockSpec((1, tq, WIDTH), lambda b, i: (b, i, 0)),
        out_shape=jax.ShapeDtypeStruct((bsz, seq, WIDTH), BF16),
        scratch_shapes=[pltpu.VMEM((2, WIDTH, (n_pairs // 2) * tq), BF16),
                        pltpu.VMEM((2, VT_ROWS, tq), F32),
                        pltpu.VMEM((SUBLANES, n_pairs * tq), F32),
                        pltpu.VMEM((2, 2 * tq, (n_pairs // 2) * tq), F32),
                        pltpu.VMEM((2, 2 * tq, (n_pairs // 2) * tq), F32),
                        pltpu.VMEM((SUBLANES, n_pairs * tq), F32)],
        compiler_params=pltpu.CompilerParams(
            dimension_semantics=("parallel", "arbitrary"), vmem_limit_bytes=VMEM_LIMIT_BYTES),
        name="diff",
    )(qt, pn, vt, lam_p, sg_col)


def _gelu_tanh(x):
    return x * (0.5 * (1.0 + jnp.tanh(math.sqrt(2.0 / math.pi) * (x + 0.044715 * (x * x * x)))))


def _sigmoid(x):
    return 1.0 / (1.0 + jnp.exp(-x))


def _merge_kernel(x_ref, ya_ref, bu_ref, bv_ref, yc_ref, gt_ref, lg_ref, lb_ref, ws_ref, bs_ref,
                  wa_ref, wb_ref, wc_ref, wo_ref, o_ref, yb_s, *, tm):
    d = x_ref.shape[-1]
    lane_w = lax.broadcasted_iota(I32, (CHUNK, B_WIDTH), 1)
    group_of_lane = lane_w // (B_WIDTH // B_GROUPS)
    r_i = lax.broadcasted_iota(I32, (CHUNK, CHUNK), 0)
    c_i = lax.broadcasted_iota(I32, (CHUNK, CHUNK), 1)
    w_tril = [jnp.where(c_i <= r_i, ws_ref[g], 0.0).astype(BF16) for g in range(B_GROUPS)]
    for c in range(tm // CHUNK):
        rows = slice(c * CHUNK, (c + 1) * CHUNK)
        u = _gelu_tanh(bu_ref[0, rows, :].astype(F32))
        v = _gelu_tanh(bv_ref[0, rows, :].astype(F32))
        mu = jnp.mean(v, axis=-1, keepdims=True)
        vc = v - mu
        var = jnp.mean(vc * vc, axis=-1, keepdims=True)
        vn = (vc * lax.rsqrt(var + EPS) * lg_ref[...] + lb_ref[...]).astype(BF16)
        s = bs_ref[...]
        for g in range(B_GROUPS):
            s = s + _dot(w_tril[g], jnp.where(group_of_lane == g, vn, jnp.zeros_like(vn)))
        yb_s[rows, :] = (u * s).astype(BF16)

    merged = jnp.zeros((tm, d), F32)
    branches = ((ya_ref[0], wa_ref), (yb_s[...], wb_ref), (yc_ref[0], wc_ref))
    for n, (y, w_ref) in enumerate(branches):
        gate = _sigmoid(gt_ref[0, :, n * d:(n + 1) * d].astype(F32))
        merged = merged + gate * _dot(y, w_ref[...])
    o_ref[0] = x_ref[0] + _dot(merged.astype(BF16), wo_ref[...])


def _merge(x, pn, ya, yc, gates, lg, lb, ws, bs_full, wa, wb, wc, wo, tm):
    bsz, seq, d = x.shape
    grid = (bsz, seq // tm)
    const2 = lambda b, i: (0, 0)
    return pl.pallas_call(
        functools.partial(_merge_kernel, tm=tm),
        grid=grid,
        in_specs=[
            pl.BlockSpec((1, tm, d), lambda b, i: (b, i, 0)),
            pl.BlockSpec((1, tm, WIDTH), lambda b, i: (b, i, 0)),
            pl.BlockSpec((1, tm, WIDTH), lambda b, i: (b, i, 1)),
            pl.BlockSpec((1, tm, WIDTH), lambda b, i: (b, i, 2)),
            pl.BlockSpec((1, tm, WIDTH), lambda b, i: (b, i, 0)),
            pl.BlockSpec((1, tm, 3 * d), lambda b, i: (b, i, 0)),
            pl.BlockSpec(lg.shape, const2),
            pl.BlockSpec(lb.shape, const2),
            pl.BlockSpec(ws.shape, lambda b, i: (0, 0, 0)),
            pl.BlockSpec(bs_full.shape, const2),
            pl.BlockSpec(wa.shape, const2),
            pl.BlockSpec(wb.shape, const2),
            pl.BlockSpec(wc.shape, const2),
            pl.BlockSpec(wo.shape, const2),
        ],
        out_specs=pl.BlockSpec((1, tm, d), lambda b, i: (b, i, 0)),
        out_shape=jax.ShapeDtypeStruct((bsz, seq, d), F32),
        scratch_shapes=[pltpu.VMEM((tm, B_WIDTH), BF16)],
        compiler_params=pltpu.CompilerParams(
            dimension_semantics=("parallel", "parallel"), vmem_limit_bytes=VMEM_LIMIT_BYTES),
        name="merge",
    )(x, ya, pn, pn, yc, gates, lg, lb, ws, bs_full, wa, wb, wc, wo)


def _ffn_kernel(x_ref, g_ref, w1_ref, w2_ref, fg_ref, o_ref, h_s, acc_s, *, final_norm):
    j = pl.program_id(1)

    @pl.when(j == 0)
    def _():
        x = x_ref[...]
        ms = jnp.mean(x * x, axis=-1, keepdims=True)
        h_s[...] = (x * lax.rsqrt(ms + EPS) * g_ref[...]).astype(BF16)
        acc_s[...] = jnp.zeros_like(acc_s)

    a = jnp.maximum(_dot(h_s[...], w1_ref[...]), 0.0)
    acc_s[...] += _dot((a * a).astype(BF16), w2_ref[...])

    @pl.when(j == pl.num_programs(1) - 1)
    def _():
        y = x_ref[...] + acc_s[...]
        if final_norm:
            ms = jnp.mean(y * y, axis=-1, keepdims=True)
            y = y * lax.rsqrt(ms + EPS) * fg_ref[...]
        o_ref[...] = y


def _ffn(x2d, g, w1, w2, fg, tm, tf, final_norm):
    m, d = x2d.shape
    dff = w1.shape[1]
    grid = (m // tm, dff // tf)
    return pl.pallas_call(
        functools.partial(_ffn_kernel, final_norm=final_norm),
        grid=grid,
        in_specs=[
            pl.BlockSpec((tm, d), lambda i, j: (i, 0)),
            pl.BlockSpec((1, d), lambda i, j: (0, 0)),
            pl.BlockSpec((d, tf), lambda i, j: (0, j)),
            pl.BlockSpec((tf, d), lambda i, j: (j, 0)),
            pl.BlockSpec((1, d), lambda i, j: (0, 0)),
        ],
        out_specs=pl.BlockSpec((tm, d), lambda i, j: (i, 0)),
        out_shape=jax.ShapeDtypeStruct((m, d), F32),
        scratch_shapes=[pltpu.VMEM((tm, d), BF16), pltpu.VMEM((tm, d), F32)],
        compiler_params=pltpu.CompilerParams(
            dimension_semantics=("parallel", "arbitrary"), vmem_limit_bytes=VMEM_LIMIT_BYTES),
        name="ffn",
    )(x2d, g, w1, w2, fg)


def _tile(n, pref):
    t = min(n, pref)
    assert n % t == 0, (n, t)
    return t


def kernel(x, attn_norm_g, w_in, idx_k_norm_g, idx_k_norm_b, sgu_norm_g, sgu_norm_b, sgu_w_s, sgu_b_s,
           diff_lambda, diff_subln_g, w_branch_a, w_branch_b, w_branch_c, w_out, mlp_norm_g, w_ff1,
           w_ff2, final_norm_g):
    bsz, seq, d = x.shape
    depth = w_in.shape[0]
    offs = [0]
    for s in IN_SIZES:
        offs.append(offs[-1] + s)
    (o_aq, o_ak, o_av, o_iq, o_ik, o_iw, o_buv, o_cq, o_ck, o_cv, o_g, o_end) = offs

    tm_proj = _tile(seq, 512)
    tq = _tile(seq, 256)
    tm_merge = _tile(seq, 512)
    tm_ffn = _tile(bsz * seq, 1024)
    tf = _tile(w_ff1.shape[2], 512)

    for l in range(depth):
        lambda_init = 0.8 - 0.6 * math.exp(-0.3 * l)
        w = w_in[l]
        cols = lambda a, b: w[:, a:b]
        wn = jnp.concatenate([cols(o_ak, o_av), cols(o_buv, o_cq), cols(o_ck, o_cv)], axis=1).astype(BF16)
        wt = jnp.concatenate([cols(o_aq, o_ak), cols(o_iq, o_ik), cols(o_cq, o_ck), cols(o_av, o_iq),
                              cols(o_cv, o_g), cols(o_iw, o_buv),
                              jnp.zeros((d, SUBLANES - IDX_HEADS), w.dtype)], axis=1).T.astype(BF16)
        wik = cols(o_ik, o_iw).astype(BF16)
        wg = cols(o_g, o_end).astype(BF16)

        pn, ikn, gates, qt, vt, iwt = _proj(
            x, attn_norm_g[l][None, :], wn, wik, wg, wt,
            idx_k_norm_g[l][None, :], idx_k_norm_b[l][None, :], tm_proj)

        ya = _dsa(qt, vt, pn, ikn, iwt, tq)
        sg_col = diff_subln_g[l][:, None]
        yc = _diff(qt, vt, pn, diff_lambda[l], sg_col, tq, lambda_init)

        bs_full = jnp.repeat(sgu_b_s[l].T, B_WIDTH // B_GROUPS, axis=1)
        x = _merge(x, pn, ya, yc, gates, sgu_norm_g[l][None, :], sgu_norm_b[l][None, :], sgu_w_s[l],
                   bs_full, w_branch_a[l].astype(BF16), w_branch_b[l].astype(BF16),
                   w_branch_c[l].astype(BF16), w_out[l].astype(BF16), tm_merge)

        x = _ffn(x.reshape(bsz * seq, d), mlp_norm_g[l][None, :], w_ff1[l].astype(BF16),
                 w_ff2[l].astype(BF16), final_norm_g[None, :], tm_ffn, tf,
                 final_norm=(l == depth - 1)).reshape(bsz, seq, d)
    return x
```

```python
import functools
import math

import jax
import jax.numpy as jnp
from jax import lax
from jax.experimental import pallas as pl
from jax.experimental.pallas import tpu as pltpu

F32 = jnp.float32
BF16 = jnp.bfloat16
I32 = jnp.int32
I16 = jnp.int16

A_HEADS = 4
A_HEAD_DIM = 64
IDX_HEADS = 4
IDX_DIM = 64
TOPK_MAX = 256
B_GROUPS = 4
B_WIDTH = 256
CHUNK = 128
C_HEADS = 4
C_HEAD_DIM = 32
C_V_DIM = 64
WIDTH = 256
EPS = 1e-6
NEG = -1e30
INT16_MIN = -(2 ** 15)
TIE_BIG = 16384.0

LANES = 128
SUBLANES = 8
PACK16_ROWS = 2 * SUBLANES

LOG2E = math.log2(math.e)
VT_HEAD_DIM = 64
VT_HEAD_ROWS = VT_HEAD_DIM + PACK16_ROWS
VT_ROWS = (WIDTH // VT_HEAD_DIM) * VT_HEAD_ROWS
VMEM_LIMIT_BYTES = 56 * 1024 * 1024

IN_SIZES = (256, 256, 256, 256, 64, 4, 512, 256, 256, 256, 3072)
N_COUNT_ACC = 4


def _nt_dot(a, b):
    return lax.dot_general(a, b, (((1,), (1,)), ((), ())), preferred_element_type=F32)


def _dot(a, b):
    return jnp.dot(a, b, preferred_element_type=F32)


def _for_chunks(n, body):
    def pair(j, carry):
        body(2 * j)
        body(2 * j + 1)
        return carry

    lax.fori_loop(0, n // 2, pair, 0)

    @pl.when(n % 2 == 1)
    def _():
        body(n - 1)


def _for_spans(n, kc, body):
    def pair(j, carry):
        body(pl.multiple_of(j * (2 * kc), 2 * kc), 2 * kc)
        return carry

    lax.fori_loop(0, n // 2, pair, 0)

    @pl.when(n % 2 == 1)
    def _():
        body(pl.multiple_of((n - 1) * kc, kc), kc)


def _proj_kernel(x_ref, g_ref, wn_ref, wik_ref, wg_ref, wt_ref, ikg_ref, ikb_ref,
                 pn_ref, ikn_ref, gt_ref, qt_ref, vt_ref, iwt_ref, *, gate_chunk):
    x = x_ref[0]
    tm = x.shape[0]
    ms = jnp.mean(x * x, axis=-1, keepdims=True)
    h = (x * lax.rsqrt(ms + EPS) * g_ref[...]).astype(BF16)
    for c0 in range(0, wn_ref.shape[1], WIDTH):
        pn_ref[0, :, c0:c0 + WIDTH] = _dot(h, wn_ref[:, c0:c0 + WIDTH]).astype(BF16)
    for c0 in range(0, wg_ref.shape[1], gate_chunk):
        gt_ref[0, :, c0:c0 + gate_chunk] = _dot(h, wg_ref[:, c0:c0 + gate_chunk]).astype(BF16)
    for blk, q_scale in enumerate((A_HEAD_DIM ** -0.5 * LOG2E, None, C_HEAD_DIM ** -0.5 * LOG2E)):
        r = _nt_dot(wt_ref[blk * WIDTH:(blk + 1) * WIDTH, :], h)
        if q_scale is not None:
            r = r * q_scale
        qt_ref[0, blk * WIDTH:(blk + 1) * WIDTH, :] = r.astype(BF16)
    for blk in range(2):
        r0 = (3 + blk) * WIDTH
        r = _nt_dot(wt_ref[r0:r0 + WIDTH, :], h).astype(BF16)
        for hd in range(WIDTH // VT_HEAD_DIM):
            o0 = hd * VT_HEAD_ROWS
            vt_ref[0, blk, o0:o0 + VT_HEAD_DIM, :] = r[hd * VT_HEAD_DIM:(hd + 1) * VT_HEAD_DIM, :]
            vt_ref[0, blk, o0 + VT_HEAD_DIM:o0 + VT_HEAD_ROWS, :] = jnp.ones(
                (VT_HEAD_ROWS - VT_HEAD_DIM, tm), BF16)
    iwt_ref[0] = _nt_dot(wt_ref[5 * WIDTH:5 * WIDTH + SUBLANES, :], h)
    ik = _dot(h, wik_ref[...])
    mu = jnp.mean(ik, axis=-1, keepdims=True)
    xc = ik - mu
    var = jnp.mean(xc * xc, axis=-1, keepdims=True)
    ikn_ref[0] = (xc * lax.rsqrt(var + EPS) * ikg_ref[...] + ikb_ref[...]).astype(BF16)


def _proj(x, g, wn, wik, wg, wt, ikg, ikb, tm):
    bsz, seq, d = x.shape
    n_nat, n_gate = wn.shape[1], wg.shape[1]
    assert wt.shape[0] == 5 * WIDTH + SUBLANES
    grid = (bsz, seq // tm)
    const = lambda b, i: (0, 0)
    return pl.pallas_call(
        functools.partial(_proj_kernel, gate_chunk=min(n_gate, 768)),
        grid=grid,
        in_specs=[
            pl.BlockSpec((1, tm, d), lambda b, i: (b, i, 0)),
            pl.BlockSpec((1, d), const),
            pl.BlockSpec(wn.shape, const),
            pl.BlockSpec(wik.shape, const),
            pl.BlockSpec(wg.shape, const),
            pl.BlockSpec(wt.shape, const),
            pl.BlockSpec(ikg.shape, const),
            pl.BlockSpec(ikb.shape, const),
        ],
        out_specs=[
            pl.BlockSpec((1, tm, n_nat), lambda b, i: (b, i, 0)),
            pl.BlockSpec((1, tm, IDX_DIM), lambda b, i: (b, i, 0)),
            pl.BlockSpec((1, tm, n_gate), lambda b, i: (b, i, 0)),
            pl.BlockSpec((1, 3 * WIDTH, tm), lambda b, i: (b, 0, i)),
            pl.BlockSpec((1, 2, VT_ROWS, tm), lambda b, i: (b, 0, 0, i)),
            pl.BlockSpec((1, SUBLANES, tm), lambda b, i: (b, 0, i)),
        ],
        out_shape=[
            jax.ShapeDtypeStruct((bsz, seq, n_nat), BF16),
            jax.ShapeDtypeStruct((bsz, seq, IDX_DIM), BF16),
            jax.ShapeDtypeStruct((bsz, seq, n_gate), BF16),
            jax.ShapeDtypeStruct((bsz, 3 * WIDTH, seq), BF16),
            jax.ShapeDtypeStruct((bsz, 2, VT_ROWS, seq), BF16),
            jax.ShapeDtypeStruct((bsz, SUBLANES, seq), F32),
        ],
        compiler_params=pltpu.CompilerParams(
            dimension_semantics=("parallel", "parallel"), vmem_limit_bytes=VMEM_LIMIT_BYTES),
        name="proj",
    )(x, g, wn, wik, wg, wt, ikg, ikb)


def _dsa_kernel(qt_ref, iqt_ref, iwt_ref, ikn_ref, k_ref, vt_ref, o_ref,
                hi_s, lo_s, bias_s, qm_s, acc_s, m_s, tri_s, s0_s, s1_s, lm_s, *, tq, top_k):
    i = pl.program_id(1)
    kc = tq
    n_chunks = i + 1
    n_grp = kc // PACK16_ROWS
    q_pos = i * tq + lax.broadcasted_iota(I32, (1, tq), 1)
    one16 = jnp.ones((PACK16_ROWS, tq), I16)
    zero16 = jnp.zeros((PACK16_ROWS, tq), I16)

    def rows16(v):
        return jnp.broadcast_to(v, (PACK16_ROWS, tq)).astype(I16)

    w_all = iwt_ref[0] * (IDX_DIM ** -0.5 * IDX_HEADS ** -0.5)
    w_rows = [w_all[h:h + 1, :] for h in range(IDX_HEADS)]
    for h in range(IDX_HEADS):
        qm_s[0:IDX_DIM, h * tq:(h + 1) * tq] = iqt_ref[0, h * IDX_DIM:(h + 1) * IDX_DIM, :]

    def score_body(c, diagonal):
        off = pl.multiple_of(c * kc, kc)
        ik = ikn_ref[0, pl.ds(off, kc), :]
        d = _dot(ik, qm_s[0:IDX_DIM, :])
        sc = jnp.zeros((kc, tq), F32)
        for h in range(IDX_HEADS):
            sc = sc + jnp.maximum(d[:, h * tq:(h + 1) * tq], 0.0) * w_rows[h]
        if diagonal:
            key_pos = off + lax.broadcasted_iota(I32, (kc, tq), 0)
            sc = jnp.where(key_pos <= q_pos, sc, -jnp.inf)
        bits = pltpu.bitcast(sc, I32)
        key = bits ^ ((bits >> 31) & 0x7FFFFFFF)
        hi_s[pl.ds(off, kc), :] = (key >> 16).astype(I16)
        lo_s[pl.ds(off, kc), :] = ((key & 0xFFFF) - 0x8000).astype(I16)

    _for_chunks(i, lambda c: score_body(c, False))
    score_body(i, True)

    def count_ge(ref, cand):
        cb = rows16(cand)

        def body(c, accs):
            off = pl.multiple_of(c * kc, kc)
            slab = ref[pl.ds(off, kc), :]
            accs = list(accs)
            for r in range(n_grp):
                kk = slab[r * PACK16_ROWS:(r + 1) * PACK16_ROWS, :]
                accs[r % N_COUNT_ACC] = accs[r % N_COUNT_ACC] + jnp.where(kk >= cb, one16, zero16)
            return tuple(accs)

        accs = lax.fori_loop(0, n_chunks, body, (zero16,) * N_COUNT_ACC)
        tot = accs[0]
        for a in accs[1:]:
            tot = tot + a
        return jnp.sum(tot.astype(I32), axis=0, keepdims=True)

    def descend(ref, need):
        c0 = count_ge(ref, jnp.zeros((1, tq), I32))
        ok0 = c0 >= need
        thr = jnp.where(ok0, 0, INT16_MIN).astype(I32)
        c_gt = jnp.where(ok0, 0, c0)

        def bit_body(j, carry):
            thr, c_gt = carry
            cand = thr | (jnp.int32(1) << (14 - j))
            cnt = count_ge(ref, cand)
            ok = cnt >= need
            return jnp.where(ok, cand, thr), jnp.where(ok, c_gt, cnt)

        return lax.fori_loop(0, 15, bit_body, (thr, c_gt))

    thr_hi, c_gt_hi = descend(hi_s, top_k)
    thr_hi16 = rows16(thr_hi)
    need_lo = top_k - c_gt_hi

    def restrict_body(c, carry):
        off = pl.multiple_of(c * kc, kc)
        in_bucket = hi_s[pl.ds(off, kc), :] == jnp.concatenate([thr_hi16] * n_grp, axis=0)
        lo_s[pl.ds(off, kc), :] = jnp.where(in_bucket, lo_s[pl.ds(off, kc), :], jnp.int16(INT16_MIN))
        return carry

    lax.fori_loop(0, n_chunks, restrict_body, 0)
    thr_lo, c_gt_lo = descend(lo_s, need_lo)
    thr_lo16 = rows16(thr_lo)
    need_tie = (need_lo - c_gt_lo).astype(F32)

    r_i = lax.broadcasted_iota(I32, (kc, kc), 0)
    c_i = lax.broadcasted_iota(I32, (kc, kc), 1)
    tri_s[0:kc, 0:kc] = jnp.where(c_i <= r_i, 1.0, 0.0).astype(BF16)
    tri_s[0:kc, kc:2 * kc] = jnp.where(c_i == r_i, TIE_BIG, 0.0).astype(BF16)
    tri_s[kc:kc + PACK16_ROWS, 0:kc] = jnp.ones((PACK16_ROWS, kc), BF16)
    tri_s[kc:kc + PACK16_ROWS, kc:2 * kc] = jnp.zeros((PACK16_ROWS, kc), BF16)
    thr_hi_kc = jnp.concatenate([thr_hi16] * n_grp, axis=0)
    thr_lo_kc = jnp.concatenate([thr_lo16] * n_grp, axis=0)
    one_b, zero_b, neg_b = (jnp.full((kc, tq), v, BF16) for v in (1.0, 0.0, -1.0))

    def bias_body(c, seen, diagonal):
        off = pl.multiple_of(c * kc, kc)
        hh = hi_s[pl.ds(off, kc), :]
        ll = lo_s[pl.ds(off, kc), :]
        in_bucket = hh == thr_hi_kc
        lo_eq = ll == thr_lo_kc
        equal = jnp.where(in_bucket, jnp.where(lo_eq, one_b, zero_b), zero_b)
        side = jnp.where(hh > thr_hi_kc, neg_b,
                         jnp.where(in_bucket, jnp.where(ll > thr_lo_kc, neg_b, jnp.where(lo_eq, zero_b, one_b)),
                                   one_b))
        rank = _dot(tri_s[...], jnp.concatenate([equal, side], axis=0))
        b = jnp.where(rank[0:kc, :] <= need_tie - seen, 0.0, NEG)
        if diagonal:
            key_pos = off + lax.broadcasted_iota(I32, (kc, tq), 0)
            b = jnp.where(key_pos <= q_pos, b, NEG)
        bias_s[pl.ds(off, kc), :] = b
        return seen + rank[kc:kc + 1, :]

    seen = lax.fori_loop(0, i // 2, lambda j, seen: bias_body(2 * j + 1, bias_body(2 * j, seen, False), False),
                         jnp.zeros((1, tq), F32))

    @pl.when(i % 2 == 0)
    def _():
        bias_body(i, seen, True)

    @pl.when(i % 2 == 1)
    def _():
        bias_body(i, bias_body(i - 1, seen, False), True)

    feat = lax.broadcasted_iota(I32, (WIDTH, tq), 0)
    qt = qt_ref[0]
    for h in range(A_HEADS):
        qm_s[:, h * tq:(h + 1) * tq] = jnp.where(feat // A_HEAD_DIM == h, qt, jnp.zeros_like(qt))
    acc_s[...] = jnp.zeros_like(acc_s)
    m_s[0:1, :] = jnp.full((1, A_HEADS * tq), NEG, F32)

    span = 2 * kc

    def logits_block(off):
        bias = bias_s[pl.ds(off, kc), :]
        return _dot(k_ref[0, pl.ds(off, kc), :], qm_s[...]) + jnp.concatenate([bias] * A_HEADS, axis=1)

    def pv_block(off, pb, h):
        rows = slice(h * VT_HEAD_ROWS, (h + 1) * VT_HEAD_ROWS)
        return _dot(vt_ref[0, 0, rows, pl.ds(off, kc)], pb[:, h * tq:(h + 1) * tq])

    def rescale_and_add(alpha, pv):
        for h in range(A_HEADS):
            rows = slice(h * VT_HEAD_ROWS, (h + 1) * VT_HEAD_ROWS)
            acc_s[rows, :] = acc_s[rows, :] * alpha[:, h * tq:(h + 1) * tq] + pv[h]

    def attend(off, n_blk):
        s = [logits_block(off + r * kc) for r in range(n_blk)]
        m_old = m_s[0:1, :]
        m_new = m_old
        for r in range(n_blk):
            m_new = jnp.maximum(m_new, jnp.max(s[r], axis=0, keepdims=True))
        m_s[0:1, :] = m_new
        pv = [None] * A_HEADS
        for r in range(n_blk):
            pb = jnp.exp2(s[r] - m_new).astype(BF16)
            for h in range(A_HEADS):
                d = pv_block(off + r * kc, pb, h)
                pv[h] = d if pv[h] is None else pv[h] + d
        rescale_and_add(jnp.exp2(m_old - m_new), pv)

    def logits_span(off, buf):
        s_ref = (s0_s, s1_s)[buf]
        lm = None
        for r in range(span // kc):
            s = logits_block(off + r * kc)
            s_ref[r * kc:(r + 1) * kc, :] = s
            lm_r = jnp.max(s, axis=0, keepdims=True)
            lm = lm_r if lm is None else jnp.maximum(lm, lm_r)
        lm_s[buf:buf + 1, :] = lm

    def step(off_next, off_cur, buf_next):
        buf_cur = 1 - buf_next
        s_next, s_cur = (s0_s, s1_s)[buf_next], (s0_s, s1_s)[buf_cur]
        m_old = m_s[0:1, :]
        m_new = jnp.maximum(m_old, lm_s[buf_cur:buf_cur + 1, :])
        m_s[0:1, :] = m_new
        lm, pv = None, [None] * A_HEADS
        for r in range(span // kc):
            blk = slice(r * kc, (r + 1) * kc)
            s = logits_block(off_next + r * kc)
            s_next[blk, :] = s
            lm_r = jnp.max(s, axis=0, keepdims=True)
            lm = lm_r if lm is None else jnp.maximum(lm, lm_r)
            pb = jnp.exp2(s_cur[blk, :] - m_new).astype(BF16)
            for h in range(A_HEADS):
                d = pv_block(off_cur + r * kc, pb, h)
                pv[h] = d if pv[h] is None else pv[h] + d
        lm_s[buf_next:buf_next + 1, :] = lm
        rescale_and_add(jnp.exp2(m_old - m_new), pv)

    n_full = n_chunks // 2
    n_piped = n_full - n_full % 2

    @pl.when(n_chunks % 2 == 1)
    def _():
        attend(pl.multiple_of(i * kc, kc), 1)

    @pl.when(n_full % 2 == 1)
    def _():
        attend(pl.multiple_of((n_full - 1) * span, span), 2)

    def drain(off_cur, buf_cur):
        s_cur = (s0_s, s1_s)[buf_cur]
        m_old = m_s[0:1, :]
        m_new = jnp.maximum(m_old, lm_s[buf_cur:buf_cur + 1, :])
        m_s[0:1, :] = m_new
        pv = [None] * A_HEADS
        for r in range(span // kc):
            pb = jnp.exp2(s_cur[r * kc:(r + 1) * kc, :] - m_new).astype(BF16)
            for h in range(A_HEADS):
                d = pv_block(off_cur + r * kc, pb, h)
                pv[h] = d if pv[h] is None else pv[h] + d
        rescale_and_add(jnp.exp2(m_old - m_new), pv)

    @pl.when(n_piped > 0)
    def _():
        logits_span(0, 0)

        def body(jj, carry):
            off0 = pl.multiple_of(2 * jj * span, span)
            step(off0 + span, off0, 1)
            step(off0 + 2 * span, off0 + span, 0)
            return carry

        lax.fori_loop(0, n_piped // 2 - 1, body, 0)
        off0 = pl.multiple_of((n_piped - 2) * span, span)
        step(off0 + span, off0, 1)
        drain(off0 + span, 1)

    out = []
    for h in range(A_HEADS):
        r0 = h * VT_HEAD_ROWS
        out.append(acc_s[r0:r0 + VT_HEAD_DIM, :] * (1.0 / acc_s[r0 + VT_HEAD_DIM:r0 + VT_HEAD_DIM + 1, :]))
    o_ref[0] = jnp.concatenate(out, axis=0).T.astype(BF16)


def _dsa(qt, vt, pn, ikn, iwt, tq):
    bsz, seq, _ = pn.shape
    top_k = min(TOPK_MAX, seq // 4)
    grid = (bsz, seq // tq)
    return pl.pallas_call(
        functools.partial(_dsa_kernel, tq=tq, top_k=top_k),
        grid=grid,
        in_specs=[
            pl.BlockSpec((1, WIDTH, tq), lambda b, i: (b, 0, i)),
            pl.BlockSpec((1, WIDTH, tq), lambda b, i: (b, 1, i)),
            pl.BlockSpec((1, SUBLANES, tq), lambda b, i: (b, 0, i)),
            pl.BlockSpec((1, seq, IDX_DIM), lambda b, i: (b, 0, 0)),
            pl.BlockSpec((1, seq, WIDTH), lambda b, i: (b, 0, 0)),
            pl.BlockSpec((1, 1, VT_ROWS, seq), lambda b, i: (b, 0, 0, 0)),
        ],
        out_specs=pl.BlockSpec((1, tq, WIDTH), lambda b, i: (b, i, 0)),
        out_shape=jax.ShapeDtypeStruct((bsz, seq, WIDTH), BF16),
        scratch_shapes=[pltpu.VMEM((seq, tq), I16),
                        pltpu.VMEM((seq, tq), I16),
                        pltpu.VMEM((seq, tq), F32),
                        pltpu.VMEM((WIDTH, A_HEADS * tq), BF16),
                        pltpu.VMEM((VT_ROWS, tq), F32),
                        pltpu.VMEM((SUBLANES, A_HEADS * tq), F32),
                        pltpu.VMEM((tq + PACK16_ROWS, 2 * tq), BF16),
                        pltpu.VMEM((2 * tq, A_HEADS * tq), F32),
                        pltpu.VMEM((2 * tq, A_HEADS * tq), F32),
                        pltpu.VMEM((SUBLANES, A_HEADS * tq), F32)],
        compiler_params=pltpu.CompilerParams(
            dimension_semantics=("parallel", "arbitrary"), vmem_limit_bytes=VMEM_LIMIT_BYTES),
        name="dsa",
    )(qt, qt, iwt, ikn, pn, vt)


def _diff_kernel(qt_ref, k_ref, vt_ref, lam_ref, sg_ref, o_ref, qm_s, acc_s, m_s, s0_s, s1_s, lm_s,
                 *, tq, seq, lambda_init):
    i = pl.program_id(1)
    span = 2 * tq
    n_pairs = 2 * C_HEADS
    grp = n_pairs // 2
    feat = lax.broadcasted_iota(I32, (WIDTH, tq), 0)
    qt = qt_ref[0]
    for p_i in range(n_pairs):
        qm_s[p_i // grp, :, (p_i % grp) * tq:(p_i % grp + 1) * tq] = jnp.where(
            feat // C_HEAD_DIM == p_i, qt, jnp.zeros_like(qt))
    acc_s[...] = jnp.zeros_like(acc_s)
    m_s[0:1, :] = jnp.full((1, n_pairs * tq), NEG, F32)
    diag_ok = lax.broadcasted_iota(I32, (tq, tq), 0) <= lax.broadcasted_iota(I32, (tq, tq), 1)
    diag_ok = jnp.concatenate([diag_ok] * grp, axis=1)

    def logits(off, width, buf, mask_tail):
        s_ref = (s0_s, s1_s)[buf]
        k_c = k_ref[0, pl.ds(off, width), :]
        for g in range(2):
            s = _dot(k_c, qm_s[g])
            if mask_tail:
                head = s[0:width - tq, :]
                tail = jnp.where(diag_ok, s[width - tq:width, :], NEG)
                s = jnp.concatenate([head, tail], axis=0) if width > tq else tail
            s_ref[g, 0:width, :] = s
            lm_s[buf:buf + 1, g * grp * tq:(g + 1) * grp * tq] = jnp.max(s, axis=0, keepdims=True)

    def accumulate(off, width, buf):
        s_ref = (s0_s, s1_s)[buf]
        for g in range(2):
            st = slice(g * grp * tq, (g + 1) * grp * tq)
            m_old = m_s[0:1, st]
            m_new = jnp.maximum(m_old, lm_s[buf:buf + 1, st])
            alpha = jnp.exp2(m_old - m_new)
            pb = jnp.exp2(s_ref[g, 0:width, :] - m_new).astype(BF16)
            m_s[0:1, st] = m_new
            for j in range(grp):
                h, comp = (g * grp + j) // 2, (g * grp + j) % 2
                rows = slice(h * VT_HEAD_ROWS, (h + 1) * VT_HEAD_ROWS)
                cols = slice(j * tq, (j + 1) * tq)
                pv = _dot(vt_ref[0, 0, rows, pl.ds(off, width)], pb[:, cols])
                acc_s[comp, rows, :] = acc_s[comp, rows, :] * alpha[:, cols] + pv

    @pl.when(i % 2 == 0)
    def _():
        off = pl.multiple_of(i * tq, tq)
        logits(off, tq, 0, True)
        accumulate(off, tq, 0)

    @pl.when(i % 2 == 1)
    def _():
        off = pl.multiple_of((i - 1) * tq, span)
        logits(off, span, 0, True)
        accumulate(off, span, 0)

    n_full = i // 2
    n_piped = n_full - n_full % 2

    @pl.when(n_full % 2 == 1)
    def _():
        off = pl.multiple_of((n_full - 1) * span, span)
        logits(off, span, 0, False)
        accumulate(off, span, 0)

    def step(off_next, off_cur, buf_next):
        buf_cur = 1 - buf_next
        s_next, s_cur = (s0_s, s1_s)[buf_next], (s0_s, s1_s)[buf_cur]
        m_new, alpha, lm_run = [], [], []
        for g in range(2):
            st = slice(g * grp * tq, (g + 1) * grp * tq)
            m_old = m_s[0:1, st]
            m_new.append(jnp.maximum(m_old, lm_s[buf_cur:buf_cur + 1, st]))
            alpha.append(jnp.exp2(m_old - m_new[g]))
            m_s[0:1, st] = m_new[g]
            lm_run.append(jnp.full((1, grp * tq), NEG, F32))
        pv = [None] * n_pairs
        for r in range(span // tq):
            blk = slice(r * tq, (r + 1) * tq)
            k_blk = k_ref[0, pl.ds(off_next + r * tq, tq), :]
            for g in range(2):
                s = _dot(k_blk, qm_s[g])
                s_next[g, blk, :] = s
                lm_run[g] = jnp.maximum(lm_run[g], jnp.max(s, axis=0, keepdims=True))
                pb = jnp.exp2(s_cur[g, blk, :] - m_new[g]).astype(BF16)
                for j in range(grp):
                    p_i = g * grp + j
                    rows = slice((p_i // 2) * VT_HEAD_ROWS, (p_i // 2 + 1) * VT_HEAD_ROWS)
                    d = _dot(vt_ref[0, 0, rows, pl.ds(off_cur + r * tq, tq)], pb[:, j * tq:(j + 1) * tq])
                    pv[p_i] = d if pv[p_i] is None else pv[p_i] + d
        for g in range(2):
            lm_s[buf_next:buf_next + 1, g * grp * tq:(g + 1) * grp * tq] = lm_run[g]
            for j in range(grp):
                p_i = g * grp + j
                rows = slice((p_i // 2) * VT_HEAD_ROWS, (p_i // 2 + 1) * VT_HEAD_ROWS)
                acc_s[p_i % 2, rows, :] = acc_s[p_i % 2, rows, :] * alpha[g][:, j * tq:(j + 1) * tq] + pv[p_i]

    @pl.when(n_piped > 0)
    def _():
        logits(0, span, 0, False)

        def body(jj, carry):
            off0 = pl.multiple_of(2 * jj * span, span)
            step(off0 + span, off0, 1)
            step(off0 + 2 * span, off0 + span, 0)
            return carry

        lax.fori_loop(0, n_piped // 2 - 1, body, 0)
        off0 = pl.multiple_of((n_piped - 2) * span, span)
        step(off0 + span, off0, 1)
        accumulate(off0 + span, span, 1)

    lq = lam_ref[...]
    lam = (jnp.exp(jnp.sum(lq[0:1] * lq[1:2], axis=1, keepdims=True))
           - jnp.exp(jnp.sum(lq[2:3] * lq[3:4], axis=1, keepdims=True)) + lambda_init)
    outs = []
    for h in range(C_HEADS):
        r0 = h * VT_HEAD_ROWS
        l_row = slice(r0 + VT_HEAD_DIM, r0 + VT_HEAD_DIM + 1)
        a1 = acc_s[0, r0:r0 + VT_HEAD_DIM, :] * (1.0 / acc_s[0, l_row, :])
        a2 = acc_s[1, r0:r0 + VT_HEAD_DIM, :] * (1.0 / acc_s[1, l_row, :])
        out = a1 - lam * a2
        rs = lax.rsqrt(jnp.mean(out * out, axis=0, keepdims=True) + EPS)
        outs.append(out * rs * sg_ref[...] * (1.0 - lambda_init))
    o_ref[0] = jnp.concatenate(outs, axis=0).T.astype(BF16)


def _diff(qt, vt, pn, lam_p, sg_col, tq, lambda_init):
    bsz, seq, _ = pn.shape
    grid = (bsz, seq // tq)
    n_pairs = 2 * C_HEADS
    return pl.pallas_call(
        functools.partial(_diff_kernel, tq=tq, seq=seq, lambda_init=lambda_init),
        grid=grid,
        in_specs=[
            pl.BlockSpec((1, WIDTH, tq), lambda b, i: (b, 2, i)),
            pl.BlockSpec((1, seq, WIDTH), lambda b, i: (b, 0, 3)),
            pl.BlockSpec((1, 1, VT_ROWS, seq), lambda b, i: (b, 1, 0, 0)),
            pl.BlockSpec(lam_p.shape, lambda b, i: (0, 0)),
            pl.BlockSpec(sg_col.shape, lambda b, i: (0, 0)),
        ],
        out_specs=pl.BlockSpec((1, tq, WIDTH), lambda b, i: (b, i, 0)),
        out_shape=jax.ShapeDtypeStruct((bsz, seq, WIDTH), BF16),
        scratch_shapes=[pltpu.VMEM((2, WIDTH, (n_pairs // 2) * tq), BF16),
                        pltpu.VMEM((2, VT_ROWS, tq), F32),
                        pltpu.VMEM((SUBLANES, n_pairs * tq), F32),
                        pltpu.VMEM((2, 2 * tq, (n_pairs // 2) * tq), F32),
                        pltpu.VMEM((2, 2 * tq, (n_pairs // 2) * tq), F32),
                        pltpu.VMEM((SUBLANES, n_pairs * tq), F32)],
        compiler_params=pltpu.CompilerParams(
            dimension_semantics=("parallel", "arbitrary"), vmem_limit_bytes=VMEM_LIMIT_BYTES),
        name="diff",
    )(qt, pn, vt, lam_p, sg_col)


def _gelu_tanh(x):
    return x * (0.5 * (1.0 + jnp.tanh(math.sqrt(2.0 / math.pi) * (x + 0.044715 * (x * x * x)))))


def _sigmoid(x):
    return 1.0 / (1.0 + jnp.exp(-x))


def _merge_kernel(x_ref, ya_ref, bu_ref, bv_ref, yc_ref, gt_ref, lg_ref, lb_ref, ws_ref, bs_ref,
                  wa_ref, wb_ref, wc_ref, wo_ref, o_ref, yb_s, *, tm):
    d = x_ref.shape[-1]
    lane_w = lax.broadcasted_iota(I32, (CHUNK, B_WIDTH), 1)
    group_of_lane = lane_w // (B_WIDTH // B_GROUPS)
    r_i = lax.broadcasted_iota(I32, (CHUNK, CHUNK), 0)
    c_i = lax.broadcasted_iota(I32, (CHUNK, CHUNK), 1)
    w_tril = [jnp.where(c_i <= r_i, ws_ref[g], 0.0).astype(BF16) for g in range(B_GROUPS)]
    for c in range(tm // CHUNK):
        rows = slice(c * CHUNK, (c + 1) * CHUNK)
        u = _gelu_tanh(bu_ref[0, rows, :].astype(F32))
        v = _gelu_tanh(bv_ref[0, rows, :].astype(F32))
        mu = jnp.mean(v, axis=-1, keepdims=True)
        vc = v - mu
        var = jnp.mean(vc * vc, axis=-1, keepdims=True)
        vn = (vc * lax.rsqrt(var + EPS) * lg_ref[...] + lb_ref[...]).astype(BF16)
        s = bs_ref[...]
        for g in range(B_GROUPS):
            s = s + _dot(w_tril[g], jnp.where(group_of_lane == g, vn, jnp.zeros_like(vn)))
        yb_s[rows, :] = (u * s).astype(BF16)

    merged = jnp.zeros((tm, d), F32)
    branches = ((ya_ref[0], wa_ref), (yb_s[...], wb_ref), (yc_ref[0], wc_ref))
    for n, (y, w_ref) in enumerate(branches):
        gate = _sigmoid(gt_ref[0, :, n * d:(n + 1) * d].astype(F32))
        merged = merged + gate * _dot(y, w_ref[...])
    o_ref[0] = x_ref[0] + _dot(merged.astype(BF16), wo_ref[...])


def _merge(x, pn, ya, yc, gates, lg, lb, ws, bs_full, wa, wb, wc, wo, tm):
    bsz, seq, d = x.shape
    grid = (bsz, seq // tm)
    const2 = lambda b, i: (0, 0)
    return pl.pallas_call(
        functools.partial(_merge_kernel, tm=tm),
        grid=grid,
        in_specs=[
            pl.BlockSpec((1, tm, d), lambda b, i: (b, i, 0)),
            pl.BlockSpec((1, tm, WIDTH), lambda b, i: (b, i, 0)),
            pl.BlockSpec((1, tm, WIDTH), lambda b, i: (b, i, 1)),
            pl.BlockSpec((1, tm, WIDTH), lambda b, i: (b, i, 2)),
            pl.BlockSpec((1, tm, WIDTH), lambda b, i: (b, i, 0)),
            pl.BlockSpec((1, tm, 3 * d), lambda b, i: (b, i, 0)),
            pl.BlockSpec(lg.shape, const2),
            pl.BlockSpec(lb.shape, const2),
            pl.BlockSpec(ws.shape, lambda b, i: (0, 0, 0)),
            pl.BlockSpec(bs_full.shape, const2),
            pl.BlockSpec(wa.shape, const2),
            pl.BlockSpec(wb.shape, const2),
            pl.BlockSpec(wc.shape, const2),
            pl.BlockSpec(wo.shape, const2),
        ],
        out_specs=pl.BlockSpec((1, tm, d), lambda b, i: (b, i, 0)),
        out_shape=jax.ShapeDtypeStruct((bsz, seq, d), F32),
        scratch_shapes=[pltpu.VMEM((tm, B_WIDTH), BF16)],
        compiler_params=pltpu.CompilerParams(
            dimension_semantics=("parallel", "parallel"), vmem_limit_bytes=VMEM_LIMIT_BYTES),
        name="merge",
    )(x, ya, pn, pn, yc, gates, lg, lb, ws, bs_full, wa, wb, wc, wo)


def _ffn_kernel(x_ref, g_ref, w1_ref, w2_ref, fg_ref, o_ref, h_s, acc_s, *, final_norm):
    j = pl.program_id(1)

    @pl.when(j == 0)
    def _():
        x = x_ref[...]
        ms = jnp.mean(x * x, axis=-1, keepdims=True)
        h_s[...] = (x * lax.rsqrt(ms + EPS) * g_ref[...]).astype(BF16)
        acc_s[...] = jnp.zeros_like(acc_s)

    a = jnp.maximum(_dot(h_s[...], w1_ref[...]), 0.0)
    acc_s[...] += _dot((a * a).astype(BF16), w2_ref[...])

    @pl.when(j == pl.num_programs(1) - 1)
    def _():
        y = x_ref[...] + acc_s[...]
        if final_norm:
            ms = jnp.mean(y * y, axis=-1, keepdims=True)
            y = y * lax.rsqrt(ms + EPS) * fg_ref[...]
        o_ref[...] = y


def _ffn(x2d, g, w1, w2, fg, tm, tf, final_norm):
    m, d = x2d.shape
    dff = w1.shape[1]
    grid = (m // tm, dff // tf)
    return pl.pallas_call(
        functools.partial(_ffn_kernel, final_norm=final_norm),
        grid=grid,
        in_specs=[
            pl.BlockSpec((tm, d), lambda i, j: (i, 0)),
            pl.BlockSpec((1, d), lambda i, j: (0, 0)),
            pl.BlockSpec((d, tf), lambda i, j: (0, j)),
            pl.BlockSpec((tf, d), lambda i, j: (j, 0)),
            pl.BlockSpec((1, d), lambda i, j: (0, 0)),
        ],
        out_specs=pl.BlockSpec((tm, d), lambda i, j: (i, 0)),
        out_shape=jax.ShapeDtypeStruct((m, d), F32),
        scratch_shapes=[pltpu.VMEM((tm, d), BF16), pltpu.VMEM((tm, d), F32)],
        compiler_params=pltpu.CompilerParams(
            dimension_semantics=("parallel", "arbitrary"), vmem_limit_bytes=VMEM_LIMIT_BYTES),
        name="ffn",
    )(x2d, g, w1, w2, fg)


def _tile(n, pref):
    t = min(n, pref)
    assert n % t == 0, (n, t)
    return t


def kernel(x, attn_norm_g, w_in, idx_k_norm_g, idx_k_norm_b, sgu_norm_g, sgu_norm_b, sgu_w_s, sgu_b_s,
           diff_lambda, diff_subln_g, w_branch_a, w_branch_b, w_branch_c, w_out, mlp_norm_g, w_ff1,
           w_ff2, final_norm_g):
    bsz, seq, d = x.shape
    depth = w_in.shape[0]
    offs = [0]
    for s in IN_SIZES:
        offs.append(offs[-1] + s)
    (o_aq, o_ak, o_av, o_iq, o_ik, o_iw, o_buv, o_cq, o_ck, o_cv, o_g, o_end) = offs

    tm_proj = _tile(seq, 512)
    tq = _tile(seq, 256)
    tm_merge = _tile(seq, 512)
    tm_ffn = _tile(bsz * seq, 1024)
    tf = _tile(w_ff1.shape[2], 512)

    for l in range(depth):
        lambda_init = 0.8 - 0.6 * math.exp(-0.3 * l)
        w = w_in[l]
        cols = lambda a, b: w[:, a:b]
        wn = jnp.concatenate([cols(o_ak, o_av), cols(o_buv, o_cq), cols(o_ck, o_cv)], axis=1).astype(BF16)
        wt = jnp.concatenate([cols(o_aq, o_ak), cols(o_iq, o_ik), cols(o_cq, o_ck), cols(o_av, o_iq),
                              cols(o_cv, o_g), cols(o_iw, o_buv),
                              jnp.zeros((d, SUBLANES - IDX_HEADS), w.dtype)], axis=1).T.astype(BF16)
        wik = cols(o_ik, o_iw).astype(BF16)
        wg = cols(o_g, o_end).astype(BF16)

        pn, ikn, gates, qt, vt, iwt = _proj(
            x, attn_norm_g[l][None, :], wn, wik, wg, wt,
            idx_k_norm_g[l][None, :], idx_k_norm_b[l][None, :], tm_proj)

        ya = _dsa(qt, vt, pn, ikn, iwt, tq)
        sg_col = diff_subln_g[l][:, None]
        yc = _diff(qt, vt, pn, diff_lambda[l], sg_col, tq, lambda_init)

        bs_full = jnp.repeat(sgu_b_s[l].T, B_WIDTH // B_GROUPS, axis=1)
        x = _merge(x, pn, ya, yc, gates, sgu_norm_g[l][None, :], sgu_norm_b[l][None, :], sgu_w_s[l],
                   bs_full, w_branch_a[l].astype(BF16), w_branch_b[l].astype(BF16),
                   w_branch_c[l].astype(BF16), w_out[l].astype(BF16), tm_merge)

        x = _ffn(x.reshape(bsz * seq, d), mlp_norm_g[l][None, :], w_ff1[l].astype(BF16),
                 w_ff2[l].astype(BF16), final_norm_g[None, :], tm_ffn, tf,
                 final_norm=(l == depth - 1)).reshape(bsz, seq, d)
    return x
```

```python
import functools
import math

import jax
import jax.numpy as jnp
from jax import lax
from jax.experimental import pallas as pl
from jax.experimental.pallas import tpu as pltpu

F32 = jnp.float32
BF16 = jnp.bfloat16
I32 = jnp.int32
I16 = jnp.int16

A_HEADS = 4
A_HEAD_DIM = 64
IDX_HEADS = 4
IDX_DIM = 64
TOPK_MAX = 256
B_GROUPS = 4
B_WIDTH = 256
CHUNK = 128
C_HEADS = 4
C_HEAD_DIM = 32
C_V_DIM = 64
WIDTH = 256
EPS = 1e-6
NEG = -1e30
INT16_MIN = -(2 ** 15)
TIE_BIG = 16384.0

LANES = 128
SUBLANES = 8
PACK16_ROWS = 2 * SUBLANES

LOG2E = math.log2(math.e)
VT_HEAD_DIM = 64
VT_HEAD_ROWS = VT_HEAD_DIM + PACK16_ROWS
VT_ROWS = (WIDTH // VT_HEAD_DIM) * VT_HEAD_ROWS
VMEM_LIMIT_BYTES = 56 * 1024 * 1024

IN_SIZES = (256, 256, 256, 256, 64, 4, 512, 256, 256, 256, 3072)
N_COUNT_ACC = 4


def _nt_dot(a, b):
    return lax.dot_general(a, b, (((1,), (1,)), ((), ())), preferred_element_type=F32)


def _dot(a, b):
    return jnp.dot(a, b, preferred_element_type=F32)


def _for_chunks(n, body):
    def pair(j, carry):
        body(2 * j)
        body(2 * j + 1)
        return carry

    lax.fori_loop(0, n // 2, pair, 0)

    @pl.when(n % 2 == 1)
    def _():
        body(n - 1)


def _for_spans(n, kc, body):
    def pair(j, carry):
        body(pl.multiple_of(j * (2 * kc), 2 * kc), 2 * kc)
        return carry

    lax.fori_loop(0, n // 2, pair, 0)

    @pl.when(n % 2 == 1)
    def _():
        body(pl.multiple_of((n - 1) * kc, kc), kc)


def _proj_kernel(x_ref, g_ref, wn_ref, wik_ref, wg_ref, wt_ref, ikg_ref, ikb_ref,
                 pn_ref, ikn_ref, gt_ref, qt_ref, vt_ref, iwt_ref, *, gate_chunk):
    x = x_ref[0]
    tm = x.shape[0]
    ms = jnp.mean(x * x, axis=-1, keepdims=True)
    h = (x * lax.rsqrt(ms + EPS) * g_ref[...]).astype(BF16)
    for c0 in range(0, wn_ref.shape[1], WIDTH):
        pn_ref[0, :, c0:c0 + WIDTH] = _dot(h, wn_ref[:, c0:c0 + WIDTH]).astype(BF16)
    for c0 in range(0, wg_ref.shape[1], gate_chunk):
        gt_ref[0, :, c0:c0 + gate_chunk] = (_dot(h, wg_ref[:, c0:c0 + gate_chunk]) * 0.5).astype(BF16)
    for blk, q_scale in enumerate((A_HEAD_DIM ** -0.5 * LOG2E, None, C_HEAD_DIM ** -0.5 * LOG2E)):
        r = _nt_dot(wt_ref[blk * WIDTH:(blk + 1) * WIDTH, :], h)
        if q_scale is not None:
            r = r * q_scale
        qt_ref[0, blk * WIDTH:(blk + 1) * WIDTH, :] = r.astype(BF16)
    for blk in range(2):
        r0 = (3 + blk) * WIDTH
        r = _nt_dot(wt_ref[r0:r0 + WIDTH, :], h).astype(BF16)
        for hd in range(WIDTH // VT_HEAD_DIM):
            o0 = hd * VT_HEAD_ROWS
            vt_ref[0, blk, o0:o0 + VT_HEAD_DIM, :] = r[hd * VT_HEAD_DIM:(hd + 1) * VT_HEAD_DIM, :]
            vt_ref[0, blk, o0 + VT_HEAD_DIM:o0 + VT_HEAD_ROWS, :] = jnp.ones(
                (VT_HEAD_ROWS - VT_HEAD_DIM, tm), BF16)
    iwt_ref[0] = _nt_dot(wt_ref[5 * WIDTH:5 * WIDTH + SUBLANES, :], h)
    ik = _dot(h, wik_ref[...])
    mu = jnp.mean(ik, axis=-1, keepdims=True)
    xc = ik - mu
    var = jnp.mean(xc * xc, axis=-1, keepdims=True)
    ikn_ref[0] = (xc * lax.rsqrt(var + EPS) * ikg_ref[...] + ikb_ref[...]).astype(BF16)


def _proj(x, g, wn, wik, wg, wt, ikg, ikb, tm):
    bsz, seq, d = x.shape
    n_nat, n_gate = wn.shape[1], wg.shape[1]
    assert wt.shape[0] == 5 * WIDTH + SUBLANES
    grid = (bsz, seq // tm)
    const = lambda b, i: (0, 0)
    return pl.pallas_call(
        functools.partial(_proj_kernel, gate_chunk=min(n_gate, 768)),
        grid=grid,
        in_specs=[
            pl.BlockSpec((1, tm, d), lambda b, i: (b, i, 0)),
            pl.BlockSpec((1, d), const),
            pl.BlockSpec(wn.shape, const),
            pl.BlockSpec(wik.shape, const),
            pl.BlockSpec(wg.shape, const),
            pl.BlockSpec(wt.shape, const),
            pl.BlockSpec(ikg.shape, const),
            pl.BlockSpec(ikb.shape, const),
        ],
        out_specs=[
            pl.BlockSpec((1, tm, n_nat), lambda b, i: (b, i, 0)),
            pl.BlockSpec((1, tm, IDX_DIM), lambda b, i: (b, i, 0)),
            pl.BlockSpec((1, tm, n_gate), lambda b, i: (b, i, 0)),
            pl.BlockSpec((1, 3 * WIDTH, tm), lambda b, i: (b, 0, i)),
            pl.BlockSpec((1, 2, VT_ROWS, tm), lambda b, i: (b, 0, 0, i)),
            pl.BlockSpec((1, SUBLANES, tm), lambda b, i: (b, 0, i)),
        ],
        out_shape=[
            jax.ShapeDtypeStruct((bsz, seq, n_nat), BF16),
            jax.ShapeDtypeStruct((bsz, seq, IDX_DIM), BF16),
            jax.ShapeDtypeStruct((bsz, seq, n_gate), BF16),
            jax.ShapeDtypeStruct((bsz, 3 * WIDTH, seq), BF16),
            jax.ShapeDtypeStruct((bsz, 2, VT_ROWS, seq), BF16),
            jax.ShapeDtypeStruct((bsz, SUBLANES, seq), F32),
        ],
        compiler_params=pltpu.CompilerParams(
            dimension_semantics=("parallel", "parallel"), vmem_limit_bytes=VMEM_LIMIT_BYTES),
        name="proj",
    )(x, g, wn, wik, wg, wt, ikg, ikb)


def _dsa_kernel(qt_ref, iqt_ref, iwt_ref, ikn_ref, k_ref, vt_ref, o_ref,
                hi_s, lo_s, bias_s, qm_s, acc_s, m_s, tri_s, s0_s, s1_s, lm_s, *, tq, top_k):
    i = pl.program_id(1)
    kc = tq
    n_chunks = i + 1
    n_grp = kc // PACK16_ROWS
    q_pos = i * tq + lax.broadcasted_iota(I32, (1, tq), 1)
    one16 = jnp.ones((PACK16_ROWS, tq), I16)
    zero16 = jnp.zeros((PACK16_ROWS, tq), I16)

    def rows16(v):
        return jnp.broadcast_to(v, (PACK16_ROWS, tq)).astype(I16)

    w_all = iwt_ref[0] * (IDX_DIM ** -0.5 * IDX_HEADS ** -0.5)
    w_rows = [w_all[h:h + 1, :] for h in range(IDX_HEADS)]
    for h in range(IDX_HEADS):
        qm_s[0:IDX_DIM, h * tq:(h + 1) * tq] = iqt_ref[0, h * IDX_DIM:(h + 1) * IDX_DIM, :]

    def score_body(c, diagonal):
        off = pl.multiple_of(c * kc, kc)
        ik = ikn_ref[0, pl.ds(off, kc), :]
        d = _dot(ik, qm_s[0:IDX_DIM, :])
        sc = jnp.zeros((kc, tq), F32)
        for h in range(IDX_HEADS):
            sc = sc + jnp.maximum(d[:, h * tq:(h + 1) * tq], 0.0) * w_rows[h]
        if diagonal:
            key_pos = off + lax.broadcasted_iota(I32, (kc, tq), 0)
            sc = jnp.where(key_pos <= q_pos, sc, -jnp.inf)
        bits = pltpu.bitcast(sc, I32)
        key = bits ^ ((bits >> 31) & 0x7FFFFFFF)
        hi_s[pl.ds(off, kc), :] = (key >> 16).astype(I16)
        lo_s[pl.ds(off, kc), :] = ((key & 0xFFFF) - 0x8000).astype(I16)

    _for_chunks(i, lambda c: score_body(c, False))
    score_body(i, True)

    def count_ge(ref, cand):
        cb = rows16(cand)

        def body(c, accs):
            off = pl.multiple_of(c * kc, kc)
            slab = ref[pl.ds(off, kc), :]
            accs = list(accs)
            for r in range(n_grp):
                kk = slab[r * PACK16_ROWS:(r + 1) * PACK16_ROWS, :]
                accs[r % N_COUNT_ACC] = accs[r % N_COUNT_ACC] + jnp.where(kk >= cb, one16, zero16)
            return tuple(accs)

        accs = lax.fori_loop(0, n_chunks, body, (zero16,) * N_COUNT_ACC)
        tot = accs[0]
        for a in accs[1:]:
            tot = tot + a
        return jnp.sum(tot.astype(I32), axis=0, keepdims=True)

    def descend(ref, need):
        c0 = count_ge(ref, jnp.zeros((1, tq), I32))
        ok0 = c0 >= need
        thr = jnp.where(ok0, 0, INT16_MIN).astype(I32)
        c_gt = jnp.where(ok0, 0, c0)

        def bit_body(j, carry):
            thr, c_gt = carry
            cand = thr | (jnp.int32(1) << (14 - j))
            cnt = count_ge(ref, cand)
            ok = cnt >= need
            return jnp.where(ok, cand, thr), jnp.where(ok, c_gt, cnt)

        return lax.fori_loop(0, 15, bit_body, (thr, c_gt))

    thr_hi, c_gt_hi = descend(hi_s, top_k)
    thr_hi16 = rows16(thr_hi)
    need_lo = top_k - c_gt_hi

    def restrict_body(c, carry):
        off = pl.multiple_of(c * kc, kc)
        in_bucket = hi_s[pl.ds(off, kc), :] == jnp.concatenate([thr_hi16] * n_grp, axis=0)
        lo_s[pl.ds(off, kc), :] = jnp.where(in_bucket, lo_s[pl.ds(off, kc), :], jnp.int16(INT16_MIN))
        return carry

    lax.fori_loop(0, n_chunks, restrict_body, 0)
    thr_lo, c_gt_lo = descend(lo_s, need_lo)
    thr_lo16 = rows16(thr_lo)
    need_tie = (need_lo - c_gt_lo).astype(F32)

    r_i = lax.broadcasted_iota(I32, (kc, kc), 0)
    c_i = lax.broadcasted_iota(I32, (kc, kc), 1)
    tri_s[0:kc, 0:kc] = jnp.where(c_i <= r_i, 1.0, 0.0).astype(BF16)
    tri_s[0:kc, kc:2 * kc] = jnp.where(c_i == r_i, TIE_BIG, 0.0).astype(BF16)
    tri_s[kc:kc + PACK16_ROWS, 0:kc] = jnp.ones((PACK16_ROWS, kc), BF16)
    tri_s[kc:kc + PACK16_ROWS, kc:2 * kc] = jnp.zeros((PACK16_ROWS, kc), BF16)
    thr_hi_kc = jnp.concatenate([thr_hi16] * n_grp, axis=0)
    thr_lo_kc = jnp.concatenate([thr_lo16] * n_grp, axis=0)
    one_b, zero_b, neg_b = (jnp.full((kc, tq), v, BF16) for v in (1.0, 0.0, -1.0))

    def bias_body(c, seen, diagonal):
        off = pl.multiple_of(c * kc, kc)
        hh = hi_s[pl.ds(off, kc), :]
        ll = lo_s[pl.ds(off, kc), :]
        in_bucket = hh == thr_hi_kc
        lo_eq = ll == thr_lo_kc
        equal = jnp.where(in_bucket, jnp.where(lo_eq, one_b, zero_b), zero_b)
        side = jnp.where(hh > thr_hi_kc, neg_b,
                         jnp.where(in_bucket, jnp.where(ll > thr_lo_kc, neg_b, jnp.where(lo_eq, zero_b, one_b)),
                                   one_b))
        rank = _dot(tri_s[...], jnp.concatenate([equal, side], axis=0))
        b = jnp.where(rank[0:kc, :] <= need_tie - seen, 0.0, NEG)
        if diagonal:
            key_pos = off + lax.broadcasted_iota(I32, (kc, tq), 0)
            b = jnp.where(key_pos <= q_pos, b, NEG)
        bias_s[pl.ds(off, kc), :] = b
        return seen + rank[kc:kc + 1, :]

    seen = lax.fori_loop(0, i // 2, lambda j, seen: bias_body(2 * j + 1, bias_body(2 * j, seen, False), False),
                         jnp.zeros((1, tq), F32))

    @pl.when(i % 2 == 0)
    def _():
        bias_body(i, seen, True)

    @pl.when(i % 2 == 1)
    def _():
        bias_body(i, bias_body(i - 1, seen, False), True)

    feat = lax.broadcasted_iota(I32, (WIDTH, tq), 0)
    qt = qt_ref[0]
    for h in range(A_HEADS):
        qm_s[:, h * tq:(h + 1) * tq] = jnp.where(feat // A_HEAD_DIM == h, qt, jnp.zeros_like(qt))
    acc_s[...] = jnp.zeros_like(acc_s)
    m_s[0:1, :] = jnp.full((1, A_HEADS * tq), NEG, F32)

    span = 2 * kc

    def logits_block(off):
        bias = bias_s[pl.ds(off, kc), :]
        return _dot(k_ref[0, pl.ds(off, kc), :], qm_s[...]) + jnp.concatenate([bias] * A_HEADS, axis=1)

    def pv_block(off, pb, h):
        rows = slice(h * VT_HEAD_ROWS, (h + 1) * VT_HEAD_ROWS)
        return _dot(vt_ref[0, 0, rows, pl.ds(off, kc)], pb[:, h * tq:(h + 1) * tq])

    def rescale_and_add(alpha, pv):
        for h in range(A_HEADS):
            rows = slice(h * VT_HEAD_ROWS, (h + 1) * VT_HEAD_ROWS)
            acc_s[rows, :] = acc_s[rows, :] * alpha[:, h * tq:(h + 1) * tq] + pv[h]

    def attend(off, n_blk):
        s = [logits_block(off + r * kc) for r in range(n_blk)]
        m_old = m_s[0:1, :]
        m_new = m_old
        for r in range(n_blk):
            m_new = jnp.maximum(m_new, jnp.max(s[r], axis=0, keepdims=True))
        m_s[0:1, :] = m_new
        pv = [None] * A_HEADS
        for r in range(n_blk):
            pb = jnp.exp2(s[r] - m_new).astype(BF16)
            for h in range(A_HEADS):
                d = pv_block(off + r * kc, pb, h)
                pv[h] = d if pv[h] is None else pv[h] + d
        rescale_and_add(jnp.exp2(m_old - m_new), pv)

    def logits_span(off, buf):
        s_ref = (s0_s, s1_s)[buf]
        lm = None
        for r in range(span // kc):
            s = logits_block(off + r * kc)
            s_ref[r * kc:(r + 1) * kc, :] = s
            lm_r = jnp.max(s, axis=0, keepdims=True)
            lm = lm_r if lm is None else jnp.maximum(lm, lm_r)
        lm_s[buf:buf + 1, :] = lm

    def step(off_next, off_cur, buf_next):
        buf_cur = 1 - buf_next
        s_next, s_cur = (s0_s, s1_s)[buf_next], (s0_s, s1_s)[buf_cur]
        m_old = m_s[0:1, :]
        m_new = jnp.maximum(m_old, lm_s[buf_cur:buf_cur + 1, :])
        m_s[0:1, :] = m_new
        lm, pv = None, [None] * A_HEADS
        for r in range(span // kc):
            blk = slice(r * kc, (r + 1) * kc)
            s = logits_block(off_next + r * kc)
            s_next[blk, :] = s
            lm_r = jnp.max(s, axis=0, keepdims=True)
            lm = lm_r if lm is None else jnp.maximum(lm, lm_r)
            pb = jnp.exp2(s_cur[blk, :] - m_new).astype(BF16)
            for h in range(A_HEADS):
                d = pv_block(off_cur + r * kc, pb, h)
                pv[h] = d if pv[h] is None else pv[h] + d
        lm_s[buf_next:buf_next + 1, :] = lm
        rescale_and_add(jnp.exp2(m_old - m_new), pv)

    n_full = n_chunks // 2
    n_piped = n_full - n_full % 2

    @pl.when(n_chunks % 2 == 1)
    def _():
        attend(pl.multiple_of(i * kc, kc), 1)

    @pl.when(n_full % 2 == 1)
    def _():
        attend(pl.multiple_of((n_full - 1) * span, span), 2)

    def drain(off_cur, buf_cur):
        s_cur = (s0_s, s1_s)[buf_cur]
        m_old = m_s[0:1, :]
        m_new = jnp.maximum(m_old, lm_s[buf_cur:buf_cur + 1, :])
        m_s[0:1, :] = m_new
        pv = [None] * A_HEADS
        for r in range(span // kc):
            pb = jnp.exp2(s_cur[r * kc:(r + 1) * kc, :] - m_new).astype(BF16)
            for h in range(A_HEADS):
                d = pv_block(off_cur + r * kc, pb, h)
                pv[h] = d if pv[h] is None else pv[h] + d
        rescale_and_add(jnp.exp2(m_old - m_new), pv)

    @pl.when(n_piped > 0)
    def _():
        logits_span(0, 0)

        def body(jj, carry):
            off0 = pl.multiple_of(2 * jj * span, span)
            step(off0 + span, off0, 1)
            step(off0 + 2 * span, off0 + span, 0)
            return carry

        lax.fori_loop(0, n_piped // 2 - 1, body, 0)
        off0 = pl.multiple_of((n_piped - 2) * span, span)
        step(off0 + span, off0, 1)
        drain(off0 + span, 1)

    out = []
    for h in range(A_HEADS):
        r0 = h * VT_HEAD_ROWS
        out.append(acc_s[r0:r0 + VT_HEAD_DIM, :] * (1.0 / acc_s[r0 + VT_HEAD_DIM:r0 + VT_HEAD_DIM + 1, :]))
    o_ref[0] = jnp.concatenate(out, axis=0).T.astype(BF16)


def _dsa(qt, vt, pn, ikn, iwt, tq):
    bsz, seq, _ = pn.shape
    top_k = min(TOPK_MAX, seq // 4)
    grid = (bsz, seq // tq)
    return pl.pallas_call(
        functools.partial(_dsa_kernel, tq=tq, top_k=top_k),
        grid=grid,
        in_specs=[
            pl.BlockSpec((1, WIDTH, tq), lambda b, i: (b, 0, i)),
            pl.BlockSpec((1, WIDTH, tq), lambda b, i: (b, 1, i)),
            pl.BlockSpec((1, SUBLANES, tq), lambda b, i: (b, 0, i)),
            pl.BlockSpec((1, seq, IDX_DIM), lambda b, i: (b, 0, 0)),
            pl.BlockSpec((1, seq, WIDTH), lambda b, i: (b, 0, 0)),
            pl.BlockSpec((1, 1, VT_ROWS, seq), lambda b, i: (b, 0, 0, 0)),
        ],
        out_specs=pl.BlockSpec((1, tq, WIDTH), lambda b, i: (b, i, 0)),
        out_shape=jax.ShapeDtypeStruct((bsz, seq, WIDTH), BF16),
        scratch_shapes=[pltpu.VMEM((seq, tq), I16),
                        pltpu.VMEM((seq, tq), I16),
                        pltpu.VMEM((seq, tq), F32),
                        pltpu.VMEM((WIDTH, A_HEADS * tq), BF16),
                        pltpu.VMEM((VT_ROWS, tq), F32),
                        pltpu.VMEM((SUBLANES, A_HEADS * tq), F32),
                        pltpu.VMEM((tq + PACK16_ROWS, 2 * tq), BF16),
                        pltpu.VMEM((2 * tq, A_HEADS * tq), F32),
                        pltpu.VMEM((2 * tq, A_HEADS * tq), F32),
                        pltpu.VMEM((SUBLANES, A_HEADS * tq), F32)],
        compiler_params=pltpu.CompilerParams(
            dimension_semantics=("parallel", "arbitrary"), vmem_limit_bytes=VMEM_LIMIT_BYTES),
        name="dsa",
    )(qt, qt, iwt, ikn, pn, vt)


def _diff_kernel(qt_ref, k_ref, vt_ref, lam_ref, sg_ref, o_ref, qm_s, acc_s, m_s, s0_s, s1_s, lm_s,
                 *, tq, seq, lambda_init):
    i = pl.program_id(1)
    span = 2 * tq
    n_pairs = 2 * C_HEADS
    grp = n_pairs // 2
    feat = lax.broadcasted_iota(I32, (WIDTH, tq), 0)
    qt = qt_ref[0]
    for p_i in range(n_pairs):
        qm_s[p_i // grp, :, (p_i % grp) * tq:(p_i % grp + 1) * tq] = jnp.where(
            feat // C_HEAD_DIM == p_i, qt, jnp.zeros_like(qt))
    acc_s[...] = jnp.zeros_like(acc_s)
    m_s[0:1, :] = jnp.full((1, n_pairs * tq), NEG, F32)
    diag_ok = lax.broadcasted_iota(I32, (tq, tq), 0) <= lax.broadcasted_iota(I32, (tq, tq), 1)
    diag_ok = jnp.concatenate([diag_ok] * grp, axis=1)

    def logits(off, width, buf, mask_tail):
        s_ref = (s0_s, s1_s)[buf]
        k_c = k_ref[0, pl.ds(off, width), :]
        for g in range(2):
            s = _dot(k_c, qm_s[g])
            if mask_tail:
                head = s[0:width - tq, :]
                tail = jnp.where(diag_ok, s[width - tq:width, :], NEG)
                s = jnp.concatenate([head, tail], axis=0) if width > tq else tail
            s_ref[g, 0:width, :] = s
            lm_s[buf:buf + 1, g * grp * tq:(g + 1) * grp * tq] = jnp.max(s, axis=0, keepdims=True)

    def accumulate(off, width, buf):
        s_ref = (s0_s, s1_s)[buf]
        for g in range(2):
            st = slice(g * grp * tq, (g + 1) * grp * tq)
            m_old = m_s[0:1, st]
            m_new = jnp.maximum(m_old, lm_s[buf:buf + 1, st])
            alpha = jnp.exp2(m_old - m_new)
            pb = jnp.exp2(s_ref[g, 0:width, :] - m_new).astype(BF16)
            m_s[0:1, st] = m_new
            for j in range(grp):
                h, comp = (g * grp + j) // 2, (g * grp + j) % 2
                rows = slice(h * VT_HEAD_ROWS, (h + 1) * VT_HEAD_ROWS)
                cols = slice(j * tq, (j + 1) * tq)
                pv = _dot(vt_ref[0, 0, rows, pl.ds(off, width)], pb[:, cols])
                acc_s[comp, rows, :] = acc_s[comp, rows, :] * alpha[:, cols] + pv

    @pl.when(i % 2 == 0)
    def _():
        off = pl.multiple_of(i * tq, tq)
        logits(off, tq, 0, True)
        accumulate(off, tq, 0)

    @pl.when(i % 2 == 1)
    def _():
        off = pl.multiple_of((i - 1) * tq, span)
        logits(off, span, 0, True)
        accumulate(off, span, 0)

    n_full = i // 2
    n_piped = n_full - n_full % 2

    @pl.when(n_full % 2 == 1)
    def _():
        off = pl.multiple_of((n_full - 1) * span, span)
        logits(off, span, 0, False)
        accumulate(off, span, 0)

    def step(off_next, off_cur, buf_next):
        buf_cur = 1 - buf_next
        s_next, s_cur = (s0_s, s1_s)[buf_next], (s0_s, s1_s)[buf_cur]
        m_new, alpha, lm_run = [], [], []
        for g in range(2):
            st = slice(g * grp * tq, (g + 1) * grp * tq)
            m_old = m_s[0:1, st]
            m_new.append(jnp.maximum(m_old, lm_s[buf_cur:buf_cur + 1, st]))
            alpha.append(jnp.exp2(m_old - m_new[g]))
            m_s[0:1, st] = m_new[g]
            lm_run.append(jnp.full((1, grp * tq), NEG, F32))
        pv = [None] * n_pairs
        for r in range(span // tq):
            blk = slice(r * tq, (r + 1) * tq)
            k_blk = k_ref[0, pl.ds(off_next + r * tq, tq), :]
            for g in range(2):
                s = _dot(k_blk, qm_s[g])
                s_next[g, blk, :] = s
                lm_run[g] = jnp.maximum(lm_run[g], jnp.max(s, axis=0, keepdims=True))
                pb = jnp.exp2(s_cur[g, blk, :] - m_new[g]).astype(BF16)
                for j in range(grp):
                    p_i = g * grp + j
                    rows = slice((p_i // 2) * VT_HEAD_ROWS, (p_i // 2 + 1) * VT_HEAD_ROWS)
                    d = _dot(vt_ref[0, 0, rows, pl.ds(off_cur + r * tq, tq)], pb[:, j * tq:(j + 1) * tq])
                    pv[p_i] = d if pv[p_i] is None else pv[p_i] + d
        for g in range(2):
            lm_s[buf_next:buf_next + 1, g * grp * tq:(g + 1) * grp * tq] = lm_run[g]
            for j in range(grp):
                p_i = g * grp + j
                rows = slice((p_i // 2) * VT_HEAD_ROWS, (p_i // 2 + 1) * VT_HEAD_ROWS)
                acc_s[p_i % 2, rows, :] = acc_s[p_i % 2, rows, :] * alpha[g][:, j * tq:(j + 1) * tq] + pv[p_i]

    @pl.when(n_piped > 0)
    def _():
        logits(0, span, 0, False)

        def body(jj, carry):
            off0 = pl.multiple_of(2 * jj * span, span)
            step(off0 + span, off0, 1)
            step(off0 + 2 * span, off0 + span, 0)
            return carry

        lax.fori_loop(0, n_piped // 2 - 1, body, 0)
        off0 = pl.multiple_of((n_piped - 2) * span, span)
        step(off0 + span, off0, 1)
        accumulate(off0 + span, span, 1)

    lq = lam_ref[...]
    lam = (jnp.exp(jnp.sum(lq[0:1] * lq[1:2], axis=1, keepdims=True))
           - jnp.exp(jnp.sum(lq[2:3] * lq[3:4], axis=1, keepdims=True)) + lambda_init)
    outs = []
    for h in range(C_HEADS):
        r0 = h * VT_HEAD_ROWS
        l_row = slice(r0 + VT_HEAD_DIM, r0 + VT_HEAD_DIM + 1)
        a1 = acc_s[0, r0:r0 + VT_HEAD_DIM, :] * (1.0 / acc_s[0, l_row, :])
        a2 = acc_s[1, r0:r0 + VT_HEAD_DIM, :] * (1.0 / acc_s[1, l_row, :])
        out = a1 - lam * a2
        rs = lax.rsqrt(jnp.mean(out * out, axis=0, keepdims=True) + EPS)
        outs.append(out * rs * sg_ref[...] * (1.0 - lambda_init))
    o_ref[0] = jnp.concatenate(outs, axis=0).T.astype(BF16)


def _diff(qt, vt, pn, lam_p, sg_col, tq, lambda_init):
    bsz, seq, _ = pn.shape
    grid = (bsz, seq // tq)
    n_pairs = 2 * C_HEADS
    return pl.pallas_call(
        functools.partial(_diff_kernel, tq=tq, seq=seq, lambda_init=lambda_init),
        grid=grid,
        in_specs=[
            pl.BlockSpec((1, WIDTH, tq), lambda b, i: (b, 2, i)),
            pl.BlockSpec((1, seq, WIDTH), lambda b, i: (b, 0, 3)),
            pl.BlockSpec((1, 1, VT_ROWS, seq), lambda b, i: (b, 1, 0, 0)),
            pl.BlockSpec(lam_p.shape, lambda b, i: (0, 0)),
            pl.BlockSpec(sg_col.shape, lambda b, i: (0, 0)),
        ],
        out_specs=pl.BlockSpec((1, tq, WIDTH), lambda b, i: (b, i, 0)),
        out_shape=jax.ShapeDtypeStruct((bsz, seq, WIDTH), BF16),
        scratch_shapes=[pltpu.VMEM((2, WIDTH, (n_pairs // 2) * tq), BF16),
                        pltpu.VMEM((2, VT_ROWS, tq), F32),
                        pltpu.VMEM((SUBLANES, n_pairs * tq), F32),
                        pltpu.VMEM((2, 2 * tq, (n_pairs // 2) * tq), F32),
                        pltpu.VMEM((2, 2 * tq, (n_pairs // 2) * tq), F32),
                        pltpu.VMEM((SUBLANES, n_pairs * tq), F32)],
        compiler_params=pltpu.CompilerParams(
            dimension_semantics=("parallel", "arbitrary"), vmem_limit_bytes=VMEM_LIMIT_BYTES),
        name="diff",
    )(qt, pn, vt, lam_p, sg_col)


def _gelu_tanh(x):
    return x * (0.5 * (1.0 + jnp.tanh(math.sqrt(2.0 / math.pi) * (x + 0.044715 * (x * x * x)))))


def _merge_kernel(x_ref, ya_ref, bu_ref, bv_ref, yc_ref, gt_ref, lg_ref, lb_ref, ws_ref, bs_ref,
                  wa_ref, wb_ref, wc_ref, wo_ref, o_ref, yb_s, *, tm):
    d = x_ref.shape[-1]
    lane_w = lax.broadcasted_iota(I32, (CHUNK, B_WIDTH), 1)
    group_of_lane = lane_w // (B_WIDTH // B_GROUPS)
    r_i = lax.broadcasted_iota(I32, (CHUNK, CHUNK), 0)
    c_i = lax.broadcasted_iota(I32, (CHUNK, CHUNK), 1)
    w_tril = [jnp.where(c_i <= r_i, ws_ref[g], 0.0).astype(BF16) for g in range(B_GROUPS)]
    for c in range(tm // CHUNK):
        rows = slice(c * CHUNK, (c + 1) * CHUNK)
        u = _gelu_tanh(bu_ref[0, rows, :].astype(F32))
        v = _gelu_tanh(bv_ref[0, rows, :].astype(F32))
        mu = jnp.mean(v, axis=-1, keepdims=True)
        vc = v - mu
        var = jnp.mean(vc * vc, axis=-1, keepdims=True)
        vn = (vc * lax.rsqrt(var + EPS) * lg_ref[...] + lb_ref[...]).astype(BF16)
        s = bs_ref[...]
        for g in range(B_GROUPS):
            s = s + _dot(w_tril[g], jnp.where(group_of_lane == g, vn, jnp.zeros_like(vn)))
        yb_s[rows, :] = (u * s).astype(BF16)

    merged = None
    branches = ((ya_ref[0], wa_ref), (yb_s[...], wb_ref), (yc_ref[0], wc_ref))
    for n, (y, w_ref) in enumerate(branches):
        y_half = _dot(y * 0.5, w_ref[...])
        term = y_half + y_half * jnp.tanh(gt_ref[0, :, n * d:(n + 1) * d].astype(F32))
        merged = term if merged is None else merged + term
    o_ref[0] = x_ref[0] + _dot(merged.astype(BF16), wo_ref[...])


def _merge(x, pn, ya, yc, gates, lg, lb, ws, bs_full, wa, wb, wc, wo, tm):
    bsz, seq, d = x.shape
    grid = (bsz, seq // tm)
    const2 = lambda b, i: (0, 0)
    return pl.pallas_call(
        functools.partial(_merge_kernel, tm=tm),
        grid=grid,
        in_specs=[
            pl.BlockSpec((1, tm, d), lambda b, i: (b, i, 0)),
            pl.BlockSpec((1, tm, WIDTH), lambda b, i: (b, i, 0)),
            pl.BlockSpec((1, tm, WIDTH), lambda b, i: (b, i, 1)),
            pl.BlockSpec((1, tm, WIDTH), lambda b, i: (b, i, 2)),
            pl.BlockSpec((1, tm, WIDTH), lambda b, i: (b, i, 0)),
            pl.BlockSpec((1, tm, 3 * d), lambda b, i: (b, i, 0)),
            pl.BlockSpec(lg.shape, const2),
            pl.BlockSpec(lb.shape, const2),
            pl.BlockSpec(ws.shape, lambda b, i: (0, 0, 0)),
            pl.BlockSpec(bs_full.shape, const2),
            pl.BlockSpec(wa.shape, const2),
            pl.BlockSpec(wb.shape, const2),
            pl.BlockSpec(wc.shape, const2),
            pl.BlockSpec(wo.shape, const2),
        ],
        out_specs=pl.BlockSpec((1, tm, d), lambda b, i: (b, i, 0)),
        out_shape=jax.ShapeDtypeStruct((bsz, seq, d), F32),
        scratch_shapes=[pltpu.VMEM((tm, B_WIDTH), BF16)],
        compiler_params=pltpu.CompilerParams(
            dimension_semantics=("parallel", "parallel"), vmem_limit_bytes=VMEM_LIMIT_BYTES),
        name="merge",
    )(x, ya, pn, pn, yc, gates, lg, lb, ws, bs_full, wa, wb, wc, wo)


def _ffn_kernel(x_ref, g_ref, w1_ref, w2_ref, fg_ref, o_ref, h_s, acc_s, *, final_norm):
    j = pl.program_id(1)

    @pl.when(j == 0)
    def _():
        x = x_ref[...]
        ms = jnp.mean(x * x, axis=-1, keepdims=True)
        h_s[...] = (x * lax.rsqrt(ms + EPS) * g_ref[...]).astype(BF16)
        acc_s[...] = jnp.zeros_like(acc_s)

    a = jnp.maximum(_dot(h_s[...], w1_ref[...]), 0.0)
    acc_s[...] += _dot((a * a).astype(BF16), w2_ref[...])

    @pl.when(j == pl.num_programs(1) - 1)
    def _():
        y = x_ref[...] + acc_s[...]
        if final_norm:
            ms = jnp.mean(y * y, axis=-1, keepdims=True)
            y = y * lax.rsqrt(ms + EPS) * fg_ref[...]
        o_ref[...] = y


def _ffn(x2d, g, w1, w2, fg, tm, tf, final_norm):
    m, d = x2d.shape
    dff = w1.shape[1]
    grid = (m // tm, dff // tf)
    return pl.pallas_call(
        functools.partial(_ffn_kernel, final_norm=final_norm),
        grid=grid,
        in_specs=[
            pl.BlockSpec((tm, d), lambda i, j: (i, 0)),
            pl.BlockSpec((1, d), lambda i, j: (0, 0)),
            pl.BlockSpec((d, tf), lambda i, j: (0, j)),
            pl.BlockSpec((tf, d), lambda i, j: (j, 0)),
            pl.BlockSpec((1, d), lambda i, j: (0, 0)),
        ],
        out_specs=pl.BlockSpec((tm, d), lambda i, j: (i, 0)),
        out_shape=jax.ShapeDtypeStruct((m, d), F32),
        scratch_shapes=[pltpu.VMEM((tm, d), BF16), pltpu.VMEM((tm, d), F32)],
        compiler_params=pltpu.CompilerParams(
            dimension_semantics=("parallel", "arbitrary"), vmem_limit_bytes=VMEM_LIMIT_BYTES),
        name="ffn",
    )(x2d, g, w1, w2, fg)


def _tile(n, pref):
    t = min(n, pref)
    assert n % t == 0, (n, t)
    return t


def kernel(x, attn_norm_g, w_in, idx_k_norm_g, idx_k_norm_b, sgu_norm_g, sgu_norm_b, sgu_w_s, sgu_b_s,
           diff_lambda, diff_subln_g, w_branch_a, w_branch_b, w_branch_c, w_out, mlp_norm_g, w_ff1,
           w_ff2, final_norm_g):
    bsz, seq, d = x.shape
    depth = w_in.shape[0]
    offs = [0]
    for s in IN_SIZES:
        offs.append(offs[-1] + s)
    (o_aq, o_ak, o_av, o_iq, o_ik, o_iw, o_buv, o_cq, o_ck, o_cv, o_g, o_end) = offs

    tm_proj = _tile(seq, 512)
    tq = _tile(seq, 256)
    tm_merge = _tile(seq, 512)
    tm_ffn = _tile(bsz * seq, 1024)
    tf = _tile(w_ff1.shape[2], 1024)

    for l in range(depth):
        lambda_init = 0.8 - 0.6 * math.exp(-0.3 * l)
        w = w_in[l]
        cols = lambda a, b: w[:, a:b]
        wn = jnp.concatenate([cols(o_ak, o_av), cols(o_buv, o_cq), cols(o_ck, o_cv)], axis=1).astype(BF16)
        wt = jnp.concatenate([cols(o_aq, o_ak), cols(o_iq, o_ik), cols(o_cq, o_ck), cols(o_av, o_iq),
                              cols(o_cv, o_g), cols(o_iw, o_buv),
                              jnp.zeros((d, SUBLANES - IDX_HEADS), w.dtype)], axis=1).T.astype(BF16)
        wik = cols(o_ik, o_iw).astype(BF16)
        wg = cols(o_g, o_end).astype(BF16)

        pn, ikn, gates, qt, vt, iwt = _proj(
            x, attn_norm_g[l][None, :], wn, wik, wg, wt,
            idx_k_norm_g[l][None, :], idx_k_norm_b[l][None, :], tm_proj)

        ya = _dsa(qt, vt, pn, ikn, iwt, tq)
        sg_col = diff_subln_g[l][:, None]
        yc = _diff(qt, vt, pn, diff_lambda[l], sg_col, tq, lambda_init)

        bs_full = jnp.repeat(sgu_b_s[l].T, B_WIDTH // B_GROUPS, axis=1)
        x = _merge(x, pn, ya, yc, gates, sgu_norm_g[l][None, :], sgu_norm_b[l][None, :], sgu_w_s[l],
                   bs_full, w_branch_a[l].astype(BF16), w_branch_b[l].astype(BF16),
                   w_branch_c[l].astype(BF16), w_out[l].astype(BF16), tm_merge)

        x = _ffn(x.reshape(bsz * seq, d), mlp_norm_g[l][None, :], w_ff1[l].astype(BF16),
                 w_ff2[l].astype(BF16), final_norm_g[None, :], tm_ffn, tf,
                 final_norm=(l == depth - 1)).reshape(bsz, seq, d)
    return x
```

```python
import functools
import math

import jax
import jax.numpy as jnp
from jax import lax
from jax.experimental import pallas as pl
from jax.experimental.pallas import tpu as pltpu

F32 = jnp.float32
BF16 = jnp.bfloat16
I32 = jnp.int32
I16 = jnp.int16

A_HEADS = 4
A_HEAD_DIM = 64
IDX_HEADS = 4
IDX_DIM = 64
TOPK_MAX = 256
B_GROUPS = 4
B_WIDTH = 256
CHUNK = 128
C_HEADS = 4
C_HEAD_DIM = 32
C_V_DIM = 64
WIDTH = 256
EPS = 1e-6
NEG = -1e30
INT16_MIN = -(2 ** 15)
TIE_BIG = 16384.0

LANES = 128
SUBLANES = 8
PACK16_ROWS = 2 * SUBLANES

LOG2E = math.log2(math.e)
VT_HEAD_DIM = 64
VT_HEAD_ROWS = VT_HEAD_DIM + PACK16_ROWS
VT_ROWS = (WIDTH // VT_HEAD_DIM) * VT_HEAD_ROWS
VMEM_LIMIT_BYTES = 56 * 1024 * 1024

IN_SIZES = (256, 256, 256, 256, 64, 4, 512, 256, 256, 256, 3072)
N_COUNT_ACC = 4


def _nt_dot(a, b):
    return lax.dot_general(a, b, (((1,), (1,)), ((), ())), preferred_element_type=F32)


def _dot(a, b):
    return jnp.dot(a, b, preferred_element_type=F32)


CHUNKS_PER_TRIP = 4


def _for_chunks(n, body):
    def group(j, carry):
        for u in range(CHUNKS_PER_TRIP):
            body(CHUNKS_PER_TRIP * j + u)
        return carry

    lax.fori_loop(0, n // CHUNKS_PER_TRIP, group, 0)

    def single(c, carry):
        body(c)
        return carry

    lax.fori_loop(CHUNKS_PER_TRIP * (n // CHUNKS_PER_TRIP), n, single, 0)


def _proj_kernel(x_ref, g_ref, wn_ref, wik_ref, wg_ref, wt_ref, ikg_ref, ikb_ref,
                 pn_ref, ikn_ref, gt_ref, qt_ref, vt_ref, iwt_ref, *, gate_chunk):
    x = x_ref[0]
    tm = x.shape[0]
    ms = jnp.mean(x * x, axis=-1, keepdims=True)
    h = (x * lax.rsqrt(ms + EPS) * g_ref[...]).astype(BF16)
    for c0 in range(0, wn_ref.shape[1], WIDTH):
        pn_ref[0, :, c0:c0 + WIDTH] = _dot(h, wn_ref[:, c0:c0 + WIDTH]).astype(BF16)
    for c0 in range(0, wg_ref.shape[1], gate_chunk):
        gt_ref[0, :, c0:c0 + gate_chunk] = (_dot(h, wg_ref[:, c0:c0 + gate_chunk]) * 0.5).astype(BF16)
    for blk, q_scale in enumerate((A_HEAD_DIM ** -0.5 * LOG2E, None, C_HEAD_DIM ** -0.5 * LOG2E)):
        r = _nt_dot(wt_ref[blk * WIDTH:(blk + 1) * WIDTH, :], h)
        if q_scale is not None:
            r = r * q_scale
        qt_ref[0, blk * WIDTH:(blk + 1) * WIDTH, :] = r.astype(BF16)
    for blk in range(2):
        r0 = (3 + blk) * WIDTH
        r = _nt_dot(wt_ref[r0:r0 + WIDTH, :], h).astype(BF16)
        for hd in range(WIDTH // VT_HEAD_DIM):
            o0 = hd * VT_HEAD_ROWS
            vt_ref[0, blk, o0:o0 + VT_HEAD_DIM, :] = r[hd * VT_HEAD_DIM:(hd + 1) * VT_HEAD_DIM, :]
            vt_ref[0, blk, o0 + VT_HEAD_DIM:o0 + VT_HEAD_ROWS, :] = jnp.ones(
                (VT_HEAD_ROWS - VT_HEAD_DIM, tm), BF16)
    iwt_ref[0] = _nt_dot(wt_ref[5 * WIDTH:5 * WIDTH + SUBLANES, :], h)
    ik = _dot(h, wik_ref[...])
    mu = jnp.mean(ik, axis=-1, keepdims=True)
    xc = ik - mu
    var = jnp.mean(xc * xc, axis=-1, keepdims=True)
    ikn_ref[0] = (xc * lax.rsqrt(var + EPS) * ikg_ref[...] + ikb_ref[...]).astype(BF16)


def _proj(x, g, wn, wik, wg, wt, ikg, ikb, tm):
    bsz, seq, d = x.shape
    n_nat, n_gate = wn.shape[1], wg.shape[1]
    assert wt.shape[0] == 5 * WIDTH + SUBLANES
    grid = (bsz, seq // tm)
    const = lambda b, i: (0, 0)
    return pl.pallas_call(
        functools.partial(_proj_kernel, gate_chunk=min(n_gate, 768)),
        grid=grid,
        in_specs=[
            pl.BlockSpec((1, tm, d), lambda b, i: (b, i, 0)),
            pl.BlockSpec((1, d), const),
            pl.BlockSpec(wn.shape, const),
            pl.BlockSpec(wik.shape, const),
            pl.BlockSpec(wg.shape, const),
            pl.BlockSpec(wt.shape, const),
            pl.BlockSpec(ikg.shape, const),
            pl.BlockSpec(ikb.shape, const),
        ],
        out_specs=[
            pl.BlockSpec((1, tm, n_nat), lambda b, i: (b, i, 0)),
            pl.BlockSpec((1, tm, IDX_DIM), lambda b, i: (b, i, 0)),
            pl.BlockSpec((1, tm, n_gate), lambda b, i: (b, i, 0)),
            pl.BlockSpec((1, 3 * WIDTH, tm), lambda b, i: (b, 0, i)),
            pl.BlockSpec((1, 2, VT_ROWS, tm), lambda b, i: (b, 0, 0, i)),
            pl.BlockSpec((1, SUBLANES, tm), lambda b, i: (b, 0, i)),
        ],
        out_shape=[
            jax.ShapeDtypeStruct((bsz, seq, n_nat), BF16),
            jax.ShapeDtypeStruct((bsz, seq, IDX_DIM), BF16),
            jax.ShapeDtypeStruct((bsz, seq, n_gate), BF16),
            jax.ShapeDtypeStruct((bsz, 3 * WIDTH, seq), BF16),
            jax.ShapeDtypeStruct((bsz, 2, VT_ROWS, seq), BF16),
            jax.ShapeDtypeStruct((bsz, SUBLANES, seq), F32),
        ],
        compiler_params=pltpu.CompilerParams(
            dimension_semantics=("parallel", "parallel"), vmem_limit_bytes=VMEM_LIMIT_BYTES),
        name="proj",
    )(x, g, wn, wik, wg, wt, ikg, ikb)


def _dsa_kernel(qt_ref, iqt_ref, iwt_ref, ikn_ref, k_ref, vt_ref, o_ref,
                hi_s, lo_s, bias_s, qm_s, acc_s, m_s, tri_s, s0_s, s1_s, lm_s, *, tq, top_k):
    i = pl.program_id(1)
    kc = tq
    n_chunks = i + 1
    n_grp = kc // PACK16_ROWS
    q_pos = i * tq + lax.broadcasted_iota(I32, (1, tq), 1)
    one16 = jnp.ones((PACK16_ROWS, tq), I16)
    zero16 = jnp.zeros((PACK16_ROWS, tq), I16)

    def rows16(v):
        return jnp.broadcast_to(v, (PACK16_ROWS, tq)).astype(I16)

    w_all = iwt_ref[0] * (IDX_DIM ** -0.5 * IDX_HEADS ** -0.5)
    w_rows = [w_all[h:h + 1, :] for h in range(IDX_HEADS)]
    for h in range(IDX_HEADS):
        qm_s[0:IDX_DIM, h * tq:(h + 1) * tq] = iqt_ref[0, h * IDX_DIM:(h + 1) * IDX_DIM, :]

    def score_body(c, diagonal):
        off = pl.multiple_of(c * kc, kc)
        ik = ikn_ref[0, pl.ds(off, kc), :]
        d = _dot(ik, qm_s[0:IDX_DIM, :])
        sc = jnp.zeros((kc, tq), F32)
        for h in range(IDX_HEADS):
            sc = sc + jnp.maximum(d[:, h * tq:(h + 1) * tq], 0.0) * w_rows[h]
        if diagonal:
            key_pos = off + lax.broadcasted_iota(I32, (kc, tq), 0)
            sc = jnp.where(key_pos <= q_pos, sc, -jnp.inf)
        bits = pltpu.bitcast(sc, I32)
        key = bits ^ ((bits >> 31) & 0x7FFFFFFF)
        hi_s[pl.ds(off, kc), :] = (key >> 16).astype(I16)
        lo_s[pl.ds(off, kc), :] = ((key & 0xFFFF) - 0x8000).astype(I16)

    _for_chunks(i, lambda c: score_body(c, False))
    score_body(i, True)

    def count_ge(ref, cand):
        cb = rows16(cand)

        def body(c, accs):
            off = pl.multiple_of(c * kc, kc)
            slab = ref[pl.ds(off, kc), :]
            accs = list(accs)
            for r in range(n_grp):
                kk = slab[r * PACK16_ROWS:(r + 1) * PACK16_ROWS, :]
                accs[r % N_COUNT_ACC] = accs[r % N_COUNT_ACC] + jnp.where(kk >= cb, one16, zero16)
            return tuple(accs)

        accs = lax.fori_loop(0, n_chunks, body, (zero16,) * N_COUNT_ACC)
        tot = accs[0]
        for a in accs[1:]:
            tot = tot + a
        return jnp.sum(tot.astype(I32), axis=0, keepdims=True)

    def descend(ref, need):
        c0 = count_ge(ref, jnp.zeros((1, tq), I32))
        ok0 = c0 >= need
        thr = jnp.where(ok0, 0, INT16_MIN).astype(I32)
        c_gt = jnp.where(ok0, 0, c0)

        def bit_body(j, carry):
            thr, c_gt = carry
            cand = thr | (jnp.int32(1) << (14 - j))
            cnt = count_ge(ref, cand)
            ok = cnt >= need
            return jnp.where(ok, cand, thr), jnp.where(ok, c_gt, cnt)

        return lax.fori_loop(0, 15, bit_body, (thr, c_gt))

    thr_hi, c_gt_hi = descend(hi_s, top_k)
    thr_hi16 = rows16(thr_hi)
    need_lo = top_k - c_gt_hi

    def restrict_body(c, carry):
        off = pl.multiple_of(c * kc, kc)
        in_bucket = hi_s[pl.ds(off, kc), :] == jnp.concatenate([thr_hi16] * n_grp, axis=0)
        lo_s[pl.ds(off, kc), :] = jnp.where(in_bucket, lo_s[pl.ds(off, kc), :], jnp.int16(INT16_MIN))
        return carry

    lax.fori_loop(0, n_chunks, restrict_body, 0)
    thr_lo, c_gt_lo = descend(lo_s, need_lo)
    thr_lo16 = rows16(thr_lo)
    need_tie = (need_lo - c_gt_lo).astype(F32)

    r_i = lax.broadcasted_iota(I32, (kc, kc), 0)
    c_i = lax.broadcasted_iota(I32, (kc, kc), 1)
    tri_s[0:kc, 0:kc] = jnp.where(c_i <= r_i, 1.0, 0.0).astype(BF16)
    tri_s[0:kc, kc:2 * kc] = jnp.where(c_i == r_i, TIE_BIG, 0.0).astype(BF16)
    tri_s[kc:kc + PACK16_ROWS, 0:kc] = jnp.ones((PACK16_ROWS, kc), BF16)
    tri_s[kc:kc + PACK16_ROWS, kc:2 * kc] = jnp.zeros((PACK16_ROWS, kc), BF16)
    thr_hi_kc = jnp.concatenate([thr_hi16] * n_grp, axis=0)
    thr_lo_kc = jnp.concatenate([thr_lo16] * n_grp, axis=0)
    one_b, zero_b, neg_b = (jnp.full((kc, tq), v, BF16) for v in (1.0, 0.0, -1.0))

    def bias_body(c, seen, diagonal):
        off = pl.multiple_of(c * kc, kc)
        hh = hi_s[pl.ds(off, kc), :]
        ll = lo_s[pl.ds(off, kc), :]
        in_bucket = hh == thr_hi_kc
        lo_eq = ll == thr_lo_kc
        equal = jnp.where(in_bucket, jnp.where(lo_eq, one_b, zero_b), zero_b)
        side = jnp.where(hh > thr_hi_kc, neg_b,
                         jnp.where(in_bucket, jnp.where(ll > thr_lo_kc, neg_b, jnp.where(lo_eq, zero_b, one_b)),
                                   one_b))
        rank = _dot(tri_s[...], jnp.concatenate([equal, side], axis=0))
        b = jnp.where(rank[0:kc, :] <= need_tie - seen, 0.0, NEG)
        if diagonal:
            key_pos = off + lax.broadcasted_iota(I32, (kc, tq), 0)
            b = jnp.where(key_pos <= q_pos, b, NEG)
        bias_s[pl.ds(off, kc), :] = b
        return seen + rank[kc:kc + 1, :]

    seen = lax.fori_loop(0, i // 2, lambda j, seen: bias_body(2 * j + 1, bias_body(2 * j, seen, False), False),
                         jnp.zeros((1, tq), F32))

    @pl.when(i % 2 == 0)
    def _():
        bias_body(i, seen, True)

    @pl.when(i % 2 == 1)
    def _():
        bias_body(i, bias_body(i - 1, seen, False), True)

    feat = lax.broadcasted_iota(I32, (WIDTH, tq), 0)
    qt = qt_ref[0]
    for h in range(A_HEADS):
        qm_s[:, h * tq:(h + 1) * tq] = jnp.where(feat // A_HEAD_DIM == h, qt, jnp.zeros_like(qt))
    acc_s[...] = jnp.zeros_like(acc_s)
    m_s[0:1, :] = jnp.full((1, A_HEADS * tq), NEG, F32)

    span = 2 * kc

    def logits_block(off):
        bias = bias_s[pl.ds(off, kc), :]
        return _dot(k_ref[0, pl.ds(off, kc), :], qm_s[...]) + jnp.concatenate([bias] * A_HEADS, axis=1)

    def pv_block(off, pb, h):
        rows = slice(h * VT_HEAD_ROWS, (h + 1) * VT_HEAD_ROWS)
        return _dot(vt_ref[0, 0, rows, pl.ds(off, kc)], pb[:, h * tq:(h + 1) * tq])

    def rescale_and_add(alpha, pv):
        for h in range(A_HEADS):
            rows = slice(h * VT_HEAD_ROWS, (h + 1) * VT_HEAD_ROWS)
            acc_s[rows, :] = acc_s[rows, :] * alpha[:, h * tq:(h + 1) * tq] + pv[h]

    def attend(off, n_blk):
        s = [logits_block(off + r * kc) for r in range(n_blk)]
        m_old = m_s[0:1, :]
        m_new = m_old
        for r in range(n_blk):
            m_new = jnp.maximum(m_new, jnp.max(s[r], axis=0, keepdims=True))
        m_s[0:1, :] = m_new
        pv = [None] * A_HEADS
        for r in range(n_blk):
            pb = jnp.exp2(s[r] - m_new).astype(BF16)
            for h in range(A_HEADS):
                d = pv_block(off + r * kc, pb, h)
                pv[h] = d if pv[h] is None else pv[h] + d
        rescale_and_add(jnp.exp2(m_old - m_new), pv)

    def logits_span(off, buf):
        s_ref = (s0_s, s1_s)[buf]
        lm = None
        for r in range(span // kc):
            s = logits_block(off + r * kc)
            s_ref[r * kc:(r + 1) * kc, :] = s
            lm_r = jnp.max(s, axis=0, keepdims=True)
            lm = lm_r if lm is None else jnp.maximum(lm, lm_r)
        lm_s[buf:buf + 1, :] = lm

    def step(off_next, off_cur, buf_next):
        buf_cur = 1 - buf_next
        s_next, s_cur = (s0_s, s1_s)[buf_next], (s0_s, s1_s)[buf_cur]
        m_old = m_s[0:1, :]
        m_new = jnp.maximum(m_old, lm_s[buf_cur:buf_cur + 1, :])
        m_s[0:1, :] = m_new
        lm, pv = None, [None] * A_HEADS
        for r in range(span // kc):
            blk = slice(r * kc, (r + 1) * kc)
            s = logits_block(off_next + r * kc)
            s_next[blk, :] = s
            lm_r = jnp.max(s, axis=0, keepdims=True)
            lm = lm_r if lm is None else jnp.maximum(lm, lm_r)
            pb = jnp.exp2(s_cur[blk, :] - m_new).astype(BF16)
            for h in range(A_HEADS):
                d = pv_block(off_cur + r * kc, pb, h)
                pv[h] = d if pv[h] is None else pv[h] + d
        lm_s[buf_next:buf_next + 1, :] = lm
        rescale_and_add(jnp.exp2(m_old - m_new), pv)

    n_full = n_chunks // 2

    @pl.when(n_chunks % 2 == 1)
    def _():
        attend(pl.multiple_of(i * kc, kc), 1)

    def drain(off_cur, buf_cur):
        s_cur = (s0_s, s1_s)[buf_cur]
        m_old = m_s[0:1, :]
        m_new = jnp.maximum(m_old, lm_s[buf_cur:buf_cur + 1, :])
        m_s[0:1, :] = m_new
        pv = [None] * A_HEADS
        for r in range(span // kc):
            pb = jnp.exp2(s_cur[r * kc:(r + 1) * kc, :] - m_new).astype(BF16)
            for h in range(A_HEADS):
                d = pv_block(off_cur + r * kc, pb, h)
                pv[h] = d if pv[h] is None else pv[h] + d
        rescale_and_add(jnp.exp2(m_old - m_new), pv)

    @pl.when(n_full > 0)
    def _():
        logits_span(0, 0)
        n_steps = n_full - 1

        def body(jj, carry):
            off0 = pl.multiple_of(2 * jj * span, span)
            step(off0 + span, off0, 1)
            step(off0 + 2 * span, off0 + span, 0)
            return carry

        lax.fori_loop(0, n_steps // 2, body, 0)

        @pl.when(n_steps % 2 == 1)
        def _():
            off_cur = pl.multiple_of((n_steps - 1) * span, span)
            step(off_cur + span, off_cur, 1)

        off_last = pl.multiple_of((n_full - 1) * span, span)
        for parity in range(2):
            @pl.when((n_full - 1) % 2 == parity)
            def _():
                drain(off_last, parity)

    out = []
    for h in range(A_HEADS):
        r0 = h * VT_HEAD_ROWS
        out.append(acc_s[r0:r0 + VT_HEAD_DIM, :] * (1.0 / acc_s[r0 + VT_HEAD_DIM:r0 + VT_HEAD_DIM + 1, :]))
    o_ref[0] = jnp.concatenate(out, axis=0).T.astype(BF16)


def _dsa(qt, vt, pn, ikn, iwt, tq):
    bsz, seq, _ = pn.shape
    top_k = min(TOPK_MAX, seq // 4)
    grid = (bsz, seq // tq)
    return pl.pallas_call(
        functools.partial(_dsa_kernel, tq=tq, top_k=top_k),
        grid=grid,
        in_specs=[
            pl.BlockSpec((1, WIDTH, tq), lambda b, i: (b, 0, i)),
            pl.BlockSpec((1, WIDTH, tq), lambda b, i: (b, 1, i)),
            pl.BlockSpec((1, SUBLANES, tq), lambda b, i: (b, 0, i)),
            pl.BlockSpec((1, seq, IDX_DIM), lambda b, i: (b, 0, 0)),
            pl.BlockSpec((1, seq, WIDTH), lambda b, i: (b, 0, 0)),
            pl.BlockSpec((1, 1, VT_ROWS, seq), lambda b, i: (b, 0, 0, 0)),
        ],
        out_specs=pl.BlockSpec((1, tq, WIDTH), lambda b, i: (b, i, 0)),
        out_shape=jax.ShapeDtypeStruct((bsz, seq, WIDTH), BF16),
        scratch_shapes=[pltpu.VMEM((seq, tq), I16),
                        pltpu.VMEM((seq, tq), I16),
                        pltpu.VMEM((seq, tq), F32),
                        pltpu.VMEM((WIDTH, A_HEADS * tq), BF16),
                        pltpu.VMEM((VT_ROWS, tq), F32),
                        pltpu.VMEM((SUBLANES, A_HEADS * tq), F32),
                        pltpu.VMEM((tq + PACK16_ROWS, 2 * tq), BF16),
                        pltpu.VMEM((2 * tq, A_HEADS * tq), F32),
                        pltpu.VMEM((2 * tq, A_HEADS * tq), F32),
                        pltpu.VMEM((SUBLANES, A_HEADS * tq), F32)],
        compiler_params=pltpu.CompilerParams(
            dimension_semantics=("parallel", "arbitrary"), vmem_limit_bytes=VMEM_LIMIT_BYTES),
        name="dsa",
    )(qt, qt, iwt, ikn, pn, vt)


def _diff_kernel(qt_ref, k_ref, vt_ref, lam_ref, sg_ref, o_ref, qm_s, acc_s, m_s, s0_s, s1_s, lm_s,
                 *, tq, seq, lambda_init):
    i = pl.program_id(1)
    span = 2 * tq
    n_pairs = 2 * C_HEADS
    grp = n_pairs // 2
    feat = lax.broadcasted_iota(I32, (WIDTH, tq), 0)
    qt = qt_ref[0]
    for p_i in range(n_pairs):
        qm_s[p_i // grp, :, (p_i % grp) * tq:(p_i % grp + 1) * tq] = jnp.where(
            feat // C_HEAD_DIM == p_i, qt, jnp.zeros_like(qt))
    acc_s[...] = jnp.zeros_like(acc_s)
    m_s[0:1, :] = jnp.full((1, n_pairs * tq), NEG, F32)
    diag_ok = lax.broadcasted_iota(I32, (tq, tq), 0) <= lax.broadcasted_iota(I32, (tq, tq), 1)
    diag_ok = jnp.concatenate([diag_ok] * grp, axis=1)

    def logits(off, width, buf, mask_tail):
        s_ref = (s0_s, s1_s)[buf]
        k_c = k_ref[0, pl.ds(off, width), :]
        for g in range(2):
            s = _dot(k_c, qm_s[g])
            if mask_tail:
                head = s[0:width - tq, :]
                tail = jnp.where(diag_ok, s[width - tq:width, :], NEG)
                s = jnp.concatenate([head, tail], axis=0) if width > tq else tail
            s_ref[g, 0:width, :] = s
            lm_s[buf:buf + 1, g * grp * tq:(g + 1) * grp * tq] = jnp.max(s, axis=0, keepdims=True)

    def accumulate(off, width, buf):
        s_ref = (s0_s, s1_s)[buf]
        for g in range(2):
            st = slice(g * grp * tq, (g + 1) * grp * tq)
            m_old = m_s[0:1, st]
            m_new = jnp.maximum(m_old, lm_s[buf:buf + 1, st])
            alpha = jnp.exp2(m_old - m_new)
            pb = jnp.exp2(s_ref[g, 0:width, :] - m_new).astype(BF16)
            m_s[0:1, st] = m_new
            for j in range(grp):
                h, comp = (g * grp + j) // 2, (g * grp + j) % 2
                rows = slice(h * VT_HEAD_ROWS, (h + 1) * VT_HEAD_ROWS)
                cols = slice(j * tq, (j + 1) * tq)
                pv = _dot(vt_ref[0, 0, rows, pl.ds(off, width)], pb[:, cols])
                acc_s[comp, rows, :] = acc_s[comp, rows, :] * alpha[:, cols] + pv

    @pl.when(i % 2 == 0)
    def _():
        off = pl.multiple_of(i * tq, tq)
        logits(off, tq, 0, True)
        accumulate(off, tq, 0)

    @pl.when(i % 2 == 1)
    def _():
        off = pl.multiple_of((i - 1) * tq, span)
        logits(off, span, 0, True)
        accumulate(off, span, 0)

    n_full = i // 2

    def step(off_next, off_cur, buf_next):
        buf_cur = 1 - buf_next
        s_next, s_cur = (s0_s, s1_s)[buf_next], (s0_s, s1_s)[buf_cur]
        m_new, alpha, lm_run = [], [], []
        for g in range(2):
            st = slice(g * grp * tq, (g + 1) * grp * tq)
            m_old = m_s[0:1, st]
            m_new.append(jnp.maximum(m_old, lm_s[buf_cur:buf_cur + 1, st]))
            alpha.append(jnp.exp2(m_old - m_new[g]))
            m_s[0:1, st] = m_new[g]
            lm_run.append(jnp.full((1, grp * tq), NEG, F32))
        pv = [None] * n_pairs
        for r in range(span // tq):
            blk = slice(r * tq, (r + 1) * tq)
            k_blk = k_ref[0, pl.ds(off_next + r * tq, tq), :]
            for g in range(2):
                s = _dot(k_blk, qm_s[g])
                s_next[g, blk, :] = s
                lm_run[g] = jnp.maximum(lm_run[g], jnp.max(s, axis=0, keepdims=True))
                pb = jnp.exp2(s_cur[g, blk, :] - m_new[g]).astype(BF16)
                for j in range(grp):
                    p_i = g * grp + j
                    rows = slice((p_i // 2) * VT_HEAD_ROWS, (p_i // 2 + 1) * VT_HEAD_ROWS)
                    d = _dot(vt_ref[0, 0, rows, pl.ds(off_cur + r * tq, tq)], pb[:, j * tq:(j + 1) * tq])
                    pv[p_i] = d if pv[p_i] is None else pv[p_i] + d
        for g in range(2):
            lm_s[buf_next:buf_next + 1, g * grp * tq:(g + 1) * grp * tq] = lm_run[g]
            for j in range(grp):
                p_i = g * grp + j
                rows = slice((p_i // 2) * VT_HEAD_ROWS, (p_i // 2 + 1) * VT_HEAD_ROWS)
                acc_s[p_i % 2, rows, :] = acc_s[p_i % 2, rows, :] * alpha[g][:, j * tq:(j + 1) * tq] + pv[p_i]

    @pl.when(n_full > 0)
    def _():
        logits(0, span, 0, False)
        n_steps = n_full - 1

        def body(jj, carry):
            off0 = pl.multiple_of(2 * jj * span, span)
            step(off0 + span, off0, 1)
            step(off0 + 2 * span, off0 + span, 0)
            return carry

        lax.fori_loop(0, n_steps // 2, body, 0)

        @pl.when(n_steps % 2 == 1)
        def _():
            off_cur = pl.multiple_of((n_steps - 1) * span, span)
            step(off_cur + span, off_cur, 1)

        off_last = pl.multiple_of((n_full - 1) * span, span)
        for parity in range(2):
            @pl.when((n_full - 1) % 2 == parity)
            def _():
                accumulate(off_last, span, parity)

    lq = lam_ref[...]
    lam = (jnp.exp(jnp.sum(lq[0:1] * lq[1:2], axis=1, keepdims=True))
           - jnp.exp(jnp.sum(lq[2:3] * lq[3:4], axis=1, keepdims=True)) + lambda_init)
    outs = []
    for h in range(C_HEADS):
        r0 = h * VT_HEAD_ROWS
        l_row = slice(r0 + VT_HEAD_DIM, r0 + VT_HEAD_DIM + 1)
        a1 = acc_s[0, r0:r0 + VT_HEAD_DIM, :] * (1.0 / acc_s[0, l_row, :])
        a2 = acc_s[1, r0:r0 + VT_HEAD_DIM, :] * (1.0 / acc_s[1, l_row, :])
        out = a1 - lam * a2
        rs = lax.rsqrt(jnp.mean(out * out, axis=0, keepdims=True) + EPS)
        outs.append(out * rs * sg_ref[...] * (1.0 - lambda_init))
    o_ref[0] = jnp.concatenate(outs, axis=0).T.astype(BF16)


def _diff(qt, vt, pn, lam_p, sg_col, tq, lambda_init):
    bsz, seq, _ = pn.shape
    grid = (bsz, seq // tq)
    n_pairs = 2 * C_HEADS
    return pl.pallas_call(
        functools.partial(_diff_kernel, tq=tq, seq=seq, lambda_init=lambda_init),
        grid=grid,
        in_specs=[
            pl.BlockSpec((1, WIDTH, tq), lambda b, i: (b, 2, i)),
            pl.BlockSpec((1, seq, WIDTH), lambda b, i: (b, 0, 3)),
            pl.BlockSpec((1, 1, VT_ROWS, seq), lambda b, i: (b, 1, 0, 0)),
            pl.BlockSpec(lam_p.shape, lambda b, i: (0, 0)),
            pl.BlockSpec(sg_col.shape, lambda b, i: (0, 0)),
        ],
        out_specs=pl.BlockSpec((1, tq, WIDTH), lambda b, i: (b, i, 0)),
        out_shape=jax.ShapeDtypeStruct((bsz, seq, WIDTH), BF16),
        scratch_shapes=[pltpu.VMEM((2, WIDTH, (n_pairs // 2) * tq), BF16),
                        pltpu.VMEM((2, VT_ROWS, tq), F32),
                        pltpu.VMEM((SUBLANES, n_pairs * tq), F32),
                        pltpu.VMEM((2, 2 * tq, (n_pairs // 2) * tq), F32),
                        pltpu.VMEM((2, 2 * tq, (n_pairs // 2) * tq), F32),
                        pltpu.VMEM((SUBLANES, n_pairs * tq), F32)],
        compiler_params=pltpu.CompilerParams(
            dimension_semantics=("parallel", "arbitrary"), vmem_limit_bytes=VMEM_LIMIT_BYTES),
        name="diff",
    )(qt, pn, vt, lam_p, sg_col)


def _gelu_tanh(x):
    return x * (0.5 * (1.0 + jnp.tanh(math.sqrt(2.0 / math.pi) * (x + 0.044715 * (x * x * x)))))


def _merge_kernel(x_ref, ya_ref, bu_ref, bv_ref, yc_ref, gt_ref, lg_ref, lb_ref, ws_ref, bs_ref,
                  wa_ref, wb_ref, wc_ref, wo_ref, o_ref, yb_s, *, tm):
    d = x_ref.shape[-1]
    lane_w = lax.broadcasted_iota(I32, (CHUNK, B_WIDTH), 1)
    group_of_lane = lane_w // (B_WIDTH // B_GROUPS)
    r_i = lax.broadcasted_iota(I32, (CHUNK, CHUNK), 0)
    c_i = lax.broadcasted_iota(I32, (CHUNK, CHUNK), 1)
    w_tril = [jnp.where(c_i <= r_i, ws_ref[g], 0.0).astype(BF16) for g in range(B_GROUPS)]
    for c in range(tm // CHUNK):
        rows = slice(c * CHUNK, (c + 1) * CHUNK)
        u = _gelu_tanh(bu_ref[0, rows, :].astype(F32))
        v = _gelu_tanh(bv_ref[0, rows, :].astype(F32))
        mu = jnp.mean(v, axis=-1, keepdims=True)
        vc = v - mu
        var = jnp.mean(vc * vc, axis=-1, keepdims=True)
        vn = (vc * lax.rsqrt(var + EPS) * lg_ref[...] + lb_ref[...]).astype(BF16)
        s = bs_ref[...]
        for g in range(B_GROUPS):
            s = s + _dot(w_tril[g], jnp.where(group_of_lane == g, vn, jnp.zeros_like(vn)))
        yb_s[rows, :] = (u * s).astype(BF16)

    merged = None
    branches = ((ya_ref[0], wa_ref), (yb_s[...], wb_ref), (yc_ref[0], wc_ref))
    for n, (y, w_ref) in enumerate(branches):
        y_half = _dot(y * 0.5, w_ref[...])
        term = y_half + y_half * jnp.tanh(gt_ref[0, :, n * d:(n + 1) * d].astype(F32))
        merged = term if merged is None else merged + term
    o_ref[0] = x_ref[0] + _dot(merged.astype(BF16), wo_ref[...])


def _merge(x, pn, ya, yc, gates, lg, lb, ws, bs_full, wa, wb, wc, wo, tm):
    bsz, seq, d = x.shape
    grid = (bsz, seq // tm)
    const2 = lambda b, i: (0, 0)
    return pl.pallas_call(
        functools.partial(_merge_kernel, tm=tm),
        grid=grid,
        in_specs=[
            pl.BlockSpec((1, tm, d), lambda b, i: (b, i, 0)),
            pl.BlockSpec((1, tm, WIDTH), lambda b, i: (b, i, 0)),
            pl.BlockSpec((1, tm, WIDTH), lambda b, i: (b, i, 1)),
            pl.BlockSpec((1, tm, WIDTH), lambda b, i: (b, i, 2)),
            pl.BlockSpec((1, tm, WIDTH), lambda b, i: (b, i, 0)),
            pl.BlockSpec((1, tm, 3 * d), lambda b, i: (b, i, 0)),
            pl.BlockSpec(lg.shape, const2),
            pl.BlockSpec(lb.shape, const2),
            pl.BlockSpec(ws.shape, lambda b, i: (0, 0, 0)),
            pl.BlockSpec(bs_full.shape, const2),
            pl.BlockSpec(wa.shape, const2),
            pl.BlockSpec(wb.shape, const2),
            pl.BlockSpec(wc.shape, const2),
            pl.BlockSpec(wo.shape, const2),
        ],
        out_specs=pl.BlockSpec((1, tm, d), lambda b, i: (b, i, 0)),
        out_shape=jax.ShapeDtypeStruct((bsz, seq, d), F32),
        scratch_shapes=[pltpu.VMEM((tm, B_WIDTH), BF16)],
        compiler_params=pltpu.CompilerParams(
            dimension_semantics=("parallel", "parallel"), vmem_limit_bytes=VMEM_LIMIT_BYTES),
        name="merge",
    )(x, ya, pn, pn, yc, gates, lg, lb, ws, bs_full, wa, wb, wc, wo)


def _ffn_kernel(x_ref, g_ref, w1_ref, w2_ref, fg_ref, o_ref, h_s, acc_s, *, final_norm):
    j = pl.program_id(1)

    @pl.when(j == 0)
    def _():
        x = x_ref[...]
        ms = jnp.mean(x * x, axis=-1, keepdims=True)
        h_s[...] = (x * lax.rsqrt(ms + EPS) * g_ref[...]).astype(BF16)
        acc_s[...] = jnp.zeros_like(acc_s)

    a = jnp.maximum(_dot(h_s[...], w1_ref[...]), 0.0)
    acc_s[...] += _dot((a * a).astype(BF16), w2_ref[...])

    @pl.when(j == pl.num_programs(1) - 1)
    def _():
        y = x_ref[...] + acc_s[...]
        if final_norm:
            ms = jnp.mean(y * y, axis=-1, keepdims=True)
            y = y * lax.rsqrt(ms + EPS) * fg_ref[...]
        o_ref[...] = y


def _ffn(x2d, g, w1, w2, fg, tm, tf, final_norm):
    m, d = x2d.shape
    dff = w1.shape[1]
    grid = (m // tm, dff // tf)
    return pl.pallas_call(
        functools.partial(_ffn_kernel, final_norm=final_norm),
        grid=grid,
        in_specs=[
            pl.BlockSpec((tm, d), lambda i, j: (i, 0)),
            pl.BlockSpec((1, d), lambda i, j: (0, 0)),
            pl.BlockSpec((d, tf), lambda i, j: (0, j)),
            pl.BlockSpec((tf, d), lambda i, j: (j, 0)),
            pl.BlockSpec((1, d), lambda i, j: (0, 0)),
        ],
        out_specs=pl.BlockSpec((tm, d), lambda i, j: (i, 0)),
        out_shape=jax.ShapeDtypeStruct((m, d), F32),
        scratch_shapes=[pltpu.VMEM((tm, d), BF16), pltpu.VMEM((tm, d), F32)],
        compiler_params=pltpu.CompilerParams(
            dimension_semantics=("parallel", "arbitrary"), vmem_limit_bytes=VMEM_LIMIT_BYTES),
        name="ffn",
    )(x2d, g, w1, w2, fg)


def _tile(n, pref):
    t = min(n, pref)
    assert n % t == 0, (n, t)
    return t


def kernel(x, attn_norm_g, w_in, idx_k_norm_g, idx_k_norm_b, sgu_norm_g, sgu_norm_b, sgu_w_s, sgu_b_s,
           diff_lambda, diff_subln_g, w_branch_a, w_branch_b, w_branch_c, w_out, mlp_norm_g, w_ff1,
           w_ff2, final_norm_g):
    bsz, seq, d = x.shape
    depth = w_in.shape[0]
    offs = [0]
    for s in IN_SIZES:
        offs.append(offs[-1] + s)
    (o_aq, o_ak, o_av, o_iq, o_ik, o_iw, o_buv, o_cq, o_ck, o_cv, o_g, o_end) = offs

    tm_proj = _tile(seq, 512)
    tq = _tile(seq, 256)
    tm_merge = _tile(seq, 512)
    tm_ffn = _tile(bsz * seq, 1024)
    tf = _tile(w_ff1.shape[2], 1024)

    for l in range(depth):
        lambda_init = 0.8 - 0.6 * math.exp(-0.3 * l)
        w = w_in[l]
        cols = lambda a, b: w[:, a:b]
        wn = jnp.concatenate([cols(o_ak, o_av), cols(o_buv, o_cq), cols(o_ck, o_cv)], axis=1).astype(BF16)
        wt = jnp.concatenate([cols(o_aq, o_ak), cols(o_iq, o_ik), cols(o_cq, o_ck), cols(o_av, o_iq),
                              cols(o_cv, o_g), cols(o_iw, o_buv),
                              jnp.zeros((d, SUBLANES - IDX_HEADS), w.dtype)], axis=1).T.astype(BF16)
        wik = cols(o_ik, o_iw).astype(BF16)
        wg = cols(o_g, o_end).astype(BF16)

        pn, ikn, gates, qt, vt, iwt = _proj(
            x, attn_norm_g[l][None, :], wn, wik, wg, wt,
            idx_k_norm_g[l][None, :], idx_k_norm_b[l][None, :], tm_proj)

        ya = _dsa(qt, vt, pn, ikn, iwt, tq)
        sg_col = diff_subln_g[l][:, None]
        yc = _diff(qt, vt, pn, diff_lambda[l], sg_col, tq, lambda_init)

        bs_full = jnp.repeat(sgu_b_s[l].T, B_WIDTH // B_GROUPS, axis=1)
        x = _merge(x, pn, ya, yc, gates, sgu_norm_g[l][None, :], sgu_norm_b[l][None, :], sgu_w_s[l],
                   bs_full, w_branch_a[l].astype(BF16), w_branch_b[l].astype(BF16),
                   w_branch_c[l].astype(BF16), w_out[l].astype(BF16), tm_merge)

        x = _ffn(x.reshape(bsz * seq, d), mlp_norm_g[l][None, :], w_ff1[l].astype(BF16),
                 w_ff2[l].astype(BF16), final_norm_g[None, :], tm_ffn, tf,
                 final_norm=(l == depth - 1)).reshape(bsz, seq, d)
    return x
```

```python
import functools
import math

import jax
import jax.numpy as jnp
from jax import lax
from jax.experimental import pallas as pl
from jax.experimental.pallas import tpu as pltpu

F32 = jnp.float32
BF16 = jnp.bfloat16
I32 = jnp.int32
I16 = jnp.int16

A_HEADS = 4
A_HEAD_DIM = 64
IDX_HEADS = 4
IDX_DIM = 64
TOPK_MAX = 256
B_GROUPS = 4
B_WIDTH = 256
CHUNK = 128
C_HEADS = 4
C_HEAD_DIM = 32
C_V_DIM = 64
WIDTH = 256
EPS = 1e-6
NEG = -1e30
INT16_MIN = -(2 ** 15)
TIE_BIG = 16384.0

LANES = 128
SUBLANES = 8
PACK16_ROWS = 2 * SUBLANES

LOG2E = math.log2(math.e)
VT_HEAD_DIM = 64
VT_HEAD_ROWS = VT_HEAD_DIM + PACK16_ROWS
VT_ROWS = (WIDTH // VT_HEAD_DIM) * VT_HEAD_ROWS
VMEM_LIMIT_BYTES = 56 * 1024 * 1024

IN_SIZES = (256, 256, 256, 256, 64, 4, 512, 256, 256, 256, 3072)
N_COUNT_ACC = 4


def _nt_dot(a, b):
    return lax.dot_general(a, b, (((1,), (1,)), ((), ())), preferred_element_type=F32)


def _dot(a, b):
    return jnp.dot(a, b, preferred_element_type=F32)


CHUNKS_PER_TRIP = 4


def _for_chunks(n, body):
    def group(j, carry):
        for u in range(CHUNKS_PER_TRIP):
            body(CHUNKS_PER_TRIP * j + u)
        return carry

    lax.fori_loop(0, n // CHUNKS_PER_TRIP, group, 0)

    def single(c, carry):
        body(c)
        return carry

    lax.fori_loop(CHUNKS_PER_TRIP * (n // CHUNKS_PER_TRIP), n, single, 0)


def _proj_kernel(x_ref, g_ref, wn_ref, wik_ref, wg_ref, wt_ref, ikg_ref, ikb_ref,
                 pn_ref, ikn_ref, gt_ref, qt_ref, vt_ref, iwt_ref, *, gate_chunk):
    x = x_ref[0]
    tm = x.shape[0]
    ms = jnp.mean(x * x, axis=-1, keepdims=True)
    h = (x * lax.rsqrt(ms + EPS) * g_ref[...]).astype(BF16)
    for c0 in range(0, wn_ref.shape[1], WIDTH):
        pn_ref[0, :, c0:c0 + WIDTH] = _dot(h, wn_ref[:, c0:c0 + WIDTH]).astype(BF16)
    for c0 in range(0, wg_ref.shape[1], gate_chunk):
        gt_ref[0, :, c0:c0 + gate_chunk] = (_dot(h, wg_ref[:, c0:c0 + gate_chunk]) * 0.5).astype(BF16)
    for blk, q_scale in enumerate((A_HEAD_DIM ** -0.5 * LOG2E, None, C_HEAD_DIM ** -0.5 * LOG2E)):
        r = _nt_dot(wt_ref[blk * WIDTH:(blk + 1) * WIDTH, :], h)
        if q_scale is not None:
            r = r * q_scale
        qt_ref[0, blk * WIDTH:(blk + 1) * WIDTH, :] = r.astype(BF16)
    for blk in range(2):
        r0 = (3 + blk) * WIDTH
        r = _nt_dot(wt_ref[r0:r0 + WIDTH, :], h).astype(BF16)
        for hd in range(WIDTH // VT_HEAD_DIM):
            o0 = hd * VT_HEAD_ROWS
            vt_ref[0, blk, o0:o0 + VT_HEAD_DIM, :] = r[hd * VT_HEAD_DIM:(hd + 1) * VT_HEAD_DIM, :]
            vt_ref[0, blk, o0 + VT_HEAD_DIM:o0 + VT_HEAD_ROWS, :] = jnp.ones(
                (VT_HEAD_ROWS - VT_HEAD_DIM, tm), BF16)
    iwt_ref[0] = _nt_dot(wt_ref[5 * WIDTH:5 * WIDTH + SUBLANES, :], h)
    ik = _dot(h, wik_ref[...])
    mu = jnp.mean(ik, axis=-1, keepdims=True)
    xc = ik - mu
    var = jnp.mean(xc * xc, axis=-1, keepdims=True)
    ikn_ref[0] = (xc * lax.rsqrt(var + EPS) * ikg_ref[...] + ikb_ref[...]).astype(BF16)


def _proj(x, g, wn, wik, wg, wt, ikg, ikb, tm):
    bsz, seq, d = x.shape
    n_nat, n_gate = wn.shape[1], wg.shape[1]
    assert wt.shape[0] == 5 * WIDTH + SUBLANES
    grid = (bsz, seq // tm)
    const = lambda b, i: (0, 0)
    return pl.pallas_call(
        functools.partial(_proj_kernel, gate_chunk=min(n_gate, 768)),
        grid=grid,
        in_specs=[
            pl.BlockSpec((1, tm, d), lambda b, i: (b, i, 0)),
            pl.BlockSpec((1, d), const),
            pl.BlockSpec(wn.shape, const),
            pl.BlockSpec(wik.shape, const),
            pl.BlockSpec(wg.shape, const),
            pl.BlockSpec(wt.shape, const),
            pl.BlockSpec(ikg.shape, const),
            pl.BlockSpec(ikb.shape, const),
        ],
        out_specs=[
            pl.BlockSpec((1, tm, n_nat), lambda b, i: (b, i, 0)),
            pl.BlockSpec((1, tm, IDX_DIM), lambda b, i: (b, i, 0)),
            pl.BlockSpec((1, tm, n_gate), lambda b, i: (b, i, 0)),
            pl.BlockSpec((1, 3 * WIDTH, tm), lambda b, i: (b, 0, i)),
            pl.BlockSpec((1, 2, VT_ROWS, tm), lambda b, i: (b, 0, 0, i)),
            pl.BlockSpec((1, SUBLANES, tm), lambda b, i: (b, 0, i)),
        ],
        out_shape=[
            jax.ShapeDtypeStruct((bsz, seq, n_nat), BF16),
            jax.ShapeDtypeStruct((bsz, seq, IDX_DIM), BF16),
            jax.ShapeDtypeStruct((bsz, seq, n_gate), BF16),
            jax.ShapeDtypeStruct((bsz, 3 * WIDTH, seq), BF16),
            jax.ShapeDtypeStruct((bsz, 2, VT_ROWS, seq), BF16),
            jax.ShapeDtypeStruct((bsz, SUBLANES, seq), F32),
        ],
        compiler_params=pltpu.CompilerParams(
            dimension_semantics=("parallel", "parallel"), vmem_limit_bytes=VMEM_LIMIT_BYTES),
        name="proj",
    )(x, g, wn, wik, wg, wt, ikg, ikb)


def _dsa_kernel(qt_ref, iqt_ref, iwt_ref, ikn_ref, k_ref, vt_ref, o_ref,
                hi_s, lo_s, bias_s, qm_s, acc_s, m_s, tri_s, s0_s, s1_s, lm_s, *, tq, top_k):
    i = pl.program_id(1)
    kc = tq
    n_chunks = i + 1
    n_grp = kc // PACK16_ROWS
    q_pos = i * tq + lax.broadcasted_iota(I32, (1, tq), 1)
    one16 = jnp.ones((PACK16_ROWS, tq), I16)
    zero16 = jnp.zeros((PACK16_ROWS, tq), I16)

    def rows16(v):
        return jnp.broadcast_to(v, (PACK16_ROWS, tq)).astype(I16)

    w_all = iwt_ref[0] * (IDX_DIM ** -0.5 * IDX_HEADS ** -0.5)
    w_rows = [w_all[h:h + 1, :] for h in range(IDX_HEADS)]
    for h in range(IDX_HEADS):
        qm_s[0:IDX_DIM, h * tq:(h + 1) * tq] = iqt_ref[0, h * IDX_DIM:(h + 1) * IDX_DIM, :]

    def score_body(c, diagonal):
        off = pl.multiple_of(c * kc, kc)
        ik = ikn_ref[0, pl.ds(off, kc), :]
        d = _dot(ik, qm_s[0:IDX_DIM, :])
        sc = jnp.zeros((kc, tq), F32)
        for h in range(IDX_HEADS):
            sc = sc + jnp.maximum(d[:, h * tq:(h + 1) * tq], 0.0) * w_rows[h]
        if diagonal:
            key_pos = off + lax.broadcasted_iota(I32, (kc, tq), 0)
            sc = jnp.where(key_pos <= q_pos, sc, -jnp.inf)
        bits = pltpu.bitcast(sc, I32)
        key = bits ^ ((bits >> 31) & 0x7FFFFFFF)
        hi_s[pl.ds(off, kc), :] = (key >> 16).astype(I16)
        lo_s[pl.ds(off, kc), :] = ((key & 0xFFFF) - 0x8000).astype(I16)

    _for_chunks(i, lambda c: score_body(c, False))
    score_body(i, True)

    def count_ge(ref, cand):
        cb = rows16(cand)

        def body(c, accs):
            off = pl.multiple_of(c * kc, kc)
            slab = ref[pl.ds(off, kc), :]
            accs = list(accs)
            for r in range(n_grp):
                kk = slab[r * PACK16_ROWS:(r + 1) * PACK16_ROWS, :]
                accs[r % N_COUNT_ACC] = accs[r % N_COUNT_ACC] + jnp.where(kk >= cb, one16, zero16)
            return tuple(accs)

        def group(j, accs):
            for u in range(CHUNKS_PER_TRIP):
                accs = body(CHUNKS_PER_TRIP * j + u, accs)
            return accs

        accs = lax.fori_loop(0, n_chunks // CHUNKS_PER_TRIP, group, (zero16,) * N_COUNT_ACC)
        accs = lax.fori_loop(CHUNKS_PER_TRIP * (n_chunks // CHUNKS_PER_TRIP), n_chunks, body, accs)
        tot = accs[0]
        for a in accs[1:]:
            tot = tot + a
        return jnp.sum(tot.astype(I32), axis=0, keepdims=True)

    def descend(ref, need):
        c0 = count_ge(ref, jnp.zeros((1, tq), I32))
        ok0 = c0 >= need
        thr = jnp.where(ok0, 0, INT16_MIN).astype(I32)
        c_gt = jnp.where(ok0, 0, c0)

        def bit_body(j, carry):
            thr, c_gt = carry
            cand = thr | (jnp.int32(1) << (14 - j))
            cnt = count_ge(ref, cand)
            ok = cnt >= need
            return jnp.where(ok, cand, thr), jnp.where(ok, c_gt, cnt)

        return lax.fori_loop(0, 15, bit_body, (thr, c_gt))

    thr_hi, c_gt_hi = descend(hi_s, top_k)
    thr_hi16 = rows16(thr_hi)
    need_lo = top_k - c_gt_hi

    def restrict_body(c, carry):
        off = pl.multiple_of(c * kc, kc)
        in_bucket = hi_s[pl.ds(off, kc), :] == jnp.concatenate([thr_hi16] * n_grp, axis=0)
        lo_s[pl.ds(off, kc), :] = jnp.where(in_bucket, lo_s[pl.ds(off, kc), :], jnp.int16(INT16_MIN))
        return carry

    lax.fori_loop(0, n_chunks, restrict_body, 0)
    thr_lo, c_gt_lo = descend(lo_s, need_lo)
    thr_lo16 = rows16(thr_lo)
    need_tie = (need_lo - c_gt_lo).astype(F32)

    r_i = lax.broadcasted_iota(I32, (kc, kc), 0)
    c_i = lax.broadcasted_iota(I32, (kc, kc), 1)
    tri_s[0:kc, 0:kc] = jnp.where(c_i <= r_i, 1.0, 0.0).astype(BF16)
    tri_s[0:kc, kc:2 * kc] = jnp.where(c_i == r_i, TIE_BIG, 0.0).astype(BF16)
    tri_s[kc:kc + PACK16_ROWS, 0:kc] = jnp.ones((PACK16_ROWS, kc), BF16)
    tri_s[kc:kc + PACK16_ROWS, kc:2 * kc] = jnp.zeros((PACK16_ROWS, kc), BF16)
    thr_hi_kc = jnp.concatenate([thr_hi16] * n_grp, axis=0)
    thr_lo_kc = jnp.concatenate([thr_lo16] * n_grp, axis=0)
    one_b, zero_b, neg_b = (jnp.full((kc, tq), v, BF16) for v in (1.0, 0.0, -1.0))

    def bias_body(c, seen, diagonal):
        off = pl.multiple_of(c * kc, kc)
        hh = hi_s[pl.ds(off, kc), :]
        ll = lo_s[pl.ds(off, kc), :]
        in_bucket = hh == thr_hi_kc
        lo_eq = ll == thr_lo_kc
        equal = jnp.where(in_bucket, jnp.where(lo_eq, one_b, zero_b), zero_b)
        side = jnp.where(hh > thr_hi_kc, neg_b,
                         jnp.where(in_bucket, jnp.where(ll > thr_lo_kc, neg_b, jnp.where(lo_eq, zero_b, one_b)),
                                   one_b))
        rank = _dot(tri_s[...], jnp.concatenate([equal, side], axis=0))
        b = jnp.where(rank[0:kc, :] <= need_tie - seen, 0.0, NEG)
        if diagonal:
            key_pos = off + lax.broadcasted_iota(I32, (kc, tq), 0)
            b = jnp.where(key_pos <= q_pos, b, NEG)
        bias_s[pl.ds(off, kc), :] = b
        return seen + rank[kc:kc + 1, :]

    def bias_group(j, seen):
        for u in range(CHUNKS_PER_TRIP):
            seen = bias_body(CHUNKS_PER_TRIP * j + u, seen, False)
        return seen

    seen = lax.fori_loop(0, i // CHUNKS_PER_TRIP, bias_group, jnp.zeros((1, tq), F32))
    seen = lax.fori_loop(CHUNKS_PER_TRIP * (i // CHUNKS_PER_TRIP), i,
                         lambda c, seen: bias_body(c, seen, False), seen)
    bias_body(i, seen, True)

    feat = lax.broadcasted_iota(I32, (WIDTH, tq), 0)
    qt = qt_ref[0]
    for h in range(A_HEADS):
        qm_s[:, h * tq:(h + 1) * tq] = jnp.where(feat // A_HEAD_DIM == h, qt, jnp.zeros_like(qt))
    acc_s[...] = jnp.zeros_like(acc_s)
    m_s[0:1, :] = jnp.full((1, A_HEADS * tq), NEG, F32)

    span = 2 * kc

    def logits_block(off):
        bias = bias_s[pl.ds(off, kc), :]
        return _dot(k_ref[0, pl.ds(off, kc), :], qm_s[...]) + jnp.concatenate([bias] * A_HEADS, axis=1)

    def pv_block(off, pb, h):
        rows = slice(h * VT_HEAD_ROWS, (h + 1) * VT_HEAD_ROWS)
        return _dot(vt_ref[0, 0, rows, pl.ds(off, kc)], pb[:, h * tq:(h + 1) * tq])

    def rescale_and_add(alpha, pv):
        for h in range(A_HEADS):
            rows = slice(h * VT_HEAD_ROWS, (h + 1) * VT_HEAD_ROWS)
            acc_s[rows, :] = acc_s[rows, :] * alpha[:, h * tq:(h + 1) * tq] + pv[h]

    def attend(off, n_blk):
        s = [logits_block(off + r * kc) for r in range(n_blk)]
        m_old = m_s[0:1, :]
        m_new = m_old
        for r in range(n_blk):
            m_new = jnp.maximum(m_new, jnp.max(s[r], axis=0, keepdims=True))
        m_s[0:1, :] = m_new
        pv = [None] * A_HEADS
        for r in range(n_blk):
            pb = jnp.exp2(s[r] - m_new).astype(BF16)
            for h in range(A_HEADS):
                d = pv_block(off + r * kc, pb, h)
                pv[h] = d if pv[h] is None else pv[h] + d
        rescale_and_add(jnp.exp2(m_old - m_new), pv)

    def logits_span(off, buf):
        s_ref = (s0_s, s1_s)[buf]
        lm = None
        for r in range(span // kc):
            s = logits_block(off + r * kc)
            s_ref[r * kc:(r + 1) * kc, :] = s
            lm_r = jnp.max(s, axis=0, keepdims=True)
            lm = lm_r if lm is None else jnp.maximum(lm, lm_r)
        lm_s[buf:buf + 1, :] = lm

    def step(off_next, off_cur, buf_next):
        buf_cur = 1 - buf_next
        s_next, s_cur = (s0_s, s1_s)[buf_next], (s0_s, s1_s)[buf_cur]
        m_old = m_s[0:1, :]
        m_new = jnp.maximum(m_old, lm_s[buf_cur:buf_cur + 1, :])
        m_s[0:1, :] = m_new
        lm, pv = None, [None] * A_HEADS
        for r in range(span // kc):
            blk = slice(r * kc, (r + 1) * kc)
            s = logits_block(off_next + r * kc)
            s_next[blk, :] = s
            lm_r = jnp.max(s, axis=0, keepdims=True)
            lm = lm_r if lm is None else jnp.maximum(lm, lm_r)
            pb = jnp.exp2(s_cur[blk, :] - m_new).astype(BF16)
            for h in range(A_HEADS):
                d = pv_block(off_cur + r * kc, pb, h)
                pv[h] = d if pv[h] is None else pv[h] + d
        lm_s[buf_next:buf_next + 1, :] = lm
        rescale_and_add(jnp.exp2(m_old - m_new), pv)

    n_full = n_chunks // 2

    @pl.when(n_chunks % 2 == 1)
    def _():
        attend(pl.multiple_of(i * kc, kc), 1)

    def drain(off_cur, buf_cur):
        s_cur = (s0_s, s1_s)[buf_cur]
        m_old = m_s[0:1, :]
        m_new = jnp.maximum(m_old, lm_s[buf_cur:buf_cur + 1, :])
        m_s[0:1, :] = m_new
        pv = [None] * A_HEADS
        for r in range(span // kc):
            pb = jnp.exp2(s_cur[r * kc:(r + 1) * kc, :] - m_new).astype(BF16)
            for h in range(A_HEADS):
                d = pv_block(off_cur + r * kc, pb, h)
                pv[h] = d if pv[h] is None else pv[h] + d
        rescale_and_add(jnp.exp2(m_old - m_new), pv)

    @pl.when(n_full > 0)
    def _():
        logits_span(0, 0)
        n_steps = n_full - 1

        def body(jj, carry):
            off0 = pl.multiple_of(2 * jj * span, span)
            step(off0 + span, off0, 1)
            step(off0 + 2 * span, off0 + span, 0)
            return carry

        lax.fori_loop(0, n_steps // 2, body, 0)

        @pl.when(n_steps % 2 == 1)
        def _():
            off_cur = pl.multiple_of((n_steps - 1) * span, span)
            step(off_cur + span, off_cur, 1)

        off_last = pl.multiple_of((n_full - 1) * span, span)
        for parity in range(2):
            @pl.when((n_full - 1) % 2 == parity)
            def _():
                drain(off_last, parity)

    out = []
    for h in range(A_HEADS):
        r0 = h * VT_HEAD_ROWS
        out.append(acc_s[r0:r0 + VT_HEAD_DIM, :] * (1.0 / acc_s[r0 + VT_HEAD_DIM:r0 + VT_HEAD_DIM + 1, :]))
    o_ref[0] = jnp.concatenate(out, axis=0).T.astype(BF16)


def _dsa(qt, vt, pn, ikn, iwt, tq):
    bsz, seq, _ = pn.shape
    top_k = min(TOPK_MAX, seq // 4)
    grid = (bsz, seq // tq)
    return pl.pallas_call(
        functools.partial(_dsa_kernel, tq=tq, top_k=top_k),
        grid=grid,
        in_specs=[
            pl.BlockSpec((1, WIDTH, tq), lambda b, i: (b, 0, i)),
            pl.BlockSpec((1, WIDTH, tq), lambda b, i: (b, 1, i)),
            pl.BlockSpec((1, SUBLANES, tq), lambda b, i: (b, 0, i)),
            pl.BlockSpec((1, seq, IDX_DIM), lambda b, i: (b, 0, 0)),
            pl.BlockSpec((1, seq, WIDTH), lambda b, i: (b, 0, 0)),
            pl.BlockSpec((1, 1, VT_ROWS, seq), lambda b, i: (b, 0, 0, 0)),
        ],
        out_specs=pl.BlockSpec((1, tq, WIDTH), lambda b, i: (b, i, 0)),
        out_shape=jax.ShapeDtypeStruct((bsz, seq, WIDTH), BF16),
        scratch_shapes=[pltpu.VMEM((seq, tq), I16),
                        pltpu.VMEM((seq, tq), I16),
                        pltpu.VMEM((seq, tq), F32),
                        pltpu.VMEM((WIDTH, A_HEADS * tq), BF16),
                        pltpu.VMEM((VT_ROWS, tq), F32),
                        pltpu.VMEM((SUBLANES, A_HEADS * tq), F32),
                        pltpu.VMEM((tq + PACK16_ROWS, 2 * tq), BF16),
                        pltpu.VMEM((2 * tq, A_HEADS * tq), F32),
                        pltpu.VMEM((2 * tq, A_HEADS * tq), F32),
                        pltpu.VMEM((SUBLANES, A_HEADS * tq), F32)],
        compiler_params=pltpu.CompilerParams(
            dimension_semantics=("parallel", "arbitrary"), vmem_limit_bytes=VMEM_LIMIT_BYTES),
        name="dsa",
    )(qt, qt, iwt, ikn, pn, vt)


def _diff_kernel(qt_ref, k_ref, vt_ref, lam_ref, sg_ref, o_ref, qm_s, acc_s, m_s, s0_s, s1_s, lm_s,
                 *, tq, seq, lambda_init):
    i = pl.program_id(1)
    span = 2 * tq
    n_pairs = 2 * C_HEADS
    grp = n_pairs // 2
    feat = lax.broadcasted_iota(I32, (WIDTH, tq), 0)
    qt = qt_ref[0]
    for p_i in range(n_pairs):
        qm_s[p_i // grp, :, (p_i % grp) * tq:(p_i % grp + 1) * tq] = jnp.where(
            feat // C_HEAD_DIM == p_i, qt, jnp.zeros_like(qt))
    acc_s[...] = jnp.zeros_like(acc_s)
    m_s[0:1, :] = jnp.full((1, n_pairs * tq), NEG, F32)
    diag_ok = lax.broadcasted_iota(I32, (tq, tq), 0) <= lax.broadcasted_iota(I32, (tq, tq), 1)
    diag_ok = jnp.concatenate([diag_ok] * grp, axis=1)

    def logits(off, width, buf, mask_tail):
        s_ref = (s0_s, s1_s)[buf]
        k_c = k_ref[0, pl.ds(off, width), :]
        for g in range(2):
            s = _dot(k_c, qm_s[g])
            if mask_tail:
                head = s[0:width - tq, :]
                tail = jnp.where(diag_ok, s[width - tq:width, :], NEG)
                s = jnp.concatenate([head, tail], axis=0) if width > tq else tail
            s_ref[g, 0:width, :] = s
            lm_s[buf:buf + 1, g * grp * tq:(g + 1) * grp * tq] = jnp.max(s, axis=0, keepdims=True)

    def accumulate(off, width, buf):
        s_ref = (s0_s, s1_s)[buf]
        for g in range(2):
            st = slice(g * grp * tq, (g + 1) * grp * tq)
            m_old = m_s[0:1, st]
            m_new = jnp.maximum(m_old, lm_s[buf:buf + 1, st])
            alpha = jnp.exp2(m_old - m_new)
            pb = jnp.exp2(s_ref[g, 0:width, :] - m_new).astype(BF16)
            m_s[0:1, st] = m_new
            for j in range(grp):
                h, comp = (g * grp + j) // 2, (g * grp + j) % 2
                rows = slice(h * VT_HEAD_ROWS, (h + 1) * VT_HEAD_ROWS)
                cols = slice(j * tq, (j + 1) * tq)
                pv = _dot(vt_ref[0, 0, rows, pl.ds(off, width)], pb[:, cols])
                acc_s[comp, rows, :] = acc_s[comp, rows, :] * alpha[:, cols] + pv

    @pl.when(i % 2 == 0)
    def _():
        off = pl.multiple_of(i * tq, tq)
        logits(off, tq, 0, True)
        accumulate(off, tq, 0)

    @pl.when(i % 2 == 1)
    def _():
        off = pl.multiple_of((i - 1) * tq, span)
        logits(off, span, 0, True)
        accumulate(off, span, 0)

    n_full = i // 2

    def step(off_next, off_cur, buf_next):
        buf_cur = 1 - buf_next
        s_next, s_cur = (s0_s, s1_s)[buf_next], (s0_s, s1_s)[buf_cur]
        m_new, alpha, lm_run = [], [], []
        for g in range(2):
            st = slice(g * grp * tq, (g + 1) * grp * tq)
            m_old = m_s[0:1, st]
            m_new.append(jnp.maximum(m_old, lm_s[buf_cur:buf_cur + 1, st]))
            alpha.append(jnp.exp2(m_old - m_new[g]))
            m_s[0:1, st] = m_new[g]
            lm_run.append(jnp.full((1, grp * tq), NEG, F32))
        pv = [None] * n_pairs
        for r in range(span // tq):
            blk = slice(r * tq, (r + 1) * tq)
            k_blk = k_ref[0, pl.ds(off_next + r * tq, tq), :]
            for g in range(2):
                s = _dot(k_blk, qm_s[g])
                s_next[g, blk, :] = s
                lm_run[g] = jnp.maximum(lm_run[g], jnp.max(s, axis=0, keepdims=True))
                pb = jnp.exp2(s_cur[g, blk, :] - m_new[g]).astype(BF16)
                for j in range(grp):
                    p_i = g * grp + j
                    rows = slice((p_i // 2) * VT_HEAD_ROWS, (p_i // 2 + 1) * VT_HEAD_ROWS)
                    d = _dot(vt_ref[0, 0, rows, pl.ds(off_cur + r * tq, tq)], pb[:, j * tq:(j + 1) * tq])
                    pv[p_i] = d if pv[p_i] is None else pv[p_i] + d
        for g in range(2):
            lm_s[buf_next:buf_next + 1, g * grp * tq:(g + 1) * grp * tq] = lm_run[g]
            for j in range(grp):
                p_i = g * grp + j
                rows = slice((p_i // 2) * VT_HEAD_ROWS, (p_i // 2 + 1) * VT_HEAD_ROWS)
                acc_s[p_i % 2, rows, :] = acc_s[p_i % 2, rows, :] * alpha[g][:, j * tq:(j + 1) * tq] + pv[p_i]

    @pl.when(n_full > 0)
    def _():
        logits(0, span, 0, False)
        n_steps = n_full - 1

        def body(jj, carry):
            off0 = pl.multiple_of(2 * jj * span, span)
            step(off0 + span, off0, 1)
            step(off0 + 2 * span, off0 + span, 0)
            return carry

        lax.fori_loop(0, n_steps // 2, body, 0)

        @pl.when(n_steps % 2 == 1)
        def _():
            off_cur = pl.multiple_of((n_steps - 1) * span, span)
            step(off_cur + span, off_cur, 1)

        off_last = pl.multiple_of((n_full - 1) * span, span)
        for parity in range(2):
            @pl.when((n_full - 1) % 2 == parity)
            def _():
                accumulate(off_last, span, parity)

    lq = lam_ref[...]
    lam = (jnp.exp(jnp.sum(lq[0:1] * lq[1:2], axis=1, keepdims=True))
           - jnp.exp(jnp.sum(lq[2:3] * lq[3:4], axis=1, keepdims=True)) + lambda_init)
    outs = []
    for h in range(C_HEADS):
        r0 = h * VT_HEAD_ROWS
        l_row = slice(r0 + VT_HEAD_DIM, r0 + VT_HEAD_DIM + 1)
        a1 = acc_s[0, r0:r0 + VT_HEAD_DIM, :] * (1.0 / acc_s[0, l_row, :])
        a2 = acc_s[1, r0:r0 + VT_HEAD_DIM, :] * (1.0 / acc_s[1, l_row, :])
        out = a1 - lam * a2
        rs = lax.rsqrt(jnp.mean(out * out, axis=0, keepdims=True) + EPS)
        outs.append(out * rs * sg_ref[...] * (1.0 - lambda_init))
    o_ref[0] = jnp.concatenate(outs, axis=0).T.astype(BF16)


def _diff(qt, vt, pn, lam_p, sg_col, tq, lambda_init):
    bsz, seq, _ = pn.shape
    grid = (bsz, seq // tq)
    n_pairs = 2 * C_HEADS
    return pl.pallas_call(
        functools.partial(_diff_kernel, tq=tq, seq=seq, lambda_init=lambda_init),
        grid=grid,
        in_specs=[
            pl.BlockSpec((1, WIDTH, tq), lambda b, i: (b, 2, i)),
            pl.BlockSpec((1, seq, WIDTH), lambda b, i: (b, 0, 3)),
            pl.BlockSpec((1, 1, VT_ROWS, seq), lambda b, i: (b, 1, 0, 0)),
            pl.BlockSpec(lam_p.shape, lambda b, i: (0, 0)),
            pl.BlockSpec(sg_col.shape, lambda b, i: (0, 0)),
        ],
        out_specs=pl.BlockSpec((1, tq, WIDTH), lambda b, i: (b, i, 0)),
        out_shape=jax.ShapeDtypeStruct((bsz, seq, WIDTH), BF16),
        scratch_shapes=[pltpu.VMEM((2, WIDTH, (n_pairs // 2) * tq), BF16),
                        pltpu.VMEM((2, VT_ROWS, tq), F32),
                        pltpu.VMEM((SUBLANES, n_pairs * tq), F32),
                        pltpu.VMEM((2, 2 * tq, (n_pairs // 2) * tq), F32),
                        pltpu.VMEM((2, 2 * tq, (n_pairs // 2) * tq), F32),
                        pltpu.VMEM((SUBLANES, n_pairs * tq), F32)],
        compiler_params=pltpu.CompilerParams(
            dimension_semantics=("parallel", "arbitrary"), vmem_limit_bytes=VMEM_LIMIT_BYTES),
        name="diff",
    )(qt, pn, vt, lam_p, sg_col)


def _gelu_tanh(x):
    return x * (0.5 * (1.0 + jnp.tanh(math.sqrt(2.0 / math.pi) * (x + 0.044715 * (x * x * x)))))


def _merge_kernel(x_ref, ya_ref, bu_ref, bv_ref, yc_ref, gt_ref, lg_ref, lb_ref, ws_ref, bs_ref,
                  wa_ref, wb_ref, wc_ref, wo_ref, o_ref, yb_s, *, tm):
    d = x_ref.shape[-1]
    lane_w = lax.broadcasted_iota(I32, (CHUNK, B_WIDTH), 1)
    group_of_lane = lane_w // (B_WIDTH // B_GROUPS)
    r_i = lax.broadcasted_iota(I32, (CHUNK, CHUNK), 0)
    c_i = lax.broadcasted_iota(I32, (CHUNK, CHUNK), 1)
    w_tril = [jnp.where(c_i <= r_i, ws_ref[g], 0.0).astype(BF16) for g in range(B_GROUPS)]
    for c in range(tm // CHUNK):
        rows = slice(c * CHUNK, (c + 1) * CHUNK)
        u = _gelu_tanh(bu_ref[0, rows, :].astype(F32))
        v = _gelu_tanh(bv_ref[0, rows, :].astype(F32))
        mu = jnp.mean(v, axis=-1, keepdims=True)
        vc = v - mu
        var = jnp.mean(vc * vc, axis=-1, keepdims=True)
        vn = (vc * lax.rsqrt(var + EPS) * lg_ref[...] + lb_ref[...]).astype(BF16)
        s = bs_ref[...]
        for g in range(B_GROUPS):
            s = s + _dot(w_tril[g], jnp.where(group_of_lane == g, vn, jnp.zeros_like(vn)))
        yb_s[rows, :] = (u * s).astype(BF16)

    merged = None
    branches = ((ya_ref[0], wa_ref), (yb_s[...], wb_ref), (yc_ref[0], wc_ref))
    for n, (y, w_ref) in enumerate(branches):
        y_half = _dot(y * 0.5, w_ref[...])
        term = y_half + y_half * jnp.tanh(gt_ref[0, :, n * d:(n + 1) * d].astype(F32))
        merged = term if merged is None else merged + term
    o_ref[0] = x_ref[0] + _dot(merged.astype(BF16), wo_ref[...])


def _merge(x, pn, ya, yc, gates, lg, lb, ws, bs_full, wa, wb, wc, wo, tm):
    bsz, seq, d = x.shape
    grid = (bsz, seq // tm)
    const2 = lambda b, i: (0, 0)
    return pl.pallas_call(
        functools.partial(_merge_kernel, tm=tm),
        grid=grid,
        in_specs=[
            pl.BlockSpec((1, tm, d), lambda b, i: (b, i, 0)),
            pl.BlockSpec((1, tm, WIDTH), lambda b, i: (b, i, 0)),
            pl.BlockSpec((1, tm, WIDTH), lambda b, i: (b, i, 1)),
            pl.BlockSpec((1, tm, WIDTH), lambda b, i: (b, i, 2)),
            pl.BlockSpec((1, tm, WIDTH), lambda b, i: (b, i, 0)),
            pl.BlockSpec((1, tm, 3 * d), lambda b, i: (b, i, 0)),
            pl.BlockSpec(lg.shape, const2),
            pl.BlockSpec(lb.shape, const2),
            pl.BlockSpec(ws.shape, lambda b, i: (0, 0, 0)),
            pl.BlockSpec(bs_full.shape, const2),
            pl.BlockSpec(wa.shape, const2),
            pl.BlockSpec(wb.shape, const2),
            pl.BlockSpec(wc.shape, const2),
            pl.BlockSpec(wo.shape, const2),
        ],
        out_specs=pl.BlockSpec((1, tm, d), lambda b, i: (b, i, 0)),
        out_shape=jax.ShapeDtypeStruct((bsz, seq, d), F32),
        scratch_shapes=[pltpu.VMEM((tm, B_WIDTH), BF16)],
        compiler_params=pltpu.CompilerParams(
            dimension_semantics=("parallel", "parallel"), vmem_limit_bytes=VMEM_LIMIT_BYTES),
        name="merge",
    )(x, ya, pn, pn, yc, gates, lg, lb, ws, bs_full, wa, wb, wc, wo)


def _ffn_kernel(x_ref, g_ref, w1_ref, w2_ref, fg_ref, o_ref, h_s, acc_s, *, final_norm):
    j = pl.program_id(1)

    @pl.when(j == 0)
    def _():
        x = x_ref[...]
        ms = jnp.mean(x * x, axis=-1, keepdims=True)
        h_s[...] = (x * lax.rsqrt(ms + EPS) * g_ref[...]).astype(BF16)
        acc_s[...] = jnp.zeros_like(acc_s)

    a = jnp.maximum(_dot(h_s[...], w1_ref[...]), 0.0)
    acc_s[...] += _dot((a * a).astype(BF16), w2_ref[...])

    @pl.when(j == pl.num_programs(1) - 1)
    def _():
        y = x_ref[...] + acc_s[...]
        if final_norm:
            ms = jnp.mean(y * y, axis=-1, keepdims=True)
            y = y * lax.rsqrt(ms + EPS) * fg_ref[...]
        o_ref[...] = y


def _ffn(x2d, g, w1, w2, fg, tm, tf, final_norm):
    m, d = x2d.shape
    dff = w1.shape[1]
    grid = (m // tm, dff // tf)
    return pl.pallas_call(
        functools.partial(_ffn_kernel, final_norm=final_norm),
        grid=grid,
        in_specs=[
            pl.BlockSpec((tm, d), lambda i, j: (i, 0)),
            pl.BlockSpec((1, d), lambda i, j: (0, 0)),
            pl.BlockSpec((d, tf), lambda i, j: (0, j)),
            pl.BlockSpec((tf, d), lambda i, j: (j, 0)),
            pl.BlockSpec((1, d), lambda i, j: (0, 0)),
        ],
        out_specs=pl.BlockSpec((tm, d), lambda i, j: (i, 0)),
        out_shape=jax.ShapeDtypeStruct((m, d), F32),
        scratch_shapes=[pltpu.VMEM((tm, d), BF16), pltpu.VMEM((tm, d), F32)],
        compiler_params=pltpu.CompilerParams(
            dimension_semantics=("parallel", "arbitrary"), vmem_limit_bytes=VMEM_LIMIT_BYTES),
        name="ffn",
    )(x2d, g, w1, w2, fg)


def _tile(n, pref):
    t = min(n, pref)
    assert n % t == 0, (n, t)
    return t


def kernel(x, attn_norm_g, w_in, idx_k_norm_g, idx_k_norm_b, sgu_norm_g, sgu_norm_b, sgu_w_s, sgu_b_s,
           diff_lambda, diff_subln_g, w_branch_a, w_branch_b, w_branch_c, w_out, mlp_norm_g, w_ff1,
           w_ff2, final_norm_g):
    bsz, seq, d = x.shape
    depth = w_in.shape[0]
    offs = [0]
    for s in IN_SIZES:
        offs.append(offs[-1] + s)
    (o_aq, o_ak, o_av, o_iq, o_ik, o_iw, o_buv, o_cq, o_ck, o_cv, o_g, o_end) = offs

    tm_proj = _tile(seq, 512)
    tq = _tile(seq, 256)
    tm_merge = _tile(seq, 512)
    tm_ffn = _tile(bsz * seq, 1024)
    tf = _tile(w_ff1.shape[2], 1024)

    for l in range(depth):
        lambda_init = 0.8 - 0.6 * math.exp(-0.3 * l)
        w = w_in[l]
        cols = lambda a, b: w[:, a:b]
        wn = jnp.concatenate([cols(o_ak, o_av), cols(o_buv, o_cq), cols(o_ck, o_cv)], axis=1).astype(BF16)
        wt = jnp.concatenate([cols(o_aq, o_ak), cols(o_iq, o_ik), cols(o_cq, o_ck), cols(o_av, o_iq),
                              cols(o_cv, o_g), cols(o_iw, o_buv),
                              jnp.zeros((d, SUBLANES - IDX_HEADS), w.dtype)], axis=1).T.astype(BF16)
        wik = cols(o_ik, o_iw).astype(BF16)
        wg = cols(o_g, o_end).astype(BF16)

        pn, ikn, gates, qt, vt, iwt = _proj(
            x, attn_norm_g[l][None, :], wn, wik, wg, wt,
            idx_k_norm_g[l][None, :], idx_k_norm_b[l][None, :], tm_proj)

        ya = _dsa(qt, vt, pn, ikn, iwt, tq)
        sg_col = diff_subln_g[l][:, None]
        yc = _diff(qt, vt, pn, diff_lambda[l], sg_col, tq, lambda_init)

        bs_full = jnp.repeat(sgu_b_s[l].T, B_WIDTH // B_GROUPS, axis=1)
        x = _merge(x, pn, ya, yc, gates, sgu_norm_g[l][None, :], sgu_norm_b[l][None, :], sgu_w_s[l],
                   bs_full, w_branch_a[l].astype(BF16), w_branch_b[l].astype(BF16),
                   w_branch_c[l].astype(BF16), w_out[l].astype(BF16), tm_merge)

        x = _ffn(x.reshape(bsz * seq, d), mlp_norm_g[l][None, :], w_ff1[l].astype(BF16),
                 w_ff2[l].astype(BF16), final_norm_g[None, :], tm_ffn, tf,
                 final_norm=(l == depth - 1)).reshape(bsz, seq, d)
    return x
```

```python
import functools
import math

import jax
import jax.numpy as jnp
from jax import lax
from jax.experimental import pallas as pl
from jax.experimental.pallas import tpu as pltpu

F32 = jnp.float32
BF16 = jnp.bfloat16
I32 = jnp.int32
I16 = jnp.int16

A_HEADS = 4
A_HEAD_DIM = 64
IDX_HEADS = 4
IDX_DIM = 64
TOPK_MAX = 256
B_GROUPS = 4
B_WIDTH = 256
CHUNK = 128
C_HEADS = 4
C_HEAD_DIM = 32
C_V_DIM = 64
WIDTH = 256
EPS = 1e-6
NEG = -1e30
INT16_MIN = -(2 ** 15)
TIE_BIG = 16384.0
DIFF_BLOCK_ROWS = 256

LANES = 128
SUBLANES = 8
PACK16_ROWS = 2 * SUBLANES

LOG2E = math.log2(math.e)
VT_HEAD_DIM = 64
VT_HEAD_ROWS = VT_HEAD_DIM + PACK16_ROWS
VT_ROWS = (WIDTH // VT_HEAD_DIM) * VT_HEAD_ROWS
VMEM_LIMIT_BYTES = 56 * 1024 * 1024

IN_SIZES = (256, 256, 256, 256, 64, 4, 512, 256, 256, 256, 3072)
N_COUNT_ACC = 4


def _nt_dot(a, b):
    return lax.dot_general(a, b, (((1,), (1,)), ((), ())), preferred_element_type=F32)


def _dot(a, b):
    return jnp.dot(a, b, preferred_element_type=F32)


CHUNKS_PER_TRIP = 4


def _for_chunks(n, body):
    def group(j, carry):
        for u in range(CHUNKS_PER_TRIP):
            body(CHUNKS_PER_TRIP * j + u)
        return carry

    lax.fori_loop(0, n // CHUNKS_PER_TRIP, group, 0)

    def single(c, carry):
        body(c)
        return carry

    lax.fori_loop(CHUNKS_PER_TRIP * (n // CHUNKS_PER_TRIP), n, single, 0)


def _proj_kernel(x_ref, g_ref, wn_ref, wik_ref, wg_ref, wt_ref, ikg_ref, ikb_ref,
                 pn_ref, ikn_ref, gt_ref, qt_ref, vt_ref, iwt_ref, *, gate_chunk):
    x = x_ref[0]
    tm = x.shape[0]
    ms = jnp.mean(x * x, axis=-1, keepdims=True)
    h = (x * lax.rsqrt(ms + EPS) * g_ref[...]).astype(BF16)
    for c0 in range(0, wn_ref.shape[1], WIDTH):
        pn_ref[0, :, c0:c0 + WIDTH] = _dot(h, wn_ref[:, c0:c0 + WIDTH]).astype(BF16)
    for c0 in range(0, wg_ref.shape[1], gate_chunk):
        gt_ref[0, :, c0:c0 + gate_chunk] = (_dot(h, wg_ref[:, c0:c0 + gate_chunk]) * 0.5).astype(BF16)
    for blk, q_scale in enumerate((A_HEAD_DIM ** -0.5 * LOG2E, None, C_HEAD_DIM ** -0.5 * LOG2E)):
        r = _nt_dot(wt_ref[blk * WIDTH:(blk + 1) * WIDTH, :], h)
        if q_scale is not None:
            r = r * q_scale
        qt_ref[0, blk * WIDTH:(blk + 1) * WIDTH, :] = r.astype(BF16)
    for blk in range(2):
        r0 = (3 + blk) * WIDTH
        r = _nt_dot(wt_ref[r0:r0 + WIDTH, :], h).astype(BF16)
        for hd in range(WIDTH // VT_HEAD_DIM):
            o0 = hd * VT_HEAD_ROWS
            vt_ref[0, blk, o0:o0 + VT_HEAD_DIM, :] = r[hd * VT_HEAD_DIM:(hd + 1) * VT_HEAD_DIM, :]
            vt_ref[0, blk, o0 + VT_HEAD_DIM:o0 + VT_HEAD_ROWS, :] = jnp.ones(
                (VT_HEAD_ROWS - VT_HEAD_DIM, tm), BF16)
    iwt_ref[0] = _nt_dot(wt_ref[5 * WIDTH:5 * WIDTH + SUBLANES, :], h)
    ik = _dot(h, wik_ref[...])
    mu = jnp.mean(ik, axis=-1, keepdims=True)
    xc = ik - mu
    var = jnp.mean(xc * xc, axis=-1, keepdims=True)
    ikn_ref[0] = (xc * lax.rsqrt(var + EPS) * ikg_ref[...] + ikb_ref[...]).astype(BF16)


def _proj(x, g, wn, wik, wg, wt, ikg, ikb, tm):
    bsz, seq, d = x.shape
    n_nat, n_gate = wn.shape[1], wg.shape[1]
    assert wt.shape[0] == 5 * WIDTH + SUBLANES
    grid = (bsz, seq // tm)
    const = lambda b, i: (0, 0)
    return pl.pallas_call(
        functools.partial(_proj_kernel, gate_chunk=min(n_gate, 768)),
        grid=grid,
        in_specs=[
            pl.BlockSpec((1, tm, d), lambda b, i: (b, i, 0)),
            pl.BlockSpec((1, d), const),
            pl.BlockSpec(wn.shape, const),
            pl.BlockSpec(wik.shape, const),
            pl.BlockSpec(wg.shape, const),
            pl.BlockSpec(wt.shape, const),
            pl.BlockSpec(ikg.shape, const),
            pl.BlockSpec(ikb.shape, const),
        ],
        out_specs=[
            pl.BlockSpec((1, tm, n_nat), lambda b, i: (b, i, 0)),
            pl.BlockSpec((1, tm, IDX_DIM), lambda b, i: (b, i, 0)),
            pl.BlockSpec((1, tm, n_gate), lambda b, i: (b, i, 0)),
            pl.BlockSpec((1, 3 * WIDTH, tm), lambda b, i: (b, 0, i)),
            pl.BlockSpec((1, 2, VT_ROWS, tm), lambda b, i: (b, 0, 0, i)),
            pl.BlockSpec((1, SUBLANES, tm), lambda b, i: (b, 0, i)),
        ],
        out_shape=[
            jax.ShapeDtypeStruct((bsz, seq, n_nat), BF16),
            jax.ShapeDtypeStruct((bsz, seq, IDX_DIM), BF16),
            jax.ShapeDtypeStruct((bsz, seq, n_gate), BF16),
            jax.ShapeDtypeStruct((bsz, 3 * WIDTH, seq), BF16),
            jax.ShapeDtypeStruct((bsz, 2, VT_ROWS, seq), BF16),
            jax.ShapeDtypeStruct((bsz, SUBLANES, seq), F32),
        ],
        compiler_params=pltpu.CompilerParams(
            dimension_semantics=("parallel", "parallel"), vmem_limit_bytes=VMEM_LIMIT_BYTES),
        name="proj",
    )(x, g, wn, wik, wg, wt, ikg, ikb)


def _dsa_kernel(qt_ref, iqt_ref, iwt_ref, ikn_ref, k_ref, vt_ref, o_ref,
                hi_s, lo_s, bias_s, qm_s, acc_s, m_s, tri_s, s0_s, s1_s, lm_s, *, tq, top_k):
    i = pl.program_id(1)
    kc = tq
    n_chunks = i + 1
    n_grp = kc // PACK16_ROWS
    q_pos = i * tq + lax.broadcasted_iota(I32, (1, tq), 1)
    one16 = jnp.ones((PACK16_ROWS, tq), I16)
    zero16 = jnp.zeros((PACK16_ROWS, tq), I16)

    def rows16(v):
        return jnp.broadcast_to(v, (PACK16_ROWS, tq)).astype(I16)

    w_all = iwt_ref[0] * (IDX_DIM ** -0.5 * IDX_HEADS ** -0.5)
    w_rows = [w_all[h:h + 1, :] for h in range(IDX_HEADS)]
    for h in range(IDX_HEADS):
        qm_s[0:IDX_DIM, h * tq:(h + 1) * tq] = iqt_ref[0, h * IDX_DIM:(h + 1) * IDX_DIM, :]

    def score_body(c, diagonal):
        off = pl.multiple_of(c * kc, kc)
        ik = ikn_ref[0, pl.ds(off, kc), :]
        d = _dot(ik, qm_s[0:IDX_DIM, :])
        sc = jnp.zeros((kc, tq), F32)
        for h in range(IDX_HEADS):
            sc = sc + jnp.maximum(d[:, h * tq:(h + 1) * tq], 0.0) * w_rows[h]
        if diagonal:
            key_pos = off + lax.broadcasted_iota(I32, (kc, tq), 0)
            sc = jnp.where(key_pos <= q_pos, sc, -jnp.inf)
        bits = pltpu.bitcast(sc, I32)
        key = bits ^ ((bits >> 31) & 0x7FFFFFFF)
        hi_s[pl.ds(off, kc), :] = (key >> 16).astype(I16)
        lo_s[pl.ds(off, kc), :] = ((key & 0xFFFF) - 0x8000).astype(I16)

    _for_chunks(i, lambda c: score_body(c, False))
    score_body(i, True)

    def count_ge(ref, cand):
        cb = rows16(cand)

        def body(c, accs):
            off = pl.multiple_of(c * kc, kc)
            slab = ref[pl.ds(off, kc), :]
            accs = list(accs)
            for r in range(n_grp):
                kk = slab[r * PACK16_ROWS:(r + 1) * PACK16_ROWS, :]
                accs[r % N_COUNT_ACC] = accs[r % N_COUNT_ACC] + jnp.where(kk >= cb, one16, zero16)
            return tuple(accs)

        def group(j, accs):
            for u in range(CHUNKS_PER_TRIP):
                accs = body(CHUNKS_PER_TRIP * j + u, accs)
            return accs

        accs = lax.fori_loop(0, n_chunks // CHUNKS_PER_TRIP, group, (zero16,) * N_COUNT_ACC)
        accs = lax.fori_loop(CHUNKS_PER_TRIP * (n_chunks // CHUNKS_PER_TRIP), n_chunks, body, accs)
        tot = accs[0]
        for a in accs[1:]:
            tot = tot + a
        return jnp.sum(tot.astype(I32), axis=0, keepdims=True)

    def descend(ref, need):
        c0 = count_ge(ref, jnp.zeros((1, tq), I32))
        ok0 = c0 >= need
        thr = jnp.where(ok0, 0, INT16_MIN).astype(I32)
        c_gt = jnp.where(ok0, 0, c0)

        def bit_body(j, carry):
            thr, c_gt = carry
            cand = thr | (jnp.int32(1) << (14 - j))
            cnt = count_ge(ref, cand)
            ok = cnt >= need
            return jnp.where(ok, cand, thr), jnp.where(ok, c_gt, cnt)

        return lax.fori_loop(0, 15, bit_body, (thr, c_gt))

    thr_hi, c_gt_hi = descend(hi_s, top_k)
    thr_hi16 = rows16(thr_hi)
    need_lo = top_k - c_gt_hi

    def restrict_body(c, carry):
        off = pl.multiple_of(c * kc, kc)
        in_bucket = hi_s[pl.ds(off, kc), :] == jnp.concatenate([thr_hi16] * n_grp, axis=0)
        lo_s[pl.ds(off, kc), :] = jnp.where(in_bucket, lo_s[pl.ds(off, kc), :], jnp.int16(INT16_MIN))
        return carry

    lax.fori_loop(0, n_chunks, restrict_body, 0)
    thr_lo, c_gt_lo = descend(lo_s, need_lo)
    thr_lo16 = rows16(thr_lo)
    need_tie = (need_lo - c_gt_lo).astype(F32)

    r_i = lax.broadcasted_iota(I32, (kc, kc), 0)
    c_i = lax.broadcasted_iota(I32, (kc, kc), 1)
    tri_s[0:kc, 0:kc] = jnp.where(c_i <= r_i, 1.0, 0.0).astype(BF16)
    tri_s[0:kc, kc:2 * kc] = jnp.where(c_i == r_i, TIE_BIG, 0.0).astype(BF16)
    tri_s[kc:kc + PACK16_ROWS, 0:kc] = jnp.ones((PACK16_ROWS, kc), BF16)
    tri_s[kc:kc + PACK16_ROWS, kc:2 * kc] = jnp.zeros((PACK16_ROWS, kc), BF16)
    thr_hi_kc = jnp.concatenate([thr_hi16] * n_grp, axis=0)
    thr_lo_kc = jnp.concatenate([thr_lo16] * n_grp, axis=0)
    one_b, zero_b, neg_b = (jnp.full((kc, tq), v, BF16) for v in (1.0, 0.0, -1.0))

    def bias_body(c, seen, diagonal):
        off = pl.multiple_of(c * kc, kc)
        hh = hi_s[pl.ds(off, kc), :]
        ll = lo_s[pl.ds(off, kc), :]
        in_bucket = hh == thr_hi_kc
        lo_eq = ll == thr_lo_kc
        equal = jnp.where(in_bucket, jnp.where(lo_eq, one_b, zero_b), zero_b)
        side = jnp.where(hh > thr_hi_kc, neg_b,
                         jnp.where(in_bucket, jnp.where(ll > thr_lo_kc, neg_b, jnp.where(lo_eq, zero_b, one_b)),
                                   one_b))
        rank = _dot(tri_s[...], jnp.concatenate([equal, side], axis=0))
        b = jnp.where(rank[0:kc, :] <= need_tie - seen, 0.0, NEG)
        if diagonal:
            key_pos = off + lax.broadcasted_iota(I32, (kc, tq), 0)
            b = jnp.where(key_pos <= q_pos, b, NEG)
        bias_s[pl.ds(off, kc), :] = b
        return seen + rank[kc:kc + 1, :]

    def bias_group(j, seen):
        for u in range(CHUNKS_PER_TRIP):
            seen = bias_body(CHUNKS_PER_TRIP * j + u, seen, False)
        return seen

    seen = lax.fori_loop(0, i // CHUNKS_PER_TRIP, bias_group, jnp.zeros((1, tq), F32))
    seen = lax.fori_loop(CHUNKS_PER_TRIP * (i // CHUNKS_PER_TRIP), i,
                         lambda c, seen: bias_body(c, seen, False), seen)
    bias_body(i, seen, True)

    feat = lax.broadcasted_iota(I32, (WIDTH, tq), 0)
    qt = qt_ref[0]
    for h in range(A_HEADS):
        qm_s[:, h * tq:(h + 1) * tq] = jnp.where(feat // A_HEAD_DIM == h, qt, jnp.zeros_like(qt))
    acc_s[...] = jnp.zeros_like(acc_s)
    m_s[0:1, :] = jnp.full((1, A_HEADS * tq), NEG, F32)

    span = 2 * kc

    def logits_block(off):
        bias = bias_s[pl.ds(off, kc), :]
        return _dot(k_ref[0, pl.ds(off, kc), :], qm_s[...]) + jnp.concatenate([bias] * A_HEADS, axis=1)

    def pv_block(off, pb, h):
        rows = slice(h * VT_HEAD_ROWS, (h + 1) * VT_HEAD_ROWS)
        return _dot(vt_ref[0, 0, rows, pl.ds(off, kc)], pb[:, h * tq:(h + 1) * tq])

    def rescale_and_add(alpha, pv):
        for h in range(A_HEADS):
            rows = slice(h * VT_HEAD_ROWS, (h + 1) * VT_HEAD_ROWS)
            acc_s[rows, :] = acc_s[rows, :] * alpha[:, h * tq:(h + 1) * tq] + pv[h]

    def attend(off, n_blk):
        s = [logits_block(off + r * kc) for r in range(n_blk)]
        m_old = m_s[0:1, :]
        m_new = m_old
        for r in range(n_blk):
            m_new = jnp.maximum(m_new, jnp.max(s[r], axis=0, keepdims=True))
        m_s[0:1, :] = m_new
        pv = [None] * A_HEADS
        for r in range(n_blk):
            pb = jnp.exp2(s[r] - m_new).astype(BF16)
            for h in range(A_HEADS):
                d = pv_block(off + r * kc, pb, h)
                pv[h] = d if pv[h] is None else pv[h] + d
        rescale_and_add(jnp.exp2(m_old - m_new), pv)

    def logits_span(off, buf):
        s_ref = (s0_s, s1_s)[buf]
        lm = None
        for r in range(span // kc):
            s = logits_block(off + r * kc)
            s_ref[r * kc:(r + 1) * kc, :] = s
            lm_r = jnp.max(s, axis=0, keepdims=True)
            lm = lm_r if lm is None else jnp.maximum(lm, lm_r)
        lm_s[buf:buf + 1, :] = lm

    def step(off_next, off_cur, buf_next):
        buf_cur = 1 - buf_next
        s_next, s_cur = (s0_s, s1_s)[buf_next], (s0_s, s1_s)[buf_cur]
        m_old = m_s[0:1, :]
        m_new = jnp.maximum(m_old, lm_s[buf_cur:buf_cur + 1, :])
        m_s[0:1, :] = m_new
        lm, pv = None, [None] * A_HEADS
        for r in range(span // kc):
            blk = slice(r * kc, (r + 1) * kc)
            s = logits_block(off_next + r * kc)
            s_next[blk, :] = s
            lm_r = jnp.max(s, axis=0, keepdims=True)
            lm = lm_r if lm is None else jnp.maximum(lm, lm_r)
            pb = jnp.exp2(s_cur[blk, :] - m_new).astype(BF16)
            for h in range(A_HEADS):
                d = pv_block(off_cur + r * kc, pb, h)
                pv[h] = d if pv[h] is None else pv[h] + d
        lm_s[buf_next:buf_next + 1, :] = lm
        rescale_and_add(jnp.exp2(m_old - m_new), pv)

    n_full = n_chunks // 2

    @pl.when(n_chunks % 2 == 1)
    def _():
        attend(pl.multiple_of(i * kc, kc), 1)

    def drain(off_cur, buf_cur):
        s_cur = (s0_s, s1_s)[buf_cur]
        m_old = m_s[0:1, :]
        m_new = jnp.maximum(m_old, lm_s[buf_cur:buf_cur + 1, :])
        m_s[0:1, :] = m_new
        pv = [None] * A_HEADS
        for r in range(span // kc):
            pb = jnp.exp2(s_cur[r * kc:(r + 1) * kc, :] - m_new).astype(BF16)
            for h in range(A_HEADS):
                d = pv_block(off_cur + r * kc, pb, h)
                pv[h] = d if pv[h] is None else pv[h] + d
        rescale_and_add(jnp.exp2(m_old - m_new), pv)

    @pl.when(n_full > 0)
    def _():
        logits_span(0, 0)
        n_steps = n_full - 1

        def body(jj, carry):
            off0 = pl.multiple_of(2 * jj * span, span)
            step(off0 + span, off0, 1)
            step(off0 + 2 * span, off0 + span, 0)
            return carry

        lax.fori_loop(0, n_steps // 2, body, 0)

        @pl.when(n_steps % 2 == 1)
        def _():
            off_cur = pl.multiple_of((n_steps - 1) * span, span)
            step(off_cur + span, off_cur, 1)

        off_last = pl.multiple_of((n_full - 1) * span, span)
        for parity in range(2):
            @pl.when((n_full - 1) % 2 == parity)
            def _():
                drain(off_last, parity)

    out = []
    for h in range(A_HEADS):
        r0 = h * VT_HEAD_ROWS
        out.append(acc_s[r0:r0 + VT_HEAD_DIM, :] * (1.0 / acc_s[r0 + VT_HEAD_DIM:r0 + VT_HEAD_DIM + 1, :]))
    o_ref[0] = jnp.concatenate(out, axis=0).T.astype(BF16)


def _dsa(qt, vt, pn, ikn, iwt, tq):
    bsz, seq, _ = pn.shape
    top_k = min(TOPK_MAX, seq // 4)
    grid = (bsz, seq // tq)
    return pl.pallas_call(
        functools.partial(_dsa_kernel, tq=tq, top_k=top_k),
        grid=grid,
        in_specs=[
            pl.BlockSpec((1, WIDTH, tq), lambda b, i: (b, 0, i)),
            pl.BlockSpec((1, WIDTH, tq), lambda b, i: (b, 1, i)),
            pl.BlockSpec((1, SUBLANES, tq), lambda b, i: (b, 0, i)),
            pl.BlockSpec((1, seq, IDX_DIM), lambda b, i: (b, 0, 0)),
            pl.BlockSpec((1, seq, WIDTH), lambda b, i: (b, 0, 0)),
            pl.BlockSpec((1, 1, VT_ROWS, seq), lambda b, i: (b, 0, 0, 0)),
        ],
        out_specs=pl.BlockSpec((1, tq, WIDTH), lambda b, i: (b, i, 0)),
        out_shape=jax.ShapeDtypeStruct((bsz, seq, WIDTH), BF16),
        scratch_shapes=[pltpu.VMEM((seq, tq), I16),
                        pltpu.VMEM((seq, tq), I16),
                        pltpu.VMEM((seq, tq), F32),
                        pltpu.VMEM((WIDTH, A_HEADS * tq), BF16),
                        pltpu.VMEM((VT_ROWS, tq), F32),
                        pltpu.VMEM((SUBLANES, A_HEADS * tq), F32),
                        pltpu.VMEM((tq + PACK16_ROWS, 2 * tq), BF16),
                        pltpu.VMEM((2 * tq, A_HEADS * tq), F32),
                        pltpu.VMEM((2 * tq, A_HEADS * tq), F32),
                        pltpu.VMEM((SUBLANES, A_HEADS * tq), F32)],
        compiler_params=pltpu.CompilerParams(
            dimension_semantics=("parallel", "arbitrary"), vmem_limit_bytes=VMEM_LIMIT_BYTES),
        name="dsa",
    )(qt, qt, iwt, ikn, pn, vt)


def _diff_kernel(qt_ref, k_ref, vt_ref, lam_ref, sg_ref, o_ref, qm_s, acc_s, m_s, s0_s, s1_s, lm_s,
                 *, tq, lambda_init):
    i = pl.program_id(1)
    span = tq
    blk = DIFF_BLOCK_ROWS
    n_blk = span // blk
    n_pairs = 2 * C_HEADS
    grp = n_pairs // 2
    feat = lax.broadcasted_iota(I32, (WIDTH, tq), 0)
    qt = qt_ref[0]
    for p_i in range(n_pairs):
        qm_s[p_i // grp, :, (p_i % grp) * tq:(p_i % grp + 1) * tq] = jnp.where(
            feat // C_HEAD_DIM == p_i, qt, jnp.zeros_like(qt))
    acc_s[...] = jnp.zeros_like(acc_s)
    m_s[0:1, :] = jnp.full((1, n_pairs * tq), NEG, F32)
    bufs = (s0_s, s1_s)
    pair_rows = [slice((p_i // 2) * VT_HEAD_ROWS, (p_i // 2 + 1) * VT_HEAD_ROWS) for p_i in range(n_pairs)]

    def logits_block(off, r, g, masked):
        s = _dot(k_ref[0, pl.ds(off + r * blk, blk), :], qm_s[g])
        if masked:
            row = r * blk + lax.broadcasted_iota(I32, (blk, tq), 0)
            ok = row <= lax.broadcasted_iota(I32, (blk, tq), 1)
            s = jnp.where(jnp.concatenate([ok] * grp, axis=1), s, NEG)
        return s

    def logits(off, buf, masked):
        for g in range(2):
            lm = None
            for r in range(n_blk):
                s = logits_block(off, r, g, masked)
                bufs[buf][g, r * blk:(r + 1) * blk, :] = s
                lm_r = jnp.max(s, axis=0, keepdims=True)
                lm = lm_r if lm is None else jnp.maximum(lm, lm_r)
            lm_s[buf:buf + 1, g * grp * tq:(g + 1) * grp * tq] = lm

    def new_max(buf_cur):
        m_new, alpha = [], []
        for g in range(2):
            st = slice(g * grp * tq, (g + 1) * grp * tq)
            m_old = m_s[0:1, st]
            m_new.append(jnp.maximum(m_old, lm_s[buf_cur:buf_cur + 1, st]))
            alpha.append(jnp.exp2(m_old - m_new[g]))
            m_s[0:1, st] = m_new[g]
        return m_new, alpha

    def pv_block(off_cur, r, g, pb, pv):
        for j in range(grp):
            p_i = g * grp + j
            d = _dot(vt_ref[0, 0, pair_rows[p_i], pl.ds(off_cur + r * blk, blk)], pb[:, j * tq:(j + 1) * tq])
            pv[p_i] = d if pv[p_i] is None else pv[p_i] + d

    def rescale_and_add(alpha, pv):
        for p_i in range(n_pairs):
            a = alpha[p_i // grp][:, (p_i % grp) * tq:(p_i % grp + 1) * tq]
            acc_s[p_i % 2, pair_rows[p_i], :] = acc_s[p_i % 2, pair_rows[p_i], :] * a + pv[p_i]

    def accumulate(off_cur, buf_cur):
        m_new, alpha = new_max(buf_cur)
        pv = [None] * n_pairs
        for r in range(n_blk):
            for g in range(2):
                pb = jnp.exp2(bufs[buf_cur][g, r * blk:(r + 1) * blk, :] - m_new[g]).astype(BF16)
                pv_block(off_cur, r, g, pb, pv)
        rescale_and_add(alpha, pv)

    def step(off_next, off_cur, buf_next):
        buf_cur = 1 - buf_next
        m_new, alpha = new_max(buf_cur)
        lm, pv = [None, None], [None] * n_pairs
        for r in range(n_blk):
            for g in range(2):
                s = logits_block(off_next, r, g, False)
                bufs[buf_next][g, r * blk:(r + 1) * blk, :] = s
                lm_r = jnp.max(s, axis=0, keepdims=True)
                lm[g] = lm_r if lm[g] is None else jnp.maximum(lm[g], lm_r)
                pb = jnp.exp2(bufs[buf_cur][g, r * blk:(r + 1) * blk, :] - m_new[g]).astype(BF16)
                pv_block(off_cur, r, g, pb, pv)
        for g in range(2):
            lm_s[buf_next:buf_next + 1, g * grp * tq:(g + 1) * grp * tq] = lm[g]
        rescale_and_add(alpha, pv)

    logits(pl.multiple_of(i * span, span), 0, True)

    def item_off(t):
        return pl.multiple_of(jnp.where(t == 0, i, t - 1) * span, span)

    def body(jj, carry):
        step(item_off(2 * jj + 1), item_off(2 * jj), 1)
        step(item_off(2 * jj + 2), item_off(2 * jj + 1), 0)
        return carry

    lax.fori_loop(0, i // 2, body, 0)

    @pl.when(i % 2 == 1)
    def _():
        step(item_off(i), item_off(i - 1), 1)

    for parity in range(2):
        @pl.when(i % 2 == parity)
        def _():
            accumulate(item_off(i), parity)

    lq = lam_ref[...]
    lam = (jnp.exp(jnp.sum(lq[0:1] * lq[1:2], axis=1, keepdims=True))
           - jnp.exp(jnp.sum(lq[2:3] * lq[3:4], axis=1, keepdims=True)) + lambda_init)
    outs = []
    for h in range(C_HEADS):
        r0 = h * VT_HEAD_ROWS
        l_row = slice(r0 + VT_HEAD_DIM, r0 + VT_HEAD_DIM + 1)
        a1 = acc_s[0, r0:r0 + VT_HEAD_DIM, :] * (1.0 / acc_s[0, l_row, :])
        a2 = acc_s[1, r0:r0 + VT_HEAD_DIM, :] * (1.0 / acc_s[1, l_row, :])
        out = a1 - lam * a2
        rs = lax.rsqrt(jnp.mean(out * out, axis=0, keepdims=True) + EPS)
        outs.append(out * rs * sg_ref[...] * (1.0 - lambda_init))
    o_ref[0] = jnp.concatenate(outs, axis=0).T.astype(BF16)


def _diff(qt, vt, pn, lam_p, sg_col, tq, lambda_init):
    bsz, seq, _ = pn.shape
    grid = (bsz, seq // tq)
    n_pairs = 2 * C_HEADS
    return pl.pallas_call(
        functools.partial(_diff_kernel, tq=tq, lambda_init=lambda_init),
        grid=grid,
        in_specs=[
            pl.BlockSpec((1, WIDTH, tq), lambda b, i: (b, 2, i)),
            pl.BlockSpec((1, seq, WIDTH), lambda b, i: (b, 0, 3)),
            pl.BlockSpec((1, 1, VT_ROWS, seq), lambda b, i: (b, 1, 0, 0)),
            pl.BlockSpec(lam_p.shape, lambda b, i: (0, 0)),
            pl.BlockSpec(sg_col.shape, lambda b, i: (0, 0)),
        ],
        out_specs=pl.BlockSpec((1, tq, WIDTH), lambda b, i: (b, i, 0)),
        out_shape=jax.ShapeDtypeStruct((bsz, seq, WIDTH), BF16),
        scratch_shapes=[pltpu.VMEM((2, WIDTH, (n_pairs // 2) * tq), BF16),
                        pltpu.VMEM((2, VT_ROWS, tq), F32),
                        pltpu.VMEM((SUBLANES, n_pairs * tq), F32),
                        pltpu.VMEM((2, tq, (n_pairs // 2) * tq), F32),
                        pltpu.VMEM((2, tq, (n_pairs // 2) * tq), F32),
                        pltpu.VMEM((SUBLANES, n_pairs * tq), F32)],
        compiler_params=pltpu.CompilerParams(
            dimension_semantics=("parallel", "arbitrary"), vmem_limit_bytes=VMEM_LIMIT_BYTES),
        name="diff",
    )(qt, pn, vt, lam_p, sg_col)


def _gelu_tanh(x):
    return x * (0.5 * (1.0 + jnp.tanh(math.sqrt(2.0 / math.pi) * (x + 0.044715 * (x * x * x)))))


def _merge_kernel(x_ref, ya_ref, bu_ref, bv_ref, yc_ref, gt_ref, lg_ref, lb_ref, ws_ref, bs_ref,
                  wa_ref, wb_ref, wc_ref, wo_ref, o_ref, yb_s, *, tm):
    d = x_ref.shape[-1]
    lane_w = lax.broadcasted_iota(I32, (CHUNK, B_WIDTH), 1)
    group_of_lane = lane_w // (B_WIDTH // B_GROUPS)
    r_i = lax.broadcasted_iota(I32, (CHUNK, CHUNK), 0)
    c_i = lax.broadcasted_iota(I32, (CHUNK, CHUNK), 1)
    w_tril = [jnp.where(c_i <= r_i, ws_ref[g], 0.0).astype(BF16) for g in range(B_GROUPS)]
    for c in range(tm // CHUNK):
        rows = slice(c * CHUNK, (c + 1) * CHUNK)
        u = _gelu_tanh(bu_ref[0, rows, :].astype(F32))
        v = _gelu_tanh(bv_ref[0, rows, :].astype(F32))
        mu = jnp.mean(v, axis=-1, keepdims=True)
        vc = v - mu
        var = jnp.mean(vc * vc, axis=-1, keepdims=True)
        vn = (vc * lax.rsqrt(var + EPS) * lg_ref[...] + lb_ref[...]).astype(BF16)
        s = bs_ref[...]
        for g in range(B_GROUPS):
            s = s + _dot(w_tril[g], jnp.where(group_of_lane == g, vn, jnp.zeros_like(vn)))
        yb_s[rows, :] = (u * s).astype(BF16)

    merged = None
    branches = ((ya_ref[0], wa_ref), (yb_s[...], wb_ref), (yc_ref[0], wc_ref))
    for n, (y, w_ref) in enumerate(branches):
        y_half = _dot(y * 0.5, w_ref[...])
        term = y_half + y_half * jnp.tanh(gt_ref[0, :, n * d:(n + 1) * d].astype(F32))
        merged = term if merged is None else merged + term
    o_ref[0] = x_ref[0] + _dot(merged.astype(BF16), wo_ref[...])


def _merge(x, pn, ya, yc, gates, lg, lb, ws, bs_full, wa, wb, wc, wo, tm):
    bsz, seq, d = x.shape
    grid = (bsz, seq // tm)
    const2 = lambda b, i: (0, 0)
    return pl.pallas_call(
        functools.partial(_merge_kernel, tm=tm),
        grid=grid,
        in_specs=[
            pl.BlockSpec((1, tm, d), lambda b, i: (b, i, 0)),
            pl.BlockSpec((1, tm, WIDTH), lambda b, i: (b, i, 0)),
            pl.BlockSpec((1, tm, WIDTH), lambda b, i: (b, i, 1)),
            pl.BlockSpec((1, tm, WIDTH), lambda b, i: (b, i, 2)),
            pl.BlockSpec((1, tm, WIDTH), lambda b, i: (b, i, 0)),
            pl.BlockSpec((1, tm, 3 * d), lambda b, i: (b, i, 0)),
            pl.BlockSpec(lg.shape, const2),
            pl.BlockSpec(lb.shape, const2),
            pl.BlockSpec(ws.shape, lambda b, i: (0, 0, 0)),
            pl.BlockSpec(bs_full.shape, const2),
            pl.BlockSpec(wa.shape, const2),
            pl.BlockSpec(wb.shape, const2),
            pl.BlockSpec(wc.shape, const2),
            pl.BlockSpec(wo.shape, const2),
        ],
        out_specs=pl.BlockSpec((1, tm, d), lambda b, i: (b, i, 0)),
        out_shape=jax.ShapeDtypeStruct((bsz, seq, d), F32),
        scratch_shapes=[pltpu.VMEM((tm, B_WIDTH), BF16)],
        compiler_params=pltpu.CompilerParams(
            dimension_semantics=("parallel", "parallel"), vmem_limit_bytes=VMEM_LIMIT_BYTES),
        name="merge",
    )(x, ya, pn, pn, yc, gates, lg, lb, ws, bs_full, wa, wb, wc, wo)


def _ffn_kernel(x_ref, g_ref, w1_ref, w2_ref, fg_ref, o_ref, h_s, acc_s, *, final_norm):
    j = pl.program_id(1)

    @pl.when(j == 0)
    def _():
        x = x_ref[...]
        ms = jnp.mean(x * x, axis=-1, keepdims=True)
        h_s[...] = (x * lax.rsqrt(ms + EPS) * g_ref[...]).astype(BF16)
        acc_s[...] = jnp.zeros_like(acc_s)

    a = jnp.maximum(_dot(h_s[...], w1_ref[...]), 0.0)
    acc_s[...] += _dot((a * a).astype(BF16), w2_ref[...])

    @pl.when(j == pl.num_programs(1) - 1)
    def _():
        y = x_ref[...] + acc_s[...]
        if final_norm:
            ms = jnp.mean(y * y, axis=-1, keepdims=True)
            y = y * lax.rsqrt(ms + EPS) * fg_ref[...]
        o_ref[...] = y


def _ffn(x2d, g, w1, w2, fg, tm, tf, final_norm):
    m, d = x2d.shape
    dff = w1.shape[1]
    grid = (m // tm, dff // tf)
    return pl.pallas_call(
        functools.partial(_ffn_kernel, final_norm=final_norm),
        grid=grid,
        in_specs=[
            pl.BlockSpec((tm, d), lambda i, j: (i, 0)),
            pl.BlockSpec((1, d), lambda i, j: (0, 0)),
            pl.BlockSpec((d, tf), lambda i, j: (0, j)),
            pl.BlockSpec((tf, d), lambda i, j: (j, 0)),
            pl.BlockSpec((1, d), lambda i, j: (0, 0)),
        ],
        out_specs=pl.BlockSpec((tm, d), lambda i, j: (i, 0)),
        out_shape=jax.ShapeDtypeStruct((m, d), F32),
        scratch_shapes=[pltpu.VMEM((tm, d), BF16), pltpu.VMEM((tm, d), F32)],
        compiler_params=pltpu.CompilerParams(
            dimension_semantics=("parallel", "arbitrary"), vmem_limit_bytes=VMEM_LIMIT_BYTES),
        name="ffn",
    )(x2d, g, w1, w2, fg)


def _tile(n, pref):
    t = min(n, pref)
    assert n % t == 0, (n, t)
    return t


def kernel(x, attn_norm_g, w_in, idx_k_norm_g, idx_k_norm_b, sgu_norm_g, sgu_norm_b, sgu_w_s, sgu_b_s,
           diff_lambda, diff_subln_g, w_branch_a, w_branch_b, w_branch_c, w_out, mlp_norm_g, w_ff1,
           w_ff2, final_norm_g):
    bsz, seq, d = x.shape
    depth = w_in.shape[0]
    offs = [0]
    for s in IN_SIZES:
        offs.append(offs[-1] + s)
    (o_aq, o_ak, o_av, o_iq, o_ik, o_iw, o_buv, o_cq, o_ck, o_cv, o_g, o_end) = offs

    tm_proj = _tile(seq, 512)
    tq = _tile(seq, 256)
    tq_diff = _tile(seq, 512)
    tm_merge = _tile(seq, 512)
    tm_ffn = _tile(bsz * seq, 1024)
    tf = _tile(w_ff1.shape[2], 1024)

    for l in range(depth):
        lambda_init = 0.8 - 0.6 * math.exp(-0.3 * l)
        w = w_in[l]
        cols = lambda a, b: w[:, a:b]
        wn = jnp.concatenate([cols(o_ak, o_av), cols(o_buv, o_cq), cols(o_ck, o_cv)], axis=1).astype(BF16)
        wt = jnp.concatenate([cols(o_aq, o_ak), cols(o_iq, o_ik), cols(o_cq, o_ck), cols(o_av, o_iq),
                              cols(o_cv, o_g), cols(o_iw, o_buv),
                              jnp.zeros((d, SUBLANES - IDX_HEADS), w.dtype)], axis=1).T.astype(BF16)
        wik = cols(o_ik, o_iw).astype(BF16)
        wg = cols(o_g, o_end).astype(BF16)

        pn, ikn, gates, qt, vt, iwt = _proj(
            x, attn_norm_g[l][None, :], wn, wik, wg, wt,
            idx_k_norm_g[l][None, :], idx_k_norm_b[l][None, :], tm_proj)

        ya = _dsa(qt, vt, pn, ikn, iwt, tq)
        sg_col = diff_subln_g[l][:, None]
        yc = _diff(qt, vt, pn, diff_lambda[l], sg_col, tq_diff, lambda_init)

        bs_full = jnp.repeat(sgu_b_s[l].T, B_WIDTH // B_GROUPS, axis=1)
        x = _merge(x, pn, ya, yc, gates, sgu_norm_g[l][None, :], sgu_norm_b[l][None, :], sgu_w_s[l],
                   bs_full, w_branch_a[l].astype(BF16), w_branch_b[l].astype(BF16),
                   w_branch_c[l].astype(BF16), w_out[l].astype(BF16), tm_merge)

        x = _ffn(x.reshape(bsz * seq, d), mlp_norm_g[l][None, :], w_ff1[l].astype(BF16),
                 w_ff2[l].astype(BF16), final_norm_g[None, :], tm_ffn, tf,
                 final_norm=(l == depth - 1)).reshape(bsz, seq, d)
    return x
```

```python
import functools
import math

import jax
import jax.numpy as jnp
from jax import lax
from jax.experimental import pallas as pl
from jax.experimental.pallas import tpu as pltpu

F32 = jnp.float32
BF16 = jnp.bfloat16
I32 = jnp.int32
I16 = jnp.int16

A_HEADS = 4
A_HEAD_DIM = 64
IDX_HEADS = 4
IDX_DIM = 64
TOPK_MAX = 256
B_GROUPS = 4
B_WIDTH = 256
CHUNK = 128
C_HEADS = 4
C_HEAD_DIM = 32
WIDTH = 256
EPS = 1e-6
NEG = -1e30
INT16_MIN = -(2 ** 15)
TIE_BIG = 16384.0
IN_SIZES = (256, 256, 256, 256, 64, 4, 512, 256, 256, 256, 3072)

SUBLANES = 8
PACK16_ROWS = 2 * SUBLANES
VMEM_LIMIT_BYTES = 56 * 1024 * 1024

LOG2E = math.log2(math.e)
VT_HEAD_DIM = 64
VT_HEAD_ROWS = VT_HEAD_DIM + PACK16_ROWS
VT_ROWS = (WIDTH // VT_HEAD_DIM) * VT_HEAD_ROWS

PROJ_ROWS = 512
PROJ_GATE_COLS = 768
DSA_QUERIES = 256
DIFF_QUERIES = 512
DIFF_BLOCK_ROWS = 256
MERGE_ROWS = 512
FFN_ROWS = 1024
FFN_HIDDEN_COLS = 1024
N_COUNT_ACC = 4
CHUNKS_PER_TRIP = 4


def _nt_dot(a, b):
    return lax.dot_general(a, b, (((1,), (1,)), ((), ())), preferred_element_type=F32)


def _dot(a, b):
    return jnp.dot(a, b, preferred_element_type=F32)


def _for_chunks(n, body):
    def group(j, carry):
        for u in range(CHUNKS_PER_TRIP):
            body(CHUNKS_PER_TRIP * j + u)
        return carry

    lax.fori_loop(0, n // CHUNKS_PER_TRIP, group, 0)

    def single(c, carry):
        body(c)
        return carry

    lax.fori_loop(CHUNKS_PER_TRIP * (n // CHUNKS_PER_TRIP), n, single, 0)


def _proj_kernel(x_ref, g_ref, wn_ref, wik_ref, wg_ref, wt_ref, ikg_ref, ikb_ref,
                 pn_ref, ikn_ref, gt_ref, qt_ref, vt_ref, iwt_ref, *, gate_chunk):
    x = x_ref[0]
    tm = x.shape[0]
    ms = jnp.mean(x * x, axis=-1, keepdims=True)
    h = (x * lax.rsqrt(ms + EPS) * g_ref[...]).astype(BF16)
    for c0 in range(0, wn_ref.shape[1], WIDTH):
        pn_ref[0, :, c0:c0 + WIDTH] = _dot(h, wn_ref[:, c0:c0 + WIDTH]).astype(BF16)
    for c0 in range(0, wg_ref.shape[1], gate_chunk):
        gt_ref[0, :, c0:c0 + gate_chunk] = (_dot(h, wg_ref[:, c0:c0 + gate_chunk]) * 0.5).astype(BF16)
    for blk, q_scale in enumerate((A_HEAD_DIM ** -0.5 * LOG2E, None, C_HEAD_DIM ** -0.5 * LOG2E)):
        r = _nt_dot(wt_ref[blk * WIDTH:(blk + 1) * WIDTH, :], h)
        if q_scale is not None:
            r = r * q_scale
        qt_ref[0, blk * WIDTH:(blk + 1) * WIDTH, :] = r.astype(BF16)
    for blk in range(2):
        r0 = (3 + blk) * WIDTH
        r = _nt_dot(wt_ref[r0:r0 + WIDTH, :], h).astype(BF16)
        for hd in range(WIDTH // VT_HEAD_DIM):
            o0 = hd * VT_HEAD_ROWS
            vt_ref[0, blk, o0:o0 + VT_HEAD_DIM, :] = r[hd * VT_HEAD_DIM:(hd + 1) * VT_HEAD_DIM, :]
            vt_ref[0, blk, o0 + VT_HEAD_DIM:o0 + VT_HEAD_ROWS, :] = jnp.ones(
                (VT_HEAD_ROWS - VT_HEAD_DIM, tm), BF16)
    iwt_ref[0] = _nt_dot(wt_ref[5 * WIDTH:5 * WIDTH + SUBLANES, :], h)
    ik = _dot(h, wik_ref[...])
    mu = jnp.mean(ik, axis=-1, keepdims=True)
    xc = ik - mu
    var = jnp.mean(xc * xc, axis=-1, keepdims=True)
    ikn_ref[0] = (xc * lax.rsqrt(var + EPS) * ikg_ref[...] + ikb_ref[...]).astype(BF16)


def _proj(x, g, wn, wik, wg, wt, ikg, ikb, tm):
    bsz, seq, d = x.shape
    n_nat, n_gate = wn.shape[1], wg.shape[1]
    assert wt.shape[0] == 5 * WIDTH + SUBLANES
    grid = (bsz, seq // tm)
    const = lambda b, i: (0, 0)
    return pl.pallas_call(
        functools.partial(_proj_kernel, gate_chunk=min(n_gate, PROJ_GATE_COLS)),
        grid=grid,
        in_specs=[
            pl.BlockSpec((1, tm, d), lambda b, i: (b, i, 0)),
            pl.BlockSpec((1, d), const),
            pl.BlockSpec(wn.shape, const),
            pl.BlockSpec(wik.shape, const),
            pl.BlockSpec(wg.shape, const),
            pl.BlockSpec(wt.shape, const),
            pl.BlockSpec(ikg.shape, const),
            pl.BlockSpec(ikb.shape, const),
        ],
        out_specs=[
            pl.BlockSpec((1, tm, n_nat), lambda b, i: (b, i, 0)),
            pl.BlockSpec((1, tm, IDX_DIM), lambda b, i: (b, i, 0)),
            pl.BlockSpec((1, tm, n_gate), lambda b, i: (b, i, 0)),
            pl.BlockSpec((1, 3 * WIDTH, tm), lambda b, i: (b, 0, i)),
            pl.BlockSpec((1, 2, VT_ROWS, tm), lambda b, i: (b, 0, 0, i)),
            pl.BlockSpec((1, SUBLANES, tm), lambda b, i: (b, 0, i)),
        ],
        out_shape=[
            jax.ShapeDtypeStruct((bsz, seq, n_nat), BF16),
            jax.ShapeDtypeStruct((bsz, seq, IDX_DIM), BF16),
            jax.ShapeDtypeStruct((bsz, seq, n_gate), BF16),
            jax.ShapeDtypeStruct((bsz, 3 * WIDTH, seq), BF16),
            jax.ShapeDtypeStruct((bsz, 2, VT_ROWS, seq), BF16),
            jax.ShapeDtypeStruct((bsz, SUBLANES, seq), F32),
        ],
        compiler_params=pltpu.CompilerParams(
            dimension_semantics=("parallel", "parallel"), vmem_limit_bytes=VMEM_LIMIT_BYTES),
        name="proj",
    )(x, g, wn, wik, wg, wt, ikg, ikb)


def _dsa_kernel(qt_ref, iqt_ref, iwt_ref, ikn_ref, k_ref, vt_ref, o_ref,
                hi_s, lo_s, bias_s, qm_s, acc_s, m_s, tri_s, s0_s, s1_s, lm_s, *, tq, top_k):
    i = pl.program_id(1)
    kc = tq
    n_chunks = i + 1
    n_grp = kc // PACK16_ROWS
    q_pos = i * tq + lax.broadcasted_iota(I32, (1, tq), 1)
    one16 = jnp.ones((PACK16_ROWS, tq), I16)
    zero16 = jnp.zeros((PACK16_ROWS, tq), I16)

    def rows16(v):
        return jnp.broadcast_to(v, (PACK16_ROWS, tq)).astype(I16)

    w_all = iwt_ref[0] * (IDX_DIM ** -0.5 * IDX_HEADS ** -0.5)
    w_rows = [w_all[h:h + 1, :] for h in range(IDX_HEADS)]
    for h in range(IDX_HEADS):
        qm_s[0:IDX_DIM, h * tq:(h + 1) * tq] = iqt_ref[0, h * IDX_DIM:(h + 1) * IDX_DIM, :]

    def score_body(c, diagonal):
        off = pl.multiple_of(c * kc, kc)
        ik = ikn_ref[0, pl.ds(off, kc), :]
        d = _dot(ik, qm_s[0:IDX_DIM, :])
        sc = jnp.zeros((kc, tq), F32)
        for h in range(IDX_HEADS):
            sc = sc + jnp.maximum(d[:, h * tq:(h + 1) * tq], 0.0) * w_rows[h]
        if diagonal:
            key_pos = off + lax.broadcasted_iota(I32, (kc, tq), 0)
            sc = jnp.where(key_pos <= q_pos, sc, -jnp.inf)
        bits = pltpu.bitcast(sc, I32)
        key = bits ^ ((bits >> 31) & 0x7FFFFFFF)
        hi_s[pl.ds(off, kc), :] = (key >> 16).astype(I16)
        lo_s[pl.ds(off, kc), :] = ((key & 0xFFFF) - 0x8000).astype(I16)

    _for_chunks(i, lambda c: score_body(c, False))
    score_body(i, True)

    def count_ge(ref, cand):
        cb = rows16(cand)

        def body(c, accs):
            off = pl.multiple_of(c * kc, kc)
            slab = ref[pl.ds(off, kc), :]
            accs = list(accs)
            for r in range(n_grp):
                kk = slab[r * PACK16_ROWS:(r + 1) * PACK16_ROWS, :]
                accs[r % N_COUNT_ACC] = accs[r % N_COUNT_ACC] + jnp.where(kk >= cb, one16, zero16)
            return tuple(accs)

        def group(j, accs):
            for u in range(CHUNKS_PER_TRIP):
                accs = body(CHUNKS_PER_TRIP * j + u, accs)
            return accs

        accs = lax.fori_loop(0, n_chunks // CHUNKS_PER_TRIP, group, (zero16,) * N_COUNT_ACC)
        accs = lax.fori_loop(CHUNKS_PER_TRIP * (n_chunks // CHUNKS_PER_TRIP), n_chunks, body, accs)
        tot = accs[0]
        for a in accs[1:]:
            tot = tot + a
        return jnp.sum(tot.astype(I32), axis=0, keepdims=True)

    def descend(ref, need):
        c0 = count_ge(ref, jnp.zeros((1, tq), I32))
        ok0 = c0 >= need
        thr = jnp.where(ok0, 0, INT16_MIN).astype(I32)
        c_gt = jnp.where(ok0, 0, c0)

        def bit_body(j, carry):
            thr, c_gt = carry
            cand = thr | (jnp.int32(1) << (14 - j))
            cnt = count_ge(ref, cand)
            ok = cnt >= need
            return jnp.where(ok, cand, thr), jnp.where(ok, c_gt, cnt)

        return lax.fori_loop(0, 15, bit_body, (thr, c_gt))

    thr_hi, c_gt_hi = descend(hi_s, top_k)
    thr_hi16 = rows16(thr_hi)
    need_lo = top_k - c_gt_hi

    def restrict_body(c, carry):
        off = pl.multiple_of(c * kc, kc)
        in_bucket = hi_s[pl.ds(off, kc), :] == jnp.concatenate([thr_hi16] * n_grp, axis=0)
        lo_s[pl.ds(off, kc), :] = jnp.where(in_bucket, lo_s[pl.ds(off, kc), :], jnp.int16(INT16_MIN))
        return carry

    lax.fori_loop(0, n_chunks, restrict_body, 0)
    thr_lo, c_gt_lo = descend(lo_s, need_lo)
    thr_lo16 = rows16(thr_lo)
    need_tie = (need_lo - c_gt_lo).astype(F32)

    r_i = lax.broadcasted_iota(I32, (kc, kc), 0)
    c_i = lax.broadcasted_iota(I32, (kc, kc), 1)
    tri_s[0:kc, 0:kc] = jnp.where(c_i <= r_i, 1.0, 0.0).astype(BF16)
    tri_s[0:kc, kc:2 * kc] = jnp.where(c_i == r_i, TIE_BIG, 0.0).astype(BF16)
    tri_s[kc:kc + PACK16_ROWS, 0:kc] = jnp.ones((PACK16_ROWS, kc), BF16)
    tri_s[kc:kc + PACK16_ROWS, kc:2 * kc] = jnp.zeros((PACK16_ROWS, kc), BF16)
    thr_hi_kc = jnp.concatenate([thr_hi16] * n_grp, axis=0)
    thr_lo_kc = jnp.concatenate([thr_lo16] * n_grp, axis=0)
    one_b, zero_b, neg_b = (jnp.full((kc, tq), v, BF16) for v in (1.0, 0.0, -1.0))

    def bias_body(c, seen, diagonal):
        off = pl.multiple_of(c * kc, kc)
        hh = hi_s[pl.ds(off, kc), :]
        ll = lo_s[pl.ds(off, kc), :]
        in_bucket = hh == thr_hi_kc
        lo_eq = ll == thr_lo_kc
        equal = jnp.where(in_bucket, jnp.where(lo_eq, one_b, zero_b), zero_b)
        side = jnp.where(hh > thr_hi_kc, neg_b,
                         jnp.where(in_bucket, jnp.where(ll > thr_lo_kc, neg_b, jnp.where(lo_eq, zero_b, one_b)),
                                   one_b))
        rank = _dot(tri_s[...], jnp.concatenate([equal, side], axis=0))
        b = jnp.where(rank[0:kc, :] <= need_tie - seen, 0.0, NEG)
        if diagonal:
            key_pos = off + lax.broadcasted_iota(I32, (kc, tq), 0)
            b = jnp.where(key_pos <= q_pos, b, NEG)
        bias_s[pl.ds(off, kc), :] = b
        return seen + rank[kc:kc + 1, :]

    def bias_group(j, seen):
        for u in range(CHUNKS_PER_TRIP):
            seen = bias_body(CHUNKS_PER_TRIP * j + u, seen, False)
        return seen

    seen = lax.fori_loop(0, i // CHUNKS_PER_TRIP, bias_group, jnp.zeros((1, tq), F32))
    seen = lax.fori_loop(CHUNKS_PER_TRIP * (i // CHUNKS_PER_TRIP), i,
                         lambda c, seen: bias_body(c, seen, False), seen)
    bias_body(i, seen, True)

    feat = lax.broadcasted_iota(I32, (WIDTH, tq), 0)
    qt = qt_ref[0]
    for h in range(A_HEADS):
        qm_s[:, h * tq:(h + 1) * tq] = jnp.where(feat // A_HEAD_DIM == h, qt, jnp.zeros_like(qt))
    acc_s[...] = jnp.zeros_like(acc_s)
    m_s[0:1, :] = jnp.full((1, A_HEADS * tq), NEG, F32)

    span = 2 * kc

    def logits_block(off):
        bias = bias_s[pl.ds(off, kc), :]
        return _dot(k_ref[0, pl.ds(off, kc), :], qm_s[...]) + jnp.concatenate([bias] * A_HEADS, axis=1)

    def pv_block(off, pb, h):
        rows = slice(h * VT_HEAD_ROWS, (h + 1) * VT_HEAD_ROWS)
        return _dot(vt_ref[0, 0, rows, pl.ds(off, kc)], pb[:, h * tq:(h + 1) * tq])

    def rescale_and_add(alpha, pv):
        for h in range(A_HEADS):
            rows = slice(h * VT_HEAD_ROWS, (h + 1) * VT_HEAD_ROWS)
            acc_s[rows, :] = acc_s[rows, :] * alpha[:, h * tq:(h + 1) * tq] + pv[h]

    def attend_chunk(off):
        s = logits_block(off)
        m_old = m_s[0:1, :]
        m_new = jnp.maximum(m_old, jnp.max(s, axis=0, keepdims=True))
        m_s[0:1, :] = m_new
        pb = jnp.exp2(s - m_new).astype(BF16)
        rescale_and_add(jnp.exp2(m_old - m_new), [pv_block(off, pb, h) for h in range(A_HEADS)])

    def logits_span(off, buf):
        s_ref = (s0_s, s1_s)[buf]
        lm = None
        for r in range(span // kc):
            s = logits_block(off + r * kc)
            s_ref[r * kc:(r + 1) * kc, :] = s
            lm_r = jnp.max(s, axis=0, keepdims=True)
            lm = lm_r if lm is None else jnp.maximum(lm, lm_r)
        lm_s[buf:buf + 1, :] = lm

    def step(off_next, off_cur, buf_next):
        buf_cur = 1 - buf_next
        s_next, s_cur = (s0_s, s1_s)[buf_next], (s0_s, s1_s)[buf_cur]
        m_old = m_s[0:1, :]
        m_new = jnp.maximum(m_old, lm_s[buf_cur:buf_cur + 1, :])
        m_s[0:1, :] = m_new
        lm, pv = None, [None] * A_HEADS
        for r in range(span // kc):
            blk = slice(r * kc, (r + 1) * kc)
            s = logits_block(off_next + r * kc)
            s_next[blk, :] = s
            lm_r = jnp.max(s, axis=0, keepdims=True)
            lm = lm_r if lm is None else jnp.maximum(lm, lm_r)
            pb = jnp.exp2(s_cur[blk, :] - m_new).astype(BF16)
            for h in range(A_HEADS):
                d = pv_block(off_cur + r * kc, pb, h)
                pv[h] = d if pv[h] is None else pv[h] + d
        lm_s[buf_next:buf_next + 1, :] = lm
        rescale_and_add(jnp.exp2(m_old - m_new), pv)

    n_full = n_chunks // 2

    @pl.when(n_chunks % 2 == 1)
    def _():
        attend_chunk(pl.multiple_of(i * kc, kc))

    def drain(off_cur, buf_cur):
        s_cur = (s0_s, s1_s)[buf_cur]
        m_old = m_s[0:1, :]
        m_new = jnp.maximum(m_old, lm_s[buf_cur:buf_cur + 1, :])
        m_s[0:1, :] = m_new
        pv = [None] * A_HEADS
        for r in range(span // kc):
            pb = jnp.exp2(s_cur[r * kc:(r + 1) * kc, :] - m_new).astype(BF16)
            for h in range(A_HEADS):
                d = pv_block(off_cur + r * kc, pb, h)
                pv[h] = d if pv[h] is None else pv[h] + d
        rescale_and_add(jnp.exp2(m_old - m_new), pv)

    @pl.when(n_full > 0)
    def _():
        logits_span(0, 0)
        n_steps = n_full - 1

        def body(jj, carry):
            off0 = pl.multiple_of(2 * jj * span, span)
            step(off0 + span, off0, 1)
            step(off0 + 2 * span, off0 + span, 0)
            return carry

        lax.fori_loop(0, n_steps // 2, body, 0)

        @pl.when(n_steps % 2 == 1)
        def _():
            off_cur = pl.multiple_of((n_steps - 1) * span, span)
            step(off_cur + span, off_cur, 1)

        off_last = pl.multiple_of((n_full - 1) * span, span)
        for parity in range(2):
            @pl.when((n_full - 1) % 2 == parity)
            def _():
                drain(off_last, parity)

    out = []
    for h in range(A_HEADS):
        r0 = h * VT_HEAD_ROWS
        out.append(acc_s[r0:r0 + VT_HEAD_DIM, :] * (1.0 / acc_s[r0 + VT_HEAD_DIM:r0 + VT_HEAD_DIM + 1, :]))
    o_ref[0] = jnp.concatenate(out, axis=0).T.astype(BF16)


def _dsa(qt, vt, pn, ikn, iwt, tq):
    bsz, seq, _ = pn.shape
    top_k = min(TOPK_MAX, seq // 4)
    grid = (bsz, seq // tq)
    return pl.pallas_call(
        functools.partial(_dsa_kernel, tq=tq, top_k=top_k),
        grid=grid,
        in_specs=[
            pl.BlockSpec((1, WIDTH, tq), lambda b, i: (b, 0, i)),
            pl.BlockSpec((1, WIDTH, tq), lambda b, i: (b, 1, i)),
            pl.BlockSpec((1, SUBLANES, tq), lambda b, i: (b, 0, i)),
            pl.BlockSpec((1, seq, IDX_DIM), lambda b, i: (b, 0, 0)),
            pl.BlockSpec((1, seq, WIDTH), lambda b, i: (b, 0, 0)),
            pl.BlockSpec((1, 1, VT_ROWS, seq), lambda b, i: (b, 0, 0, 0)),
        ],
        out_specs=pl.BlockSpec((1, tq, WIDTH), lambda b, i: (b, i, 0)),
        out_shape=jax.ShapeDtypeStruct((bsz, seq, WIDTH), BF16),
        scratch_shapes=[pltpu.VMEM((seq, tq), I16),
                        pltpu.VMEM((seq, tq), I16),
                        pltpu.VMEM((seq, tq), F32),
                        pltpu.VMEM((WIDTH, A_HEADS * tq), BF16),
                        pltpu.VMEM((VT_ROWS, tq), F32),
                        pltpu.VMEM((SUBLANES, A_HEADS * tq), F32),
                        pltpu.VMEM((tq + PACK16_ROWS, 2 * tq), BF16),
                        pltpu.VMEM((2 * tq, A_HEADS * tq), F32),
                        pltpu.VMEM((2 * tq, A_HEADS * tq), F32),
                        pltpu.VMEM((SUBLANES, A_HEADS * tq), F32)],
        compiler_params=pltpu.CompilerParams(
            dimension_semantics=("parallel", "arbitrary"), vmem_limit_bytes=VMEM_LIMIT_BYTES),
        name="dsa",
    )(qt, qt, iwt, ikn, pn, vt)


def _diff_kernel(qt_ref, k_ref, vt_ref, lam_ref, sg_ref, o_ref, qm_s, acc_s, m_s, s0_s, s1_s, lm_s,
                 *, tq, lambda_init):
    i = pl.program_id(1)
    span = tq
    blk = DIFF_BLOCK_ROWS
    n_blk = span // blk
    n_pairs = 2 * C_HEADS
    grp = n_pairs // 2
    feat = lax.broadcasted_iota(I32, (WIDTH, tq), 0)
    qt = qt_ref[0]
    for p_i in range(n_pairs):
        qm_s[p_i // grp, :, (p_i % grp) * tq:(p_i % grp + 1) * tq] = jnp.where(
            feat // C_HEAD_DIM == p_i, qt, jnp.zeros_like(qt))
    acc_s[...] = jnp.zeros_like(acc_s)
    m_s[0:1, :] = jnp.full((1, n_pairs * tq), NEG, F32)
    bufs = (s0_s, s1_s)
    pair_rows = [slice((p_i // 2) * VT_HEAD_ROWS, (p_i // 2 + 1) * VT_HEAD_ROWS) for p_i in range(n_pairs)]

    def logits_block(off, r, g, masked):
        s = _dot(k_ref[0, pl.ds(off + r * blk, blk), :], qm_s[g])
        if masked:
            row = r * blk + lax.broadcasted_iota(I32, (blk, tq), 0)
            ok = row <= lax.broadcasted_iota(I32, (blk, tq), 1)
            s = jnp.where(jnp.concatenate([ok] * grp, axis=1), s, NEG)
        return s

    def logits(off, buf, masked):
        for g in range(2):
            lm = None
            for r in range(n_blk):
                s = logits_block(off, r, g, masked)
                bufs[buf][g, r * blk:(r + 1) * blk, :] = s
                lm_r = jnp.max(s, axis=0, keepdims=True)
                lm = lm_r if lm is None else jnp.maximum(lm, lm_r)
            lm_s[buf:buf + 1, g * grp * tq:(g + 1) * grp * tq] = lm

    def new_max(buf_cur):
        m_new, alpha = [], []
        for g in range(2):
            st = slice(g * grp * tq, (g + 1) * grp * tq)
            m_old = m_s[0:1, st]
            m_new.append(jnp.maximum(m_old, lm_s[buf_cur:buf_cur + 1, st]))
            alpha.append(jnp.exp2(m_old - m_new[g]))
            m_s[0:1, st] = m_new[g]
        return m_new, alpha

    def pv_block(off_cur, r, g, pb, pv):
        for j in range(grp):
            p_i = g * grp + j
            d = _dot(vt_ref[0, 0, pair_rows[p_i], pl.ds(off_cur + r * blk, blk)], pb[:, j * tq:(j + 1) * tq])
            pv[p_i] = d if pv[p_i] is None else pv[p_i] + d

    def rescale_and_add(alpha, pv):
        for p_i in range(n_pairs):
            a = alpha[p_i // grp][:, (p_i % grp) * tq:(p_i % grp + 1) * tq]
            acc_s[p_i % 2, pair_rows[p_i], :] = acc_s[p_i % 2, pair_rows[p_i], :] * a + pv[p_i]

    def accumulate(off_cur, buf_cur):
        m_new, alpha = new_max(buf_cur)
        pv = [None] * n_pairs
        for r in range(n_blk):
            for g in range(2):
                pb = jnp.exp2(bufs[buf_cur][g, r * blk:(r + 1) * blk, :] - m_new[g]).astype(BF16)
                pv_block(off_cur, r, g, pb, pv)
        rescale_and_add(alpha, pv)

    def step(off_next, off_cur, buf_next):
        buf_cur = 1 - buf_next
        m_new, alpha = new_max(buf_cur)
        lm, pv = [None, None], [None] * n_pairs
        for r in range(n_blk):
            for g in range(2):
                s = logits_block(off_next, r, g, False)
                bufs[buf_next][g, r * blk:(r + 1) * blk, :] = s
                lm_r = jnp.max(s, axis=0, keepdims=True)
                lm[g] = lm_r if lm[g] is None else jnp.maximum(lm[g], lm_r)
                pb = jnp.exp2(bufs[buf_cur][g, r * blk:(r + 1) * blk, :] - m_new[g]).astype(BF16)
                pv_block(off_cur, r, g, pb, pv)
        for g in range(2):
            lm_s[buf_next:buf_next + 1, g * grp * tq:(g + 1) * grp * tq] = lm[g]
        rescale_and_add(alpha, pv)

    logits(pl.multiple_of(i * span, span), 0, True)

    def item_off(t):
        return pl.multiple_of(jnp.where(t == 0, i, t - 1) * span, span)

    def body(jj, carry):
        step(item_off(2 * jj + 1), item_off(2 * jj), 1)
        step(item_off(2 * jj + 2), item_off(2 * jj + 1), 0)
        return carry

    lax.fori_loop(0, i // 2, body, 0)

    @pl.when(i % 2 == 1)
    def _():
        step(item_off(i), item_off(i - 1), 1)

    for parity in range(2):
        @pl.when(i % 2 == parity)
        def _():
            accumulate(item_off(i), parity)

    lq = lam_ref[...]
    lam = (jnp.exp(jnp.sum(lq[0:1] * lq[1:2], axis=1, keepdims=True))
           - jnp.exp(jnp.sum(lq[2:3] * lq[3:4], axis=1, keepdims=True)) + lambda_init)
    outs = []
    for h in range(C_HEADS):
        r0 = h * VT_HEAD_ROWS
        l_row = slice(r0 + VT_HEAD_DIM, r0 + VT_HEAD_DIM + 1)
        a1 = acc_s[0, r0:r0 + VT_HEAD_DIM, :] * (1.0 / acc_s[0, l_row, :])
        a2 = acc_s[1, r0:r0 + VT_HEAD_DIM, :] * (1.0 / acc_s[1, l_row, :])
        out = a1 - lam * a2
        rs = lax.rsqrt(jnp.mean(out * out, axis=0, keepdims=True) + EPS)
        outs.append(out * rs * sg_ref[...] * (1.0 - lambda_init))
    o_ref[0] = jnp.concatenate(outs, axis=0).T.astype(BF16)


def _diff(qt, vt, pn, lam_p, sg_col, tq, lambda_init):
    bsz, seq, _ = pn.shape
    grid = (bsz, seq // tq)
    n_pairs = 2 * C_HEADS
    return pl.pallas_call(
        functools.partial(_diff_kernel, tq=tq, lambda_init=lambda_init),
        grid=grid,
        in_specs=[
            pl.BlockSpec((1, WIDTH, tq), lambda b, i: (b, 2, i)),
            pl.BlockSpec((1, seq, WIDTH), lambda b, i: (b, 0, 3)),
            pl.BlockSpec((1, 1, VT_ROWS, seq), lambda b, i: (b, 1, 0, 0)),
            pl.BlockSpec(lam_p.shape, lambda b, i: (0, 0)),
            pl.BlockSpec(sg_col.shape, lambda b, i: (0, 0)),
        ],
        out_specs=pl.BlockSpec((1, tq, WIDTH), lambda b, i: (b, i, 0)),
        out_shape=jax.ShapeDtypeStruct((bsz, seq, WIDTH), BF16),
        scratch_shapes=[pltpu.VMEM((2, WIDTH, (n_pairs // 2) * tq), BF16),
                        pltpu.VMEM((2, VT_ROWS, tq), F32),
                        pltpu.VMEM((SUBLANES, n_pairs * tq), F32),
                        pltpu.VMEM((2, tq, (n_pairs // 2) * tq), F32),
                        pltpu.VMEM((2, tq, (n_pairs // 2) * tq), F32),
                        pltpu.VMEM((SUBLANES, n_pairs * tq), F32)],
        compiler_params=pltpu.CompilerParams(
            dimension_semantics=("parallel", "arbitrary"), vmem_limit_bytes=VMEM_LIMIT_BYTES),
        name="diff",
    )(qt, pn, vt, lam_p, sg_col)


def _gelu_tanh(x):
    return x * (0.5 * (1.0 + jnp.tanh(math.sqrt(2.0 / math.pi) * (x + 0.044715 * (x * x * x)))))


def _merge_kernel(x_ref, ya_ref, bu_ref, bv_ref, yc_ref, gt_ref, lg_ref, lb_ref, ws_ref, bs_ref,
                  wa_ref, wb_ref, wc_ref, wo_ref, o_ref, yb_s, *, tm):
    d = x_ref.shape[-1]
    lane_w = lax.broadcasted_iota(I32, (CHUNK, B_WIDTH), 1)
    group_of_lane = lane_w // (B_WIDTH // B_GROUPS)
    r_i = lax.broadcasted_iota(I32, (CHUNK, CHUNK), 0)
    c_i = lax.broadcasted_iota(I32, (CHUNK, CHUNK), 1)
    w_tril = [jnp.where(c_i <= r_i, ws_ref[g], 0.0).astype(BF16) for g in range(B_GROUPS)]
    for c in range(tm // CHUNK):
        rows = slice(c * CHUNK, (c + 1) * CHUNK)
        u = _gelu_tanh(bu_ref[0, rows, :].astype(F32))
        v = _gelu_tanh(bv_ref[0, rows, :].astype(F32))
        mu = jnp.mean(v, axis=-1, keepdims=True)
        vc = v - mu
        var = jnp.mean(vc * vc, axis=-1, keepdims=True)
        vn = (vc * lax.rsqrt(var + EPS) * lg_ref[...] + lb_ref[...]).astype(BF16)
        s = bs_ref[...]
        for g in range(B_GROUPS):
            s = s + _dot(w_tril[g], jnp.where(group_of_lane == g, vn, jnp.zeros_like(vn)))
        yb_s[rows, :] = (u * s).astype(BF16)

    merged = None
    branches = ((ya_ref[0], wa_ref), (yb_s[...], wb_ref), (yc_ref[0], wc_ref))
    for n, (y, w_ref) in enumerate(branches):
        y_half = _dot(y * 0.5, w_ref[...])
        term = y_half + y_half * jnp.tanh(gt_ref[0, :, n * d:(n + 1) * d].astype(F32))
        merged = term if merged is None else merged + term
    o_ref[0] = x_ref[0] + _dot(merged.astype(BF16), wo_ref[...])


def _merge(x, pn, ya, yc, gates, lg, lb, ws, bs_full, wa, wb, wc, wo, tm):
    bsz, seq, d = x.shape
    grid = (bsz, seq // tm)
    const2 = lambda b, i: (0, 0)
    return pl.pallas_call(
        functools.partial(_merge_kernel, tm=tm),
        grid=grid,
        in_specs=[
            pl.BlockSpec((1, tm, d), lambda b, i: (b, i, 0)),
            pl.BlockSpec((1, tm, WIDTH), lambda b, i: (b, i, 0)),
            pl.BlockSpec((1, tm, WIDTH), lambda b, i: (b, i, 1)),
            pl.BlockSpec((1, tm, WIDTH), lambda b, i: (b, i, 2)),
            pl.BlockSpec((1, tm, WIDTH), lambda b, i: (b, i, 0)),
            pl.BlockSpec((1, tm, 3 * d), lambda b, i: (b, i, 0)),
            pl.BlockSpec(lg.shape, const2),
            pl.BlockSpec(lb.shape, const2),
            pl.BlockSpec(ws.shape, lambda b, i: (0, 0, 0)),
            pl.BlockSpec(bs_full.shape, const2),
            pl.BlockSpec(wa.shape, const2),
            pl.BlockSpec(wb.shape, const2),
            pl.BlockSpec(wc.shape, const2),
            pl.BlockSpec(wo.shape, const2),
        ],
        out_specs=pl.BlockSpec((1, tm, d), lambda b, i: (b, i, 0)),
        out_shape=jax.ShapeDtypeStruct((bsz, seq, d), F32),
        scratch_shapes=[pltpu.VMEM((tm, B_WIDTH), BF16)],
        compiler_params=pltpu.CompilerParams(
            dimension_semantics=("parallel", "parallel"), vmem_limit_bytes=VMEM_LIMIT_BYTES),
        name="merge",
    )(x, ya, pn, pn, yc, gates, lg, lb, ws, bs_full, wa, wb, wc, wo)


def _ffn_kernel(x_ref, g_ref, w1_ref, w2_ref, fg_ref, o_ref, h_s, acc_s, *, final_norm):
    j = pl.program_id(1)

    @pl.when(j == 0)
    def _():
        x = x_ref[...]
        ms = jnp.mean(x * x, axis=-1, keepdims=True)
        h_s[...] = (x * lax.rsqrt(ms + EPS) * g_ref[...]).astype(BF16)
        acc_s[...] = jnp.zeros_like(acc_s)

    a = jnp.maximum(_dot(h_s[...], w1_ref[...]), 0.0)
    acc_s[...] += _dot((a * a).astype(BF16), w2_ref[...])

    @pl.when(j == pl.num_programs(1) - 1)
    def _():
        y = x_ref[...] + acc_s[...]
        if final_norm:
            ms = jnp.mean(y * y, axis=-1, keepdims=True)
            y = y * lax.rsqrt(ms + EPS) * fg_ref[...]
        o_ref[...] = y


def _ffn(x2d, g, w1, w2, fg, tm, tf, final_norm):
    m, d = x2d.shape
    dff = w1.shape[1]
    grid = (m // tm, dff // tf)
    return pl.pallas_call(
        functools.partial(_ffn_kernel, final_norm=final_norm),
        grid=grid,
        in_specs=[
            pl.BlockSpec((tm, d), lambda i, j: (i, 0)),
            pl.BlockSpec((1, d), lambda i, j: (0, 0)),
            pl.BlockSpec((d, tf), lambda i, j: (0, j)),
            pl.BlockSpec((tf, d), lambda i, j: (j, 0)),
            pl.BlockSpec((1, d), lambda i, j: (0, 0)),
        ],
        out_specs=pl.BlockSpec((tm, d), lambda i, j: (i, 0)),
        out_shape=jax.ShapeDtypeStruct((m, d), F32),
        scratch_shapes=[pltpu.VMEM((tm, d), BF16), pltpu.VMEM((tm, d), F32)],
        compiler_params=pltpu.CompilerParams(
            dimension_semantics=("parallel", "arbitrary"), vmem_limit_bytes=VMEM_LIMIT_BYTES),
        name="ffn",
    )(x2d, g, w1, w2, fg)


def _tile(n, pref):
    t = min(n, pref)
    assert n % t == 0, (n, t)
    return t


def kernel(x, attn_norm_g, w_in, idx_k_norm_g, idx_k_norm_b, sgu_norm_g, sgu_norm_b, sgu_w_s, sgu_b_s,
           diff_lambda, diff_subln_g, w_branch_a, w_branch_b, w_branch_c, w_out, mlp_norm_g, w_ff1,
           w_ff2, final_norm_g):
    bsz, seq, d = x.shape
    depth = w_in.shape[0]
    offs = [0]
    for s in IN_SIZES:
        offs.append(offs[-1] + s)
    (o_aq, o_ak, o_av, o_iq, o_ik, o_iw, o_buv, o_cq, o_ck, o_cv, o_g, o_end) = offs

    tm_proj = _tile(seq, PROJ_ROWS)
    tq = _tile(seq, DSA_QUERIES)
    tq_diff = _tile(seq, DIFF_QUERIES)
    tm_merge = _tile(seq, MERGE_ROWS)
    tm_ffn = _tile(bsz * seq, FFN_ROWS)
    tf = _tile(w_ff1.shape[2], FFN_HIDDEN_COLS)

    for l in range(depth):
        lambda_init = 0.8 - 0.6 * math.exp(-0.3 * l)
        w = w_in[l]
        cols = lambda a, b: w[:, a:b]
        wn = jnp.concatenate([cols(o_ak, o_av), cols(o_buv, o_cq), cols(o_ck, o_cv)], axis=1).astype(BF16)
        wt = jnp.concatenate([cols(o_aq, o_ak), cols(o_iq, o_ik), cols(o_cq, o_ck), cols(o_av, o_iq),
                              cols(o_cv, o_g), cols(o_iw, o_buv),
                              jnp.zeros((d, SUBLANES - IDX_HEADS), w.dtype)], axis=1).T.astype(BF16)
        wik = cols(o_ik, o_iw).astype(BF16)
        wg = cols(o_g, o_end).astype(BF16)

        pn, ikn, gates, qt, vt, iwt = _proj(
            x, attn_norm_g[l][None, :], wn, wik, wg, wt,
            idx_k_norm_g[l][None, :], idx_k_norm_b[l][None, :], tm_proj)

        ya = _dsa(qt, vt, pn, ikn, iwt, tq)
        sg_col = diff_subln_g[l][:, None]
        yc = _diff(qt, vt, pn, diff_lambda[l], sg_col, tq_diff, lambda_init)

        bs_full = jnp.repeat(sgu_b_s[l].T, B_WIDTH // B_GROUPS, axis=1)
        x = _merge(x, pn, ya, yc, gates, sgu_norm_g[l][None, :], sgu_norm_b[l][None, :], sgu_w_s[l],
                   bs_full, w_branch_a[l].astype(BF16), w_branch_b[l].astype(BF16),
                   w_branch_c[l].astype(BF16), w_out[l].astype(BF16), tm_merge)

        x = _ffn(x.reshape(bsz * seq, d), mlp_norm_g[l][None, :], w_ff1[l].astype(BF16),
                 w_ff2[l].astype(BF16), final_norm_g[None, :], tm_ffn, tf,
                 final_norm=(l == depth - 1)).reshape(bsz, seq, d)
    return x
```

```python
import functools
import math

import jax
import jax.numpy as jnp
from jax import lax
from jax.experimental import pallas as pl
from jax.experimental.pallas import tpu as pltpu

F32 = jnp.float32
BF16 = jnp.bfloat16
I32 = jnp.int32
I16 = jnp.int16

A_HEADS = 4
A_HEAD_DIM = 64
IDX_HEADS = 4
IDX_DIM = 64
TOPK_MAX = 256
B_GROUPS = 4
B_WIDTH = 256
CHUNK = 128
C_HEADS = 4
C_HEAD_DIM = 32
WIDTH = 256
EPS = 1e-6
NEG = -1e30
INT16_MIN = -(2 ** 15)
TIE_BIG = 16384.0
IN_SIZES = (256, 256, 256, 256, 64, 4, 512, 256, 256, 256, 3072)

SUBLANES = 8
PACK16_ROWS = 2 * SUBLANES
VMEM_LIMIT_BYTES = 56 * 1024 * 1024

LOG2E = math.log2(math.e)
VT_HEAD_DIM = 64
VT_HEAD_ROWS = VT_HEAD_DIM + PACK16_ROWS
VT_ROWS = (WIDTH // VT_HEAD_DIM) * VT_HEAD_ROWS

PROJ_ROWS = 512
PROJ_GATE_COLS = 768
DSA_QUERIES = 256
DIFF_QUERIES = 512
DIFF_BLOCK_ROWS = 512
MERGE_ROWS = 512
FFN_ROWS = 1024
FFN_HIDDEN_COLS = 1024
N_COUNT_ACC = 4
CHUNKS_PER_TRIP = 4


def _nt_dot(a, b):
    return lax.dot_general(a, b, (((1,), (1,)), ((), ())), preferred_element_type=F32)


def _dot(a, b):
    return jnp.dot(a, b, preferred_element_type=F32)


def _for_chunks(n, body):
    def group(j, carry):
        for u in range(CHUNKS_PER_TRIP):
            body(CHUNKS_PER_TRIP * j + u)
        return carry

    lax.fori_loop(0, n // CHUNKS_PER_TRIP, group, 0)

    def single(c, carry):
        body(c)
        return carry

    lax.fori_loop(CHUNKS_PER_TRIP * (n // CHUNKS_PER_TRIP), n, single, 0)


def _proj_kernel(x_ref, g_ref, wn_ref, wik_ref, wg_ref, wt_ref, ikg_ref, ikb_ref,
                 pn_ref, ikn_ref, gt_ref, qt_ref, vt_ref, iwt_ref, *, gate_chunk):
    x = x_ref[0]
    tm = x.shape[0]
    ms = jnp.mean(x * x, axis=-1, keepdims=True)
    h = (x * lax.rsqrt(ms + EPS) * g_ref[...]).astype(BF16)
    for c0 in range(0, wn_ref.shape[1], WIDTH):
        pn_ref[0, :, c0:c0 + WIDTH] = _dot(h, wn_ref[:, c0:c0 + WIDTH]).astype(BF16)
    for c0 in range(0, wg_ref.shape[1], gate_chunk):
        gt_ref[0, :, c0:c0 + gate_chunk] = (_dot(h, wg_ref[:, c0:c0 + gate_chunk]) * 0.5).astype(BF16)
    for blk, q_scale in enumerate((A_HEAD_DIM ** -0.5 * LOG2E, None, C_HEAD_DIM ** -0.5 * LOG2E)):
        r = _nt_dot(wt_ref[blk * WIDTH:(blk + 1) * WIDTH, :], h)
        if q_scale is not None:
            r = r * q_scale
        qt_ref[0, blk * WIDTH:(blk + 1) * WIDTH, :] = r.astype(BF16)
    for blk in range(2):
        r0 = (3 + blk) * WIDTH
        r = _nt_dot(wt_ref[r0:r0 + WIDTH, :], h).astype(BF16)
        for hd in range(WIDTH // VT_HEAD_DIM):
            o0 = hd * VT_HEAD_ROWS
            vt_ref[0, blk, o0:o0 + VT_HEAD_DIM, :] = r[hd * VT_HEAD_DIM:(hd + 1) * VT_HEAD_DIM, :]
            vt_ref[0, blk, o0 + VT_HEAD_DIM:o0 + VT_HEAD_ROWS, :] = jnp.ones(
                (VT_HEAD_ROWS - VT_HEAD_DIM, tm), BF16)
    iwt_ref[0] = _nt_dot(wt_ref[5 * WIDTH:5 * WIDTH + SUBLANES, :], h)
    ik = _dot(h, wik_ref[...])
    mu = jnp.mean(ik, axis=-1, keepdims=True)
    xc = ik - mu
    var = jnp.mean(xc * xc, axis=-1, keepdims=True)
    ikn_ref[0] = (xc * lax.rsqrt(var + EPS) * ikg_ref[...] + ikb_ref[...]).astype(BF16)


def _proj(x, g, wn, wik, wg, wt, ikg, ikb, tm):
    bsz, seq, d = x.shape
    n_nat, n_gate = wn.shape[1], wg.shape[1]
    assert wt.shape[0] == 5 * WIDTH + SUBLANES
    grid = (bsz, seq // tm)
    const = lambda b, i: (0, 0)
    return pl.pallas_call(
        functools.partial(_proj_kernel, gate_chunk=min(n_gate, PROJ_GATE_COLS)),
        grid=grid,
        in_specs=[
            pl.BlockSpec((1, tm, d), lambda b, i: (b, i, 0)),
            pl.BlockSpec((1, d), const),
            pl.BlockSpec(wn.shape, const),
            pl.BlockSpec(wik.shape, const),
            pl.BlockSpec(wg.shape, const),
            pl.BlockSpec(wt.shape, const),
            pl.BlockSpec(ikg.shape, const),
            pl.BlockSpec(ikb.shape, const),
        ],
        out_specs=[
            pl.BlockSpec((1, tm, n_nat), lambda b, i: (b, i, 0)),
            pl.BlockSpec((1, tm, IDX_DIM), lambda b, i: (b, i, 0)),
            pl.BlockSpec((1, tm, n_gate), lambda b, i: (b, i, 0)),
            pl.BlockSpec((1, 3 * WIDTH, tm), lambda b, i: (b, 0, i)),
            pl.BlockSpec((1, 2, VT_ROWS, tm), lambda b, i: (b, 0, 0, i)),
            pl.BlockSpec((1, SUBLANES, tm), lambda b, i: (b, 0, i)),
        ],
        out_shape=[
            jax.ShapeDtypeStruct((bsz, seq, n_nat), BF16),
            jax.ShapeDtypeStruct((bsz, seq, IDX_DIM), BF16),
            jax.ShapeDtypeStruct((bsz, seq, n_gate), BF16),
            jax.ShapeDtypeStruct((bsz, 3 * WIDTH, seq), BF16),
            jax.ShapeDtypeStruct((bsz, 2, VT_ROWS, seq), BF16),
            jax.ShapeDtypeStruct((bsz, SUBLANES, seq), F32),
        ],
        compiler_params=pltpu.CompilerParams(
            dimension_semantics=("parallel", "parallel"), vmem_limit_bytes=VMEM_LIMIT_BYTES),
        name="proj",
    )(x, g, wn, wik, wg, wt, ikg, ikb)


def _dsa_kernel(qt_ref, iqt_ref, iwt_ref, ikn_ref, k_ref, vt_ref, o_ref,
                hi_s, lo_s, bias_s, qm_s, acc_s, m_s, tri_s, s0_s, s1_s, lm_s, *, tq, top_k):
    i = pl.program_id(1)
    kc = tq
    n_chunks = i + 1
    n_grp = kc // PACK16_ROWS
    q_pos = i * tq + lax.broadcasted_iota(I32, (1, tq), 1)
    one16 = jnp.ones((PACK16_ROWS, tq), I16)
    zero16 = jnp.zeros((PACK16_ROWS, tq), I16)

    def rows16(v):
        return jnp.broadcast_to(v, (PACK16_ROWS, tq)).astype(I16)

    w_all = iwt_ref[0] * (IDX_DIM ** -0.5 * IDX_HEADS ** -0.5)
    w_rows = [w_all[h:h + 1, :] for h in range(IDX_HEADS)]
    for h in range(IDX_HEADS):
        qm_s[0:IDX_DIM, h * tq:(h + 1) * tq] = iqt_ref[0, h * IDX_DIM:(h + 1) * IDX_DIM, :]

    def score_body(c, diagonal):
        off = pl.multiple_of(c * kc, kc)
        ik = ikn_ref[0, pl.ds(off, kc), :]
        d = _dot(ik, qm_s[0:IDX_DIM, :])
        sc = jnp.zeros((kc, tq), F32)
        for h in range(IDX_HEADS):
            sc = sc + jnp.maximum(d[:, h * tq:(h + 1) * tq], 0.0) * w_rows[h]
        if diagonal:
            key_pos = off + lax.broadcasted_iota(I32, (kc, tq), 0)
            sc = jnp.where(key_pos <= q_pos, sc, -jnp.inf)
        bits = pltpu.bitcast(sc, I32)
        key = bits ^ ((bits >> 31) & 0x7FFFFFFF)
        hi_s[pl.ds(off, kc), :] = (key >> 16).astype(I16)
        lo_s[pl.ds(off, kc), :] = ((key & 0xFFFF) - 0x8000).astype(I16)

    _for_chunks(i, lambda c: score_body(c, False))
    score_body(i, True)

    def count_ge(ref, cand):
        cb = rows16(cand)

        def body(c, accs):
            off = pl.multiple_of(c * kc, kc)
            slab = ref[pl.ds(off, kc), :]
            accs = list(accs)
            for r in range(n_grp):
                kk = slab[r * PACK16_ROWS:(r + 1) * PACK16_ROWS, :]
                accs[r % N_COUNT_ACC] = accs[r % N_COUNT_ACC] + jnp.where(kk >= cb, one16, zero16)
            return tuple(accs)

        def group(j, accs):
            for u in range(CHUNKS_PER_TRIP):
                accs = body(CHUNKS_PER_TRIP * j + u, accs)
            return accs

        accs = lax.fori_loop(0, n_chunks // CHUNKS_PER_TRIP, group, (zero16,) * N_COUNT_ACC)
        accs = lax.fori_loop(CHUNKS_PER_TRIP * (n_chunks // CHUNKS_PER_TRIP), n_chunks, body, accs)
        tot = accs[0]
        for a in accs[1:]:
            tot = tot + a
        return jnp.sum(tot.astype(I32), axis=0, keepdims=True)

    def descend(ref, need):
        c0 = count_ge(ref, jnp.zeros((1, tq), I32))
        ok0 = c0 >= need
        thr = jnp.where(ok0, 0, INT16_MIN).astype(I32)
        c_gt = jnp.where(ok0, 0, c0)

        def bit_body(j, carry):
            thr, c_gt = carry
            cand = thr | (jnp.int32(1) << (14 - j))
            cnt = count_ge(ref, cand)
            ok = cnt >= need
            return jnp.where(ok, cand, thr), jnp.where(ok, c_gt, cnt)

        return lax.fori_loop(0, 15, bit_body, (thr, c_gt))

    thr_hi, c_gt_hi = descend(hi_s, top_k)
    thr_hi16 = rows16(thr_hi)
    need_lo = top_k - c_gt_hi

    def restrict_body(c, carry):
        off = pl.multiple_of(c * kc, kc)
        in_bucket = hi_s[pl.ds(off, kc), :] == jnp.concatenate([thr_hi16] * n_grp, axis=0)
        lo_s[pl.ds(off, kc), :] = jnp.where(in_bucket, lo_s[pl.ds(off, kc), :], jnp.int16(INT16_MIN))
        return carry

    lax.fori_loop(0, n_chunks, restrict_body, 0)
    thr_lo, c_gt_lo = descend(lo_s, need_lo)
    thr_lo16 = rows16(thr_lo)
    need_tie = (need_lo - c_gt_lo).astype(F32)

    r_i = lax.broadcasted_iota(I32, (kc, kc), 0)
    c_i = lax.broadcasted_iota(I32, (kc, kc), 1)
    tri_s[0:kc, 0:kc] = jnp.where(c_i <= r_i, 1.0, 0.0).astype(BF16)
    tri_s[0:kc, kc:2 * kc] = jnp.where(c_i == r_i, TIE_BIG, 0.0).astype(BF16)
    tri_s[kc:kc + PACK16_ROWS, 0:kc] = jnp.ones((PACK16_ROWS, kc), BF16)
    tri_s[kc:kc + PACK16_ROWS, kc:2 * kc] = jnp.zeros((PACK16_ROWS, kc), BF16)
    thr_hi_kc = jnp.concatenate([thr_hi16] * n_grp, axis=0)
    thr_lo_kc = jnp.concatenate([thr_lo16] * n_grp, axis=0)
    one_b, zero_b, neg_b = (jnp.full((kc, tq), v, BF16) for v in (1.0, 0.0, -1.0))

    def bias_body(c, seen, diagonal):
        off = pl.multiple_of(c * kc, kc)
        hh = hi_s[pl.ds(off, kc), :]
        ll = lo_s[pl.ds(off, kc), :]
        in_bucket = hh == thr_hi_kc
        lo_eq = ll == thr_lo_kc
        equal = jnp.where(in_bucket, jnp.where(lo_eq, one_b, zero_b), zero_b)
        side = jnp.where(hh > thr_hi_kc, neg_b,
                         jnp.where(in_bucket, jnp.where(ll > thr_lo_kc, neg_b, jnp.where(lo_eq, zero_b, one_b)),
                                   one_b))
        rank = _dot(tri_s[...], jnp.concatenate([equal, side], axis=0))
        b = jnp.where(rank[0:kc, :] <= need_tie - seen, 0.0, NEG)
        if diagonal:
            key_pos = off + lax.broadcasted_iota(I32, (kc, tq), 0)
            b = jnp.where(key_pos <= q_pos, b, NEG)
        bias_s[pl.ds(off, kc), :] = b
        return seen + rank[kc:kc + 1, :]

    def bias_group(j, seen):
        for u in range(CHUNKS_PER_TRIP):
            seen = bias_body(CHUNKS_PER_TRIP * j + u, seen, False)
        return seen

    seen = lax.fori_loop(0, i // CHUNKS_PER_TRIP, bias_group, jnp.zeros((1, tq), F32))
    seen = lax.fori_loop(CHUNKS_PER_TRIP * (i // CHUNKS_PER_TRIP), i,
                         lambda c, seen: bias_body(c, seen, False), seen)
    bias_body(i, seen, True)

    feat = lax.broadcasted_iota(I32, (WIDTH, tq), 0)
    qt = qt_ref[0]
    for h in range(A_HEADS):
        qm_s[:, h * tq:(h + 1) * tq] = jnp.where(feat // A_HEAD_DIM == h, qt, jnp.zeros_like(qt))
    acc_s[...] = jnp.zeros_like(acc_s)
    m_s[0:1, :] = jnp.full((1, A_HEADS * tq), NEG, F32)

    span = 2 * kc

    def logits_block(off):
        bias = bias_s[pl.ds(off, kc), :]
        return _dot(k_ref[0, pl.ds(off, kc), :], qm_s[...]) + jnp.concatenate([bias] * A_HEADS, axis=1)

    def pv_block(off, pb, h):
        rows = slice(h * VT_HEAD_ROWS, (h + 1) * VT_HEAD_ROWS)
        return _dot(vt_ref[0, 0, rows, pl.ds(off, kc)], pb[:, h * tq:(h + 1) * tq])

    def rescale_and_add(alpha, pv):
        for h in range(A_HEADS):
            rows = slice(h * VT_HEAD_ROWS, (h + 1) * VT_HEAD_ROWS)
            acc_s[rows, :] = acc_s[rows, :] * alpha[:, h * tq:(h + 1) * tq] + pv[h]

    def attend_chunk(off):
        s = logits_block(off)
        m_old = m_s[0:1, :]
        m_new = jnp.maximum(m_old, jnp.max(s, axis=0, keepdims=True))
        m_s[0:1, :] = m_new
        pb = jnp.exp2(s - m_new).astype(BF16)
        rescale_and_add(jnp.exp2(m_old - m_new), [pv_block(off, pb, h) for h in range(A_HEADS)])

    def logits_span(off, buf):
        s_ref = (s0_s, s1_s)[buf]
        lm = None
        for r in range(span // kc):
            s = logits_block(off + r * kc)
            s_ref[r * kc:(r + 1) * kc, :] = s
            lm_r = jnp.max(s, axis=0, keepdims=True)
            lm = lm_r if lm is None else jnp.maximum(lm, lm_r)
        lm_s[buf:buf + 1, :] = lm

    def step(off_next, off_cur, buf_next):
        buf_cur = 1 - buf_next
        s_next, s_cur = (s0_s, s1_s)[buf_next], (s0_s, s1_s)[buf_cur]
        m_old = m_s[0:1, :]
        m_new = jnp.maximum(m_old, lm_s[buf_cur:buf_cur + 1, :])
        m_s[0:1, :] = m_new
        lm, pv = None, [None] * A_HEADS
        for r in range(span // kc):
            blk = slice(r * kc, (r + 1) * kc)
            s = logits_block(off_next + r * kc)
            s_next[blk, :] = s
            lm_r = jnp.max(s, axis=0, keepdims=True)
            lm = lm_r if lm is None else jnp.maximum(lm, lm_r)
            pb = jnp.exp2(s_cur[blk, :] - m_new).astype(BF16)
            for h in range(A_HEADS):
                d = pv_block(off_cur + r * kc, pb, h)
                pv[h] = d if pv[h] is None else pv[h] + d
        lm_s[buf_next:buf_next + 1, :] = lm
        rescale_and_add(jnp.exp2(m_old - m_new), pv)

    n_full = n_chunks // 2

    @pl.when(n_chunks % 2 == 1)
    def _():
        attend_chunk(pl.multiple_of(i * kc, kc))

    def drain(off_cur, buf_cur):
        s_cur = (s0_s, s1_s)[buf_cur]
        m_old = m_s[0:1, :]
        m_new = jnp.maximum(m_old, lm_s[buf_cur:buf_cur + 1, :])
        m_s[0:1, :] = m_new
        pv = [None] * A_HEADS
        for r in range(span // kc):
            pb = jnp.exp2(s_cur[r * kc:(r + 1) * kc, :] - m_new).astype(BF16)
            for h in range(A_HEADS):
                d = pv_block(off_cur + r * kc, pb, h)
                pv[h] = d if pv[h] is None else pv[h] + d
        rescale_and_add(jnp.exp2(m_old - m_new), pv)

    @pl.when(n_full > 0)
    def _():
        logits_span(0, 0)
        n_steps = n_full - 1

        def body(jj, carry):
            off0 = pl.multiple_of(2 * jj * span, span)
            step(off0 + span, off0, 1)
            step(off0 + 2 * span, off0 + span, 0)
            return carry

        lax.fori_loop(0, n_steps // 2, body, 0)

        @pl.when(n_steps % 2 == 1)
        def _():
            off_cur = pl.multiple_of((n_steps - 1) * span, span)
            step(off_cur + span, off_cur, 1)

        off_last = pl.multiple_of((n_full - 1) * span, span)
        for parity in range(2):
            @pl.when((n_full - 1) % 2 == parity)
            def _():
                drain(off_last, parity)

    out = []
    for h in range(A_HEADS):
        r0 = h * VT_HEAD_ROWS
        out.append(acc_s[r0:r0 + VT_HEAD_DIM, :] * (1.0 / acc_s[r0 + VT_HEAD_DIM:r0 + VT_HEAD_DIM + 1, :]))
    o_ref[0] = jnp.concatenate(out, axis=0).T.astype(BF16)


def _dsa(qt, vt, pn, ikn, iwt, tq):
    bsz, seq, _ = pn.shape
    top_k = min(TOPK_MAX, seq // 4)
    grid = (bsz, seq // tq)
    return pl.pallas_call(
        functools.partial(_dsa_kernel, tq=tq, top_k=top_k),
        grid=grid,
        in_specs=[
            pl.BlockSpec((1, WIDTH, tq), lambda b, i: (b, 0, i)),
            pl.BlockSpec((1, WIDTH, tq), lambda b, i: (b, 1, i)),
            pl.BlockSpec((1, SUBLANES, tq), lambda b, i: (b, 0, i)),
            pl.BlockSpec((1, seq, IDX_DIM), lambda b, i: (b, 0, 0)),
            pl.BlockSpec((1, seq, WIDTH), lambda b, i: (b, 0, 0)),
            pl.BlockSpec((1, 1, VT_ROWS, seq), lambda b, i: (b, 0, 0, 0)),
        ],
        out_specs=pl.BlockSpec((1, tq, WIDTH), lambda b, i: (b, i, 0)),
        out_shape=jax.ShapeDtypeStruct((bsz, seq, WIDTH), BF16),
        scratch_shapes=[pltpu.VMEM((seq, tq), I16),
                        pltpu.VMEM((seq, tq), I16),
                        pltpu.VMEM((seq, tq), F32),
                        pltpu.VMEM((WIDTH, A_HEADS * tq), BF16),
                        pltpu.VMEM((VT_ROWS, tq), F32),
                        pltpu.VMEM((SUBLANES, A_HEADS * tq), F32),
                        pltpu.VMEM((tq + PACK16_ROWS, 2 * tq), BF16),
                        pltpu.VMEM((2 * tq, A_HEADS * tq), F32),
                        pltpu.VMEM((2 * tq, A_HEADS * tq), F32),
                        pltpu.VMEM((SUBLANES, A_HEADS * tq), F32)],
        compiler_params=pltpu.CompilerParams(
            dimension_semantics=("parallel", "arbitrary"), vmem_limit_bytes=VMEM_LIMIT_BYTES),
        name="dsa",
    )(qt, qt, iwt, ikn, pn, vt)


def _diff_kernel(qt_ref, k_ref, vt_ref, lam_ref, sg_ref, o_ref, qm_s, acc_s, m_s, s0_s, s1_s, lm_s,
                 *, tq, lambda_init):
    i = pl.program_id(1)
    span = tq
    blk = DIFF_BLOCK_ROWS
    n_blk = span // blk
    n_pairs = 2 * C_HEADS
    grp = n_pairs // 2
    feat = lax.broadcasted_iota(I32, (WIDTH, tq), 0)
    qt = qt_ref[0]
    for p_i in range(n_pairs):
        qm_s[p_i // grp, :, (p_i % grp) * tq:(p_i % grp + 1) * tq] = jnp.where(
            feat // C_HEAD_DIM == p_i, qt, jnp.zeros_like(qt))
    acc_s[...] = jnp.zeros_like(acc_s)
    m_s[0:1, :] = jnp.full((1, n_pairs * tq), NEG, F32)
    bufs = (s0_s, s1_s)
    pair_rows = [slice((p_i // 2) * VT_HEAD_ROWS, (p_i // 2 + 1) * VT_HEAD_ROWS) for p_i in range(n_pairs)]

    def logits_block(off, r, g, masked):
        s = _dot(k_ref[0, pl.ds(off + r * blk, blk), :], qm_s[g])
        if masked:
            row = r * blk + lax.broadcasted_iota(I32, (blk, tq), 0)
            ok = row <= lax.broadcasted_iota(I32, (blk, tq), 1)
            s = jnp.where(jnp.concatenate([ok] * grp, axis=1), s, NEG)
        return s

    def logits(off, buf, masked):
        for g in range(2):
            lm = None
            for r in range(n_blk):
                s = logits_block(off, r, g, masked)
                bufs[buf][g, r * blk:(r + 1) * blk, :] = s
                lm_r = jnp.max(s, axis=0, keepdims=True)
                lm = lm_r if lm is None else jnp.maximum(lm, lm_r)
            lm_s[buf:buf + 1, g * grp * tq:(g + 1) * grp * tq] = lm

    def new_max(buf_cur):
        m_new, alpha = [], []
        for g in range(2):
            st = slice(g * grp * tq, (g + 1) * grp * tq)
            m_old = m_s[0:1, st]
            m_new.append(jnp.maximum(m_old, lm_s[buf_cur:buf_cur + 1, st]))
            alpha.append(jnp.exp2(m_old - m_new[g]))
            m_s[0:1, st] = m_new[g]
        return m_new, alpha

    def pv_block(off_cur, r, g, pb, pv):
        for j in range(grp):
            p_i = g * grp + j
            d = _dot(vt_ref[0, 0, pair_rows[p_i], pl.ds(off_cur + r * blk, blk)], pb[:, j * tq:(j + 1) * tq])
            pv[p_i] = d if pv[p_i] is None else pv[p_i] + d

    def rescale_and_add(alpha, pv):
        for p_i in range(n_pairs):
            a = alpha[p_i // grp][:, (p_i % grp) * tq:(p_i % grp + 1) * tq]
            acc_s[p_i % 2, pair_rows[p_i], :] = acc_s[p_i % 2, pair_rows[p_i], :] * a + pv[p_i]

    def accumulate(off_cur, buf_cur):
        m_new, alpha = new_max(buf_cur)
        pv = [None] * n_pairs
        for r in range(n_blk):
            for g in range(2):
                pb = jnp.exp2(bufs[buf_cur][g, r * blk:(r + 1) * blk, :] - m_new[g]).astype(BF16)
                pv_block(off_cur, r, g, pb, pv)
        rescale_and_add(alpha, pv)

    def step(off_next, off_cur, buf_next):
        buf_cur = 1 - buf_next
        m_new, alpha = new_max(buf_cur)
        lm, pv = [None, None], [None] * n_pairs
        for r in range(n_blk):
            for g in range(2):
                s = logits_block(off_next, r, g, False)
                bufs[buf_next][g, r * blk:(r + 1) * blk, :] = s
                lm_r = jnp.max(s, axis=0, keepdims=True)
                lm[g] = lm_r if lm[g] is None else jnp.maximum(lm[g], lm_r)
                pb = jnp.exp2(bufs[buf_cur][g, r * blk:(r + 1) * blk, :] - m_new[g]).astype(BF16)
                pv_block(off_cur, r, g, pb, pv)
        for g in range(2):
            lm_s[buf_next:buf_next + 1, g * grp * tq:(g + 1) * grp * tq] = lm[g]
        rescale_and_add(alpha, pv)

    logits(pl.multiple_of(i * span, span), 0, True)

    def item_off(t):
        return pl.multiple_of(jnp.where(t == 0, i, t - 1) * span, span)

    def body(jj, carry):
        step(item_off(2 * jj + 1), item_off(2 * jj), 1)
        step(item_off(2 * jj + 2), item_off(2 * jj + 1), 0)
        return carry

    lax.fori_loop(0, i // 2, body, 0)

    @pl.when(i % 2 == 1)
    def _():
        step(item_off(i), item_off(i - 1), 1)

    for parity in range(2):
        @pl.when(i % 2 == parity)
        def _():
            accumulate(item_off(i), parity)

    lq = lam_ref[...]
    lam = (jnp.exp(jnp.sum(lq[0:1] * lq[1:2], axis=1, keepdims=True))
           - jnp.exp(jnp.sum(lq[2:3] * lq[3:4], axis=1, keepdims=True)) + lambda_init)
    outs = []
    for h in range(C_HEADS):
        r0 = h * VT_HEAD_ROWS
        l_row = slice(r0 + VT_HEAD_DIM, r0 + VT_HEAD_DIM + 1)
        a1 = acc_s[0, r0:r0 + VT_HEAD_DIM, :] * (1.0 / acc_s[0, l_row, :])
        a2 = acc_s[1, r0:r0 + VT_HEAD_DIM, :] * (1.0 / acc_s[1, l_row, :])
        out = a1 - lam * a2
        rs = lax.rsqrt(jnp.mean(out * out, axis=0, keepdims=True) + EPS)
        outs.append(out * rs * sg_ref[...] * (1.0 - lambda_init))
    o_ref[0] = jnp.concatenate(outs, axis=0).T.astype(BF16)


def _diff(qt, vt, pn, lam_p, sg_col, tq, lambda_init):
    bsz, seq, _ = pn.shape
    grid = (bsz, seq // tq)
    n_pairs = 2 * C_HEADS
    return pl.pallas_call(
        functools.partial(_diff_kernel, tq=tq, lambda_init=lambda_init),
        grid=grid,
        in_specs=[
            pl.BlockSpec((1, WIDTH, tq), lambda b, i: (b, 2, i)),
            pl.BlockSpec((1, seq, WIDTH), lambda b, i: (b, 0, 3)),
            pl.BlockSpec((1, 1, VT_ROWS, seq), lambda b, i: (b, 1, 0, 0)),
            pl.BlockSpec(lam_p.shape, lambda b, i: (0, 0)),
            pl.BlockSpec(sg_col.shape, lambda b, i: (0, 0)),
        ],
        out_specs=pl.BlockSpec((1, tq, WIDTH), lambda b, i: (b, i, 0)),
        out_shape=jax.ShapeDtypeStruct((bsz, seq, WIDTH), BF16),
        scratch_shapes=[pltpu.VMEM((2, WIDTH, (n_pairs // 2) * tq), BF16),
                        pltpu.VMEM((2, VT_ROWS, tq), F32),
                        pltpu.VMEM((SUBLANES, n_pairs * tq), F32),
                        pltpu.VMEM((2, tq, (n_pairs // 2) * tq), F32),
                        pltpu.VMEM((2, tq, (n_pairs // 2) * tq), F32),
                        pltpu.VMEM((SUBLANES, n_pairs * tq), F32)],
        compiler_params=pltpu.CompilerParams(
            dimension_semantics=("parallel", "arbitrary"), vmem_limit_bytes=VMEM_LIMIT_BYTES),
        name="diff",
    )(qt, pn, vt, lam_p, sg_col)


def _gelu_tanh(x):
    return x * (0.5 * (1.0 + jnp.tanh(math.sqrt(2.0 / math.pi) * (x + 0.044715 * (x * x * x)))))


def _merge_kernel(x_ref, ya_ref, bu_ref, bv_ref, yc_ref, gt_ref, lg_ref, lb_ref, ws_ref, bs_ref,
                  wa_ref, wb_ref, wc_ref, wo_ref, o_ref, yb_s, *, tm):
    d = x_ref.shape[-1]
    lane_w = lax.broadcasted_iota(I32, (CHUNK, B_WIDTH), 1)
    group_of_lane = lane_w // (B_WIDTH // B_GROUPS)
    r_i = lax.broadcasted_iota(I32, (CHUNK, CHUNK), 0)
    c_i = lax.broadcasted_iota(I32, (CHUNK, CHUNK), 1)
    w_tril = [jnp.where(c_i <= r_i, ws_ref[g], 0.0).astype(BF16) for g in range(B_GROUPS)]
    for c in range(tm // CHUNK):
        rows = slice(c * CHUNK, (c + 1) * CHUNK)
        u = _gelu_tanh(bu_ref[0, rows, :].astype(F32))
        v = _gelu_tanh(bv_ref[0, rows, :].astype(F32))
        mu = jnp.mean(v, axis=-1, keepdims=True)
        vc = v - mu
        var = jnp.mean(vc * vc, axis=-1, keepdims=True)
        vn = (vc * lax.rsqrt(var + EPS) * lg_ref[...] + lb_ref[...]).astype(BF16)
        s = bs_ref[...]
        for g in range(B_GROUPS):
            s = s + _dot(w_tril[g], jnp.where(group_of_lane == g, vn, jnp.zeros_like(vn)))
        yb_s[rows, :] = (u * s).astype(BF16)

    merged = None
    branches = ((ya_ref[0], wa_ref), (yb_s[...], wb_ref), (yc_ref[0], wc_ref))
    for n, (y, w_ref) in enumerate(branches):
        y_half = _dot(y * 0.5, w_ref[...])
        term = y_half + y_half * jnp.tanh(gt_ref[0, :, n * d:(n + 1) * d].astype(F32))
        merged = term if merged is None else merged + term
    o_ref[0] = x_ref[0] + _dot(merged.astype(BF16), wo_ref[...])


def _merge(x, pn, ya, yc, gates, lg, lb, ws, bs_full, wa, wb, wc, wo, tm):
    bsz, seq, d = x.shape
    grid = (bsz, seq // tm)
    const2 = lambda b, i: (0, 0)
    return pl.pallas_call(
        functools.partial(_merge_kernel, tm=tm),
        grid=grid,
        in_specs=[
            pl.BlockSpec((1, tm, d), lambda b, i: (b, i, 0)),
            pl.BlockSpec((1, tm, WIDTH), lambda b, i: (b, i, 0)),
            pl.BlockSpec((1, tm, WIDTH), lambda b, i: (b, i, 1)),
            pl.BlockSpec((1, tm, WIDTH), lambda b, i: (b, i, 2)),
            pl.BlockSpec((1, tm, WIDTH), lambda b, i: (b, i, 0)),
            pl.BlockSpec((1, tm, 3 * d), lambda b, i: (b, i, 0)),
            pl.BlockSpec(lg.shape, const2),
            pl.BlockSpec(lb.shape, const2),
            pl.BlockSpec(ws.shape, lambda b, i: (0, 0, 0)),
            pl.BlockSpec(bs_full.shape, const2),
            pl.BlockSpec(wa.shape, const2),
            pl.BlockSpec(wb.shape, const2),
            pl.BlockSpec(wc.shape, const2),
            pl.BlockSpec(wo.shape, const2),
        ],
        out_specs=pl.BlockSpec((1, tm, d), lambda b, i: (b, i, 0)),
        out_shape=jax.ShapeDtypeStruct((bsz, seq, d), F32),
        scratch_shapes=[pltpu.VMEM((tm, B_WIDTH), BF16)],
        compiler_params=pltpu.CompilerParams(
            dimension_semantics=("parallel", "parallel"), vmem_limit_bytes=VMEM_LIMIT_BYTES),
        name="merge",
    )(x, ya, pn, pn, yc, gates, lg, lb, ws, bs_full, wa, wb, wc, wo)


def _ffn_kernel(x_ref, g_ref, w1_ref, w2_ref, fg_ref, o_ref, h_s, acc_s, *, final_norm):
    j = pl.program_id(1)

    @pl.when(j == 0)
    def _():
        x = x_ref[...]
        ms = jnp.mean(x * x, axis=-1, keepdims=True)
        h_s[...] = (x * lax.rsqrt(ms + EPS) * g_ref[...]).astype(BF16)
        acc_s[...] = jnp.zeros_like(acc_s)

    a = jnp.maximum(_dot(h_s[...], w1_ref[...]), 0.0)
    acc_s[...] += _dot((a * a).astype(BF16), w2_ref[...])

    @pl.when(j == pl.num_programs(1) - 1)
    def _():
        y = x_ref[...] + acc_s[...]
        if final_norm:
            ms = jnp.mean(y * y, axis=-1, keepdims=True)
            y = y * lax.rsqrt(ms + EPS) * fg_ref[...]
        o_ref[...] = y


def _ffn(x2d, g, w1, w2, fg, tm, tf, final_norm):
    m, d = x2d.shape
    dff = w1.shape[1]
    grid = (m // tm, dff // tf)
    return pl.pallas_call(
        functools.partial(_ffn_kernel, final_norm=final_norm),
        grid=grid,
        in_specs=[
            pl.BlockSpec((tm, d), lambda i, j: (i, 0)),
            pl.BlockSpec((1, d), lambda i, j: (0, 0)),
            pl.BlockSpec((d, tf), lambda i, j: (0, j)),
            pl.BlockSpec((tf, d), lambda i, j: (j, 0)),
            pl.BlockSpec((1, d), lambda i, j: (0, 0)),
        ],
        out_specs=pl.BlockSpec((tm, d), lambda i, j: (i, 0)),
        out_shape=jax.ShapeDtypeStruct((m, d), F32),
        scratch_shapes=[pltpu.VMEM((tm, d), BF16), pltpu.VMEM((tm, d), F32)],
        compiler_params=pltpu.CompilerParams(
            dimension_semantics=("parallel", "arbitrary"), vmem_limit_bytes=VMEM_LIMIT_BYTES),
        name="ffn",
    )(x2d, g, w1, w2, fg)


def _tile(n, pref):
    t = min(n, pref)
    assert n % t == 0, (n, t)
    return t


def kernel(x, attn_norm_g, w_in, idx_k_norm_g, idx_k_norm_b, sgu_norm_g, sgu_norm_b, sgu_w_s, sgu_b_s,
           diff_lambda, diff_subln_g, w_branch_a, w_branch_b, w_branch_c, w_out, mlp_norm_g, w_ff1,
           w_ff2, final_norm_g):
    bsz, seq, d = x.shape
    depth = w_in.shape[0]
    offs = [0]
    for s in IN_SIZES:
        offs.append(offs[-1] + s)
    (o_aq, o_ak, o_av, o_iq, o_ik, o_iw, o_buv, o_cq, o_ck, o_cv, o_g, o_end) = offs

    tm_proj = _tile(seq, PROJ_ROWS)
    tq = _tile(seq, DSA_QUERIES)
    tq_diff = _tile(seq, DIFF_QUERIES)
    tm_merge = _tile(seq, MERGE_ROWS)
    tm_ffn = _tile(bsz * seq, FFN_ROWS)
    tf = _tile(w_ff1.shape[2], FFN_HIDDEN_COLS)

    for l in range(depth):
        lambda_init = 0.8 - 0.6 * math.exp(-0.3 * l)
        w = w_in[l]
        cols = lambda a, b: w[:, a:b]
        wn = jnp.concatenate([cols(o_ak, o_av), cols(o_buv, o_cq), cols(o_ck, o_cv)], axis=1).astype(BF16)
        wt = jnp.concatenate([cols(o_aq, o_ak), cols(o_iq, o_ik), cols(o_cq, o_ck), cols(o_av, o_iq),
                              cols(o_cv, o_g), cols(o_iw, o_buv),
                              jnp.zeros((d, SUBLANES - IDX_HEADS), w.dtype)], axis=1).T.astype(BF16)
        wik = cols(o_ik, o_iw).astype(BF16)
        wg = cols(o_g, o_end).astype(BF16)

        pn, ikn, gates, qt, vt, iwt = _proj(
            x, attn_norm_g[l][None, :], wn, wik, wg, wt,
            idx_k_norm_g[l][None, :], idx_k_norm_b[l][None, :], tm_proj)

        ya = _dsa(qt, vt, pn, ikn, iwt, tq)
        sg_col = diff_subln_g[l][:, None]
        yc = _diff(qt, vt, pn, diff_lambda[l], sg_col, tq_diff, lambda_init)

        bs_full = jnp.repeat(sgu_b_s[l].T, B_WIDTH // B_GROUPS, axis=1)
        x = _merge(x, pn, ya, yc, gates, sgu_norm_g[l][None, :], sgu_norm_b[l][None, :], sgu_w_s[l],
                   bs_full, w_branch_a[l].astype(BF16), w_branch_b[l].astype(BF16),
                   w_branch_c[l].astype(BF16), w_out[l].astype(BF16), tm_merge)

        x = _ffn(x.reshape(bsz * seq, d), mlp_norm_g[l][None, :], w_ff1[l].astype(BF16),
                 w_ff2[l].astype(BF16), final_norm_g[None, :], tm_ffn, tf,
                 final_norm=(l == depth - 1)).reshape(bsz, seq, d)
    return x
```

```python
import functools
import math

import jax
import jax.numpy as jnp
from jax import lax
from jax.experimental import pallas as pl
from jax.experimental.pallas import tpu as pltpu

F32 = jnp.float32
BF16 = jnp.bfloat16
I32 = jnp.int32
I16 = jnp.int16

A_HEADS = 4
A_HEAD_DIM = 64
IDX_HEADS = 4
IDX_DIM = 64
TOPK_MAX = 256
B_GROUPS = 4
B_WIDTH = 256
CHUNK = 128
C_HEADS = 4
C_HEAD_DIM = 32
WIDTH = 256
EPS = 1e-6
NEG = -1e30
INT16_MIN = -(2 ** 15)
TIE_BIG = 16384.0
IN_SIZES = (256, 256, 256, 256, 64, 4, 512, 256, 256, 256, 3072)

SUBLANES = 8
PACK16_ROWS = 2 * SUBLANES
VMEM_LIMIT_BYTES = 56 * 1024 * 1024

LOG2E = math.log2(math.e)
VT_HEAD_DIM = 64
VT_HEAD_ROWS = VT_HEAD_DIM + PACK16_ROWS
VT_ROWS = (WIDTH // VT_HEAD_DIM) * VT_HEAD_ROWS

PROJ_ROWS = 1024
PROJ_GATE_COLS = 768
DSA_QUERIES = 256
DIFF_QUERIES = 512
DIFF_BLOCK_ROWS = 512
MERGE_ROWS = 512
FFN_ROWS = 1024
FFN_HIDDEN_COLS = 1024
N_COUNT_ACC = 4
CHUNKS_PER_TRIP = 4


def _nt_dot(a, b):
    return lax.dot_general(a, b, (((1,), (1,)), ((), ())), preferred_element_type=F32)


def _dot(a, b):
    return jnp.dot(a, b, preferred_element_type=F32)


def _for_chunks(n, body):
    def group(j, carry):
        for u in range(CHUNKS_PER_TRIP):
            body(CHUNKS_PER_TRIP * j + u)
        return carry

    lax.fori_loop(0, n // CHUNKS_PER_TRIP, group, 0)

    def single(c, carry):
        body(c)
        return carry

    lax.fori_loop(CHUNKS_PER_TRIP * (n // CHUNKS_PER_TRIP), n, single, 0)


def _proj_kernel(x_ref, g_ref, wn_ref, wik_ref, wg_ref, wt_ref, ikg_ref, ikb_ref,
                 pn_ref, ikn_ref, gt_ref, qt_ref, vt_ref, iwt_ref, *, gate_chunk):
    x = x_ref[0]
    tm = x.shape[0]
    ms = jnp.mean(x * x, axis=-1, keepdims=True)
    h = (x * lax.rsqrt(ms + EPS) * g_ref[...]).astype(BF16)
    for c0 in range(0, wn_ref.shape[1], WIDTH):
        pn_ref[0, :, c0:c0 + WIDTH] = _dot(h, wn_ref[:, c0:c0 + WIDTH]).astype(BF16)
    for c0 in range(0, wg_ref.shape[1], gate_chunk):
        gt_ref[0, :, c0:c0 + gate_chunk] = (_dot(h, wg_ref[:, c0:c0 + gate_chunk]) * 0.5).astype(BF16)
    for blk, q_scale in enumerate((A_HEAD_DIM ** -0.5 * LOG2E, None, C_HEAD_DIM ** -0.5 * LOG2E)):
        r = _nt_dot(wt_ref[blk * WIDTH:(blk + 1) * WIDTH, :], h)
        if q_scale is not None:
            r = r * q_scale
        qt_ref[0, blk * WIDTH:(blk + 1) * WIDTH, :] = r.astype(BF16)
    for blk in range(2):
        r0 = (3 + blk) * WIDTH
        r = _nt_dot(wt_ref[r0:r0 + WIDTH, :], h).astype(BF16)
        for hd in range(WIDTH // VT_HEAD_DIM):
            o0 = hd * VT_HEAD_ROWS
            vt_ref[0, blk, o0:o0 + VT_HEAD_DIM, :] = r[hd * VT_HEAD_DIM:(hd + 1) * VT_HEAD_DIM, :]
            vt_ref[0, blk, o0 + VT_HEAD_DIM:o0 + VT_HEAD_ROWS, :] = jnp.ones(
                (VT_HEAD_ROWS - VT_HEAD_DIM, tm), BF16)
    iwt_ref[0] = _nt_dot(wt_ref[5 * WIDTH:5 * WIDTH + SUBLANES, :], h)
    ik = _dot(h, wik_ref[...])
    mu = jnp.mean(ik, axis=-1, keepdims=True)
    xc = ik - mu
    var = jnp.mean(xc * xc, axis=-1, keepdims=True)
    ikn_ref[0] = (xc * lax.rsqrt(var + EPS) * ikg_ref[...] + ikb_ref[...]).astype(BF16)


def _proj(x, g, wn, wik, wg, wt, ikg, ikb, tm):
    bsz, seq, d = x.shape
    n_nat, n_gate = wn.shape[1], wg.shape[1]
    assert wt.shape[0] == 5 * WIDTH + SUBLANES
    grid = (bsz, seq // tm)
    const = lambda b, i: (0, 0)
    return pl.pallas_call(
        functools.partial(_proj_kernel, gate_chunk=min(n_gate, PROJ_GATE_COLS)),
        grid=grid,
        in_specs=[
            pl.BlockSpec((1, tm, d), lambda b, i: (b, i, 0)),
            pl.BlockSpec((1, d), const),
            pl.BlockSpec(wn.shape, const, pipeline_mode=pl.Buffered(1)),
            pl.BlockSpec(wik.shape, const, pipeline_mode=pl.Buffered(1)),
            pl.BlockSpec(wg.shape, const, pipeline_mode=pl.Buffered(1)),
            pl.BlockSpec(wt.shape, const, pipeline_mode=pl.Buffered(1)),
            pl.BlockSpec(ikg.shape, const),
            pl.BlockSpec(ikb.shape, const),
        ],
        out_specs=[
            pl.BlockSpec((1, tm, n_nat), lambda b, i: (b, i, 0)),
            pl.BlockSpec((1, tm, IDX_DIM), lambda b, i: (b, i, 0)),
            pl.BlockSpec((1, tm, n_gate), lambda b, i: (b, i, 0)),
            pl.BlockSpec((1, 3 * WIDTH, tm), lambda b, i: (b, 0, i)),
            pl.BlockSpec((1, 2, VT_ROWS, tm), lambda b, i: (b, 0, 0, i)),
            pl.BlockSpec((1, SUBLANES, tm), lambda b, i: (b, 0, i)),
        ],
        out_shape=[
            jax.ShapeDtypeStruct((bsz, seq, n_nat), BF16),
            jax.ShapeDtypeStruct((bsz, seq, IDX_DIM), BF16),
            jax.ShapeDtypeStruct((bsz, seq, n_gate), BF16),
            jax.ShapeDtypeStruct((bsz, 3 * WIDTH, seq), BF16),
            jax.ShapeDtypeStruct((bsz, 2, VT_ROWS, seq), BF16),
            jax.ShapeDtypeStruct((bsz, SUBLANES, seq), F32),
        ],
        compiler_params=pltpu.CompilerParams(
            dimension_semantics=("parallel", "parallel"), vmem_limit_bytes=VMEM_LIMIT_BYTES),
        name="proj",
    )(x, g, wn, wik, wg, wt, ikg, ikb)


def _dsa_kernel(qt_ref, iqt_ref, iwt_ref, ikn_ref, k_ref, vt_ref, o_ref,
                hi_s, lo_s, bias_s, qm_s, acc_s, m_s, tri_s, s0_s, s1_s, lm_s, *, tq, top_k):
    i = pl.program_id(1)
    kc = tq
    n_chunks = i + 1
    n_grp = kc // PACK16_ROWS
    q_pos = i * tq + lax.broadcasted_iota(I32, (1, tq), 1)
    one16 = jnp.ones((PACK16_ROWS, tq), I16)
    zero16 = jnp.zeros((PACK16_ROWS, tq), I16)

    def rows16(v):
        return jnp.broadcast_to(v, (PACK16_ROWS, tq)).astype(I16)

    w_all = iwt_ref[0] * (IDX_DIM ** -0.5 * IDX_HEADS ** -0.5)
    w_rows = [w_all[h:h + 1, :] for h in range(IDX_HEADS)]
    for h in range(IDX_HEADS):
        qm_s[0:IDX_DIM, h * tq:(h + 1) * tq] = iqt_ref[0, h * IDX_DIM:(h + 1) * IDX_DIM, :]

    def score_body(c, diagonal):
        off = pl.multiple_of(c * kc, kc)
        ik = ikn_ref[0, pl.ds(off, kc), :]
        d = _dot(ik, qm_s[0:IDX_DIM, :])
        sc = jnp.zeros((kc, tq), F32)
        for h in range(IDX_HEADS):
            sc = sc + jnp.maximum(d[:, h * tq:(h + 1) * tq], 0.0) * w_rows[h]
        if diagonal:
            key_pos = off + lax.broadcasted_iota(I32, (kc, tq), 0)
            sc = jnp.where(key_pos <= q_pos, sc, -jnp.inf)
        bits = pltpu.bitcast(sc, I32)
        key = bits ^ ((bits >> 31) & 0x7FFFFFFF)
        hi_s[pl.ds(off, kc), :] = (key >> 16).astype(I16)
        lo_s[pl.ds(off, kc), :] = ((key & 0xFFFF) - 0x8000).astype(I16)

    _for_chunks(i, lambda c: score_body(c, False))
    score_body(i, True)

    def count_ge(ref, cand):
        cb = rows16(cand)

        def body(c, accs):
            off = pl.multiple_of(c * kc, kc)
            slab = ref[pl.ds(off, kc), :]
            accs = list(accs)
            for r in range(n_grp):
                kk = slab[r * PACK16_ROWS:(r + 1) * PACK16_ROWS, :]
                accs[r % N_COUNT_ACC] = accs[r % N_COUNT_ACC] + jnp.where(kk >= cb, one16, zero16)
            return tuple(accs)

        def group(j, accs):
            for u in range(CHUNKS_PER_TRIP):
                accs = body(CHUNKS_PER_TRIP * j + u, accs)
            return accs

        accs = lax.fori_loop(0, n_chunks // CHUNKS_PER_TRIP, group, (zero16,) * N_COUNT_ACC)
        accs = lax.fori_loop(CHUNKS_PER_TRIP * (n_chunks // CHUNKS_PER_TRIP), n_chunks, body, accs)
        tot = accs[0]
        for a in accs[1:]:
            tot = tot + a
        return jnp.sum(tot.astype(I32), axis=0, keepdims=True)

    def descend(ref, need):
        c0 = count_ge(ref, jnp.zeros((1, tq), I32))
        ok0 = c0 >= need
        thr = jnp.where(ok0, 0, INT16_MIN).astype(I32)
        c_gt = jnp.where(ok0, 0, c0)

        def bit_body(j, carry):
            thr, c_gt = carry
            cand = thr | (jnp.int32(1) << (14 - j))
            cnt = count_ge(ref, cand)
            ok = cnt >= need
            return jnp.where(ok, cand, thr), jnp.where(ok, c_gt, cnt)

        return lax.fori_loop(0, 15, bit_body, (thr, c_gt))

    thr_hi, c_gt_hi = descend(hi_s, top_k)
    thr_hi16 = rows16(thr_hi)
    need_lo = top_k - c_gt_hi

    def restrict_body(c, carry):
        off = pl.multiple_of(c * kc, kc)
        in_bucket = hi_s[pl.ds(off, kc), :] == jnp.concatenate([thr_hi16] * n_grp, axis=0)
        lo_s[pl.ds(off, kc), :] = jnp.where(in_bucket, lo_s[pl.ds(off, kc), :], jnp.int16(INT16_MIN))
        return carry

    lax.fori_loop(0, n_chunks, restrict_body, 0)
    thr_lo, c_gt_lo = descend(lo_s, need_lo)
    thr_lo16 = rows16(thr_lo)
    need_tie = (need_lo - c_gt_lo).astype(F32)

    r_i = lax.broadcasted_iota(I32, (kc, kc), 0)
    c_i = lax.broadcasted_iota(I32, (kc, kc), 1)
    tri_s[0:kc, 0:kc] = jnp.where(c_i <= r_i, 1.0, 0.0).astype(BF16)
    tri_s[0:kc, kc:2 * kc] = jnp.where(c_i == r_i, TIE_BIG, 0.0).astype(BF16)
    tri_s[kc:kc + PACK16_ROWS, 0:kc] = jnp.ones((PACK16_ROWS, kc), BF16)
    tri_s[kc:kc + PACK16_ROWS, kc:2 * kc] = jnp.zeros((PACK16_ROWS, kc), BF16)
    thr_hi_kc = jnp.concatenate([thr_hi16] * n_grp, axis=0)
    thr_lo_kc = jnp.concatenate([thr_lo16] * n_grp, axis=0)
    one_b, zero_b, neg_b = (jnp.full((kc, tq), v, BF16) for v in (1.0, 0.0, -1.0))

    def bias_body(c, seen, diagonal):
        off = pl.multiple_of(c * kc, kc)
        hh = hi_s[pl.ds(off, kc), :]
        ll = lo_s[pl.ds(off, kc), :]
        in_bucket = hh == thr_hi_kc
        lo_eq = ll == thr_lo_kc
        equal = jnp.where(in_bucket, jnp.where(lo_eq, one_b, zero_b), zero_b)
        side = jnp.where(hh > thr_hi_kc, neg_b,
                         jnp.where(in_bucket, jnp.where(ll > thr_lo_kc, neg_b, jnp.where(lo_eq, zero_b, one_b)),
                                   one_b))
        rank = _dot(tri_s[...], jnp.concatenate([equal, side], axis=0))
        b = jnp.where(rank[0:kc, :] <= need_tie - seen, 0.0, NEG)
        if diagonal:
            key_pos = off + lax.broadcasted_iota(I32, (kc, tq), 0)
            b = jnp.where(key_pos <= q_pos, b, NEG)
        bias_s[pl.ds(off, kc), :] = b
        return seen + rank[kc:kc + 1, :]

    def bias_group(j, seen):
        for u in range(CHUNKS_PER_TRIP):
            seen = bias_body(CHUNKS_PER_TRIP * j + u, seen, False)
        return seen

    seen = lax.fori_loop(0, i // CHUNKS_PER_TRIP, bias_group, jnp.zeros((1, tq), F32))
    seen = lax.fori_loop(CHUNKS_PER_TRIP * (i // CHUNKS_PER_TRIP), i,
                         lambda c, seen: bias_body(c, seen, False), seen)
    bias_body(i, seen, True)

    feat = lax.broadcasted_iota(I32, (WIDTH, tq), 0)
    qt = qt_ref[0]
    for h in range(A_HEADS):
        qm_s[:, h * tq:(h + 1) * tq] = jnp.where(feat // A_HEAD_DIM == h, qt, jnp.zeros_like(qt))
    acc_s[...] = jnp.zeros_like(acc_s)
    m_s[0:1, :] = jnp.full((1, A_HEADS * tq), NEG, F32)

    span = 2 * kc

    def logits_block(off):
        bias = bias_s[pl.ds(off, kc), :]
        return _dot(k_ref[0, pl.ds(off, kc), :], qm_s[...]) + jnp.concatenate([bias] * A_HEADS, axis=1)

    def pv_block(off, pb, h):
        rows = slice(h * VT_HEAD_ROWS, (h + 1) * VT_HEAD_ROWS)
        return _dot(vt_ref[0, 0, rows, pl.ds(off, kc)], pb[:, h * tq:(h + 1) * tq])

    def rescale_and_add(alpha, pv):
        for h in range(A_HEADS):
            rows = slice(h * VT_HEAD_ROWS, (h + 1) * VT_HEAD_ROWS)
            acc_s[rows, :] = acc_s[rows, :] * alpha[:, h * tq:(h + 1) * tq] + pv[h]

    def attend_chunk(off):
        s = logits_block(off)
        m_old = m_s[0:1, :]
        m_new = jnp.maximum(m_old, jnp.max(s, axis=0, keepdims=True))
        m_s[0:1, :] = m_new
        pb = jnp.exp2(s - m_new).astype(BF16)
        rescale_and_add(jnp.exp2(m_old - m_new), [pv_block(off, pb, h) for h in range(A_HEADS)])

    def logits_span(off, buf):
        s_ref = (s0_s, s1_s)[buf]
        lm = None
        for r in range(span // kc):
            s = logits_block(off + r * kc)
            s_ref[r * kc:(r + 1) * kc, :] = s
            lm_r = jnp.max(s, axis=0, keepdims=True)
            lm = lm_r if lm is None else jnp.maximum(lm, lm_r)
        lm_s[buf:buf + 1, :] = lm

    def step(off_next, off_cur, buf_next):
        buf_cur = 1 - buf_next
        s_next, s_cur = (s0_s, s1_s)[buf_next], (s0_s, s1_s)[buf_cur]
        m_old = m_s[0:1, :]
        m_new = jnp.maximum(m_old, lm_s[buf_cur:buf_cur + 1, :])
        m_s[0:1, :] = m_new
        lm, pv = None, [None] * A_HEADS
        for r in range(span // kc):
            blk = slice(r * kc, (r + 1) * kc)
            s = logits_block(off_next + r * kc)
            s_next[blk, :] = s
            lm_r = jnp.max(s, axis=0, keepdims=True)
            lm = lm_r if lm is None else jnp.maximum(lm, lm_r)
            pb = jnp.exp2(s_cur[blk, :] - m_new).astype(BF16)
            for h in range(A_HEADS):
                d = pv_block(off_cur + r * kc, pb, h)
                pv[h] = d if pv[h] is None else pv[h] + d
        lm_s[buf_next:buf_next + 1, :] = lm
        rescale_and_add(jnp.exp2(m_old - m_new), pv)

    n_full = n_chunks // 2

    @pl.when(n_chunks % 2 == 1)
    def _():
        attend_chunk(pl.multiple_of(i * kc, kc))

    def drain(off_cur, buf_cur):
        s_cur = (s0_s, s1_s)[buf_cur]
        m_old = m_s[0:1, :]
        m_new = jnp.maximum(m_old, lm_s[buf_cur:buf_cur + 1, :])
        m_s[0:1, :] = m_new
        pv = [None] * A_HEADS
        for r in range(span // kc):
            pb = jnp.exp2(s_cur[r * kc:(r + 1) * kc, :] - m_new).astype(BF16)
            for h in range(A_HEADS):
                d = pv_block(off_cur + r * kc, pb, h)
                pv[h] = d if pv[h] is None else pv[h] + d
        rescale_and_add(jnp.exp2(m_old - m_new), pv)

    @pl.when(n_full > 0)
    def _():
        logits_span(0, 0)
        n_steps = n_full - 1

        def body(jj, carry):
            off0 = pl.multiple_of(2 * jj * span, span)
            step(off0 + span, off0, 1)
            step(off0 + 2 * span, off0 + span, 0)
            return carry

        lax.fori_loop(0, n_steps // 2, body, 0)

        @pl.when(n_steps % 2 == 1)
        def _():
            off_cur = pl.multiple_of((n_steps - 1) * span, span)
            step(off_cur + span, off_cur, 1)

        off_last = pl.multiple_of((n_full - 1) * span, span)
        for parity in range(2):
            @pl.when((n_full - 1) % 2 == parity)
            def _():
                drain(off_last, parity)

    out = []
    for h in range(A_HEADS):
        r0 = h * VT_HEAD_ROWS
        out.append(acc_s[r0:r0 + VT_HEAD_DIM, :] * (1.0 / acc_s[r0 + VT_HEAD_DIM:r0 + VT_HEAD_DIM + 1, :]))
    o_ref[0] = jnp.concatenate(out, axis=0).T.astype(BF16)


def _dsa(qt, vt, pn, ikn, iwt, tq):
    bsz, seq, _ = pn.shape
    top_k = min(TOPK_MAX, seq // 4)
    grid = (bsz, seq // tq)
    return pl.pallas_call(
        functools.partial(_dsa_kernel, tq=tq, top_k=top_k),
        grid=grid,
        in_specs=[
            pl.BlockSpec((1, WIDTH, tq), lambda b, i: (b, 0, i)),
            pl.BlockSpec((1, WIDTH, tq), lambda b, i: (b, 1, i)),
            pl.BlockSpec((1, SUBLANES, tq), lambda b, i: (b, 0, i)),
            pl.BlockSpec((1, seq, IDX_DIM), lambda b, i: (b, 0, 0)),
            pl.BlockSpec((1, seq, WIDTH), lambda b, i: (b, 0, 0)),
            pl.BlockSpec((1, 1, VT_ROWS, seq), lambda b, i: (b, 0, 0, 0)),
        ],
        out_specs=pl.BlockSpec((1, tq, WIDTH), lambda b, i: (b, i, 0)),
        out_shape=jax.ShapeDtypeStruct((bsz, seq, WIDTH), BF16),
        scratch_shapes=[pltpu.VMEM((seq, tq), I16),
                        pltpu.VMEM((seq, tq), I16),
                        pltpu.VMEM((seq, tq), F32),
                        pltpu.VMEM((WIDTH, A_HEADS * tq), BF16),
                        pltpu.VMEM((VT_ROWS, tq), F32),
                        pltpu.VMEM((SUBLANES, A_HEADS * tq), F32),
                        pltpu.VMEM((tq + PACK16_ROWS, 2 * tq), BF16),
                        pltpu.VMEM((2 * tq, A_HEADS * tq), F32),
                        pltpu.VMEM((2 * tq, A_HEADS * tq), F32),
                        pltpu.VMEM((SUBLANES, A_HEADS * tq), F32)],
        compiler_params=pltpu.CompilerParams(
            dimension_semantics=("parallel", "arbitrary"), vmem_limit_bytes=VMEM_LIMIT_BYTES),
        name="dsa",
    )(qt, qt, iwt, ikn, pn, vt)


def _diff_kernel(qt_ref, k_ref, vt_ref, lam_ref, sg_ref, o_ref, qm_s, acc_s, m_s, s0_s, s1_s, lm_s,
                 *, tq, lambda_init):
    i = pl.program_id(1)
    span = tq
    blk = DIFF_BLOCK_ROWS
    n_blk = span // blk
    n_pairs = 2 * C_HEADS
    grp = n_pairs // 2
    feat = lax.broadcasted_iota(I32, (WIDTH, tq), 0)
    qt = qt_ref[0]
    for p_i in range(n_pairs):
        qm_s[p_i // grp, :, (p_i % grp) * tq:(p_i % grp + 1) * tq] = jnp.where(
            feat // C_HEAD_DIM == p_i, qt, jnp.zeros_like(qt))
    acc_s[...] = jnp.zeros_like(acc_s)
    m_s[0:1, :] = jnp.full((1, n_pairs * tq), NEG, F32)
    bufs = (s0_s, s1_s)
    pair_rows = [slice((p_i // 2) * VT_HEAD_ROWS, (p_i // 2 + 1) * VT_HEAD_ROWS) for p_i in range(n_pairs)]

    def logits_block(off, r, g, masked):
        s = _dot(k_ref[0, pl.ds(off + r * blk, blk), :], qm_s[g])
        if masked:
            row = r * blk + lax.broadcasted_iota(I32, (blk, tq), 0)
            ok = row <= lax.broadcasted_iota(I32, (blk, tq), 1)
            s = jnp.where(jnp.concatenate([ok] * grp, axis=1), s, NEG)
        return s

    def logits(off, buf, masked):
        for g in range(2):
            lm = None
            for r in range(n_blk):
                s = logits_block(off, r, g, masked)
                bufs[buf][g, r * blk:(r + 1) * blk, :] = s
                lm_r = jnp.max(s, axis=0, keepdims=True)
                lm = lm_r if lm is None else jnp.maximum(lm, lm_r)
            lm_s[buf:buf + 1, g * grp * tq:(g + 1) * grp * tq] = lm

    def new_max(buf_cur):
        m_new, alpha = [], []
        for g in range(2):
            st = slice(g * grp * tq, (g + 1) * grp * tq)
            m_old = m_s[0:1, st]
            m_new.append(jnp.maximum(m_old, lm_s[buf_cur:buf_cur + 1, st]))
            alpha.append(jnp.exp2(m_old - m_new[g]))
            m_s[0:1, st] = m_new[g]
        return m_new, alpha

    def pv_block(off_cur, r, g, pb, pv):
        for j in range(grp):
            p_i = g * grp + j
            d = _dot(vt_ref[0, 0, pair_rows[p_i], pl.ds(off_cur + r * blk, blk)], pb[:, j * tq:(j + 1) * tq])
            pv[p_i] = d if pv[p_i] is None else pv[p_i] + d

    def rescale_and_add(alpha, pv):
        for p_i in range(n_pairs):
            a = alpha[p_i // grp][:, (p_i % grp) * tq:(p_i % grp + 1) * tq]
            acc_s[p_i % 2, pair_rows[p_i], :] = acc_s[p_i % 2, pair_rows[p_i], :] * a + pv[p_i]

    def accumulate(off_cur, buf_cur):
        m_new, alpha = new_max(buf_cur)
        pv = [None] * n_pairs
        for r in range(n_blk):
            for g in range(2):
                pb = jnp.exp2(bufs[buf_cur][g, r * blk:(r + 1) * blk, :] - m_new[g]).astype(BF16)
                pv_block(off_cur, r, g, pb, pv)
        rescale_and_add(alpha, pv)

    def step(off_next, off_cur, buf_next):
        buf_cur = 1 - buf_next
        m_new, alpha = new_max(buf_cur)
        lm, pv = [None, None], [None] * n_pairs
        for r in range(n_blk):
            for g in range(2):
                s = logits_block(off_next, r, g, False)
                bufs[buf_next][g, r * blk:(r + 1) * blk, :] = s
                lm_r = jnp.max(s, axis=0, keepdims=True)
                lm[g] = lm_r if lm[g] is None else jnp.maximum(lm[g], lm_r)
                pb = jnp.exp2(bufs[buf_cur][g, r * blk:(r + 1) * blk, :] - m_new[g]).astype(BF16)
                pv_block(off_cur, r, g, pb, pv)
        for g in range(2):
            lm_s[buf_next:buf_next + 1, g * grp * tq:(g + 1) * grp * tq] = lm[g]
        rescale_and_add(alpha, pv)

    logits(pl.multiple_of(i * span, span), 0, True)

    def item_off(t):
        return pl.multiple_of(jnp.where(t == 0, i, t - 1) * span, span)

    def body(jj, carry):
        step(item_off(2 * jj + 1), item_off(2 * jj), 1)
        step(item_off(2 * jj + 2), item_off(2 * jj + 1), 0)
        return carry

    lax.fori_loop(0, i // 2, body, 0)

    @pl.when(i % 2 == 1)
    def _():
        step(item_off(i), item_off(i - 1), 1)

    for parity in range(2):
        @pl.when(i % 2 == parity)
        def _():
            accumulate(item_off(i), parity)

    lq = lam_ref[...]
    lam = (jnp.exp(jnp.sum(lq[0:1] * lq[1:2], axis=1, keepdims=True))
           - jnp.exp(jnp.sum(lq[2:3] * lq[3:4], axis=1, keepdims=True)) + lambda_init)
    outs = []
    for h in range(C_HEADS):
        r0 = h * VT_HEAD_ROWS
        l_row = slice(r0 + VT_HEAD_DIM, r0 + VT_HEAD_DIM + 1)
        a1 = acc_s[0, r0:r0 + VT_HEAD_DIM, :] * (1.0 / acc_s[0, l_row, :])
        a2 = acc_s[1, r0:r0 + VT_HEAD_DIM, :] * (1.0 / acc_s[1, l_row, :])
        out = a1 - lam * a2
        rs = lax.rsqrt(jnp.mean(out * out, axis=0, keepdims=True) + EPS)
        outs.append(out * rs * sg_ref[...] * (1.0 - lambda_init))
    o_ref[0] = jnp.concatenate(outs, axis=0).T.astype(BF16)


def _diff(qt, vt, pn, lam_p, sg_col, tq, lambda_init):
    bsz, seq, _ = pn.shape
    grid = (bsz, seq // tq)
    n_pairs = 2 * C_HEADS
    return pl.pallas_call(
        functools.partial(_diff_kernel, tq=tq, lambda_init=lambda_init),
        grid=grid,
        in_specs=[
            pl.BlockSpec((1, WIDTH, tq), lambda b, i: (b, 2, i)),
            pl.BlockSpec((1, seq, WIDTH), lambda b, i: (b, 0, 3)),
            pl.BlockSpec((1, 1, VT_ROWS, seq), lambda b, i: (b, 1, 0, 0)),
            pl.BlockSpec(lam_p.shape, lambda b, i: (0, 0)),
            pl.BlockSpec(sg_col.shape, lambda b, i: (0, 0)),
        ],
        out_specs=pl.BlockSpec((1, tq, WIDTH), lambda b, i: (b, i, 0)),
        out_shape=jax.ShapeDtypeStruct((bsz, seq, WIDTH), BF16),
        scratch_shapes=[pltpu.VMEM((2, WIDTH, (n_pairs // 2) * tq), BF16),
                        pltpu.VMEM((2, VT_ROWS, tq), F32),
                        pltpu.VMEM((SUBLANES, n_pairs * tq), F32),
                        pltpu.VMEM((2, tq, (n_pairs // 2) * tq), F32),
                        pltpu.VMEM((2, tq, (n_pairs // 2) * tq), F32),
                        pltpu.VMEM((SUBLANES, n_pairs * tq), F32)],
        compiler_params=pltpu.CompilerParams(
            dimension_semantics=("parallel", "arbitrary"), vmem_limit_bytes=VMEM_LIMIT_BYTES),
        name="diff",
    )(qt, pn, vt, lam_p, sg_col)


def _gelu_tanh(x):
    return x * (0.5 * (1.0 + jnp.tanh(math.sqrt(2.0 / math.pi) * (x + 0.044715 * (x * x * x)))))


def _merge_kernel(x_ref, ya_ref, bu_ref, bv_ref, yc_ref, gt_ref, lg_ref, lb_ref, ws_ref, bs_ref,
                  wa_ref, wb_ref, wc_ref, wo_ref, o_ref, yb_s, *, tm):
    d = x_ref.shape[-1]
    lane_w = lax.broadcasted_iota(I32, (CHUNK, B_WIDTH), 1)
    group_of_lane = lane_w // (B_WIDTH // B_GROUPS)
    r_i = lax.broadcasted_iota(I32, (CHUNK, CHUNK), 0)
    c_i = lax.broadcasted_iota(I32, (CHUNK, CHUNK), 1)
    w_tril = [jnp.where(c_i <= r_i, ws_ref[g], 0.0).astype(BF16) for g in range(B_GROUPS)]
    for c in range(tm // CHUNK):
        rows = slice(c * CHUNK, (c + 1) * CHUNK)
        u = _gelu_tanh(bu_ref[0, rows, :].astype(F32))
        v = _gelu_tanh(bv_ref[0, rows, :].astype(F32))
        mu = jnp.mean(v, axis=-1, keepdims=True)
        vc = v - mu
        var = jnp.mean(vc * vc, axis=-1, keepdims=True)
        vn = (vc * lax.rsqrt(var + EPS) * lg_ref[...] + lb_ref[...]).astype(BF16)
        s = bs_ref[...]
        for g in range(B_GROUPS):
            s = s + _dot(w_tril[g], jnp.where(group_of_lane == g, vn, jnp.zeros_like(vn)))
        yb_s[rows, :] = (u * s).astype(BF16)

    merged = None
    branches = ((ya_ref[0], wa_ref), (yb_s[...], wb_ref), (yc_ref[0], wc_ref))
    for n, (y, w_ref) in enumerate(branches):
        y_half = _dot(y * 0.5, w_ref[...])
        term = y_half + y_half * jnp.tanh(gt_ref[0, :, n * d:(n + 1) * d].astype(F32))
        merged = term if merged is None else merged + term
    o_ref[0] = x_ref[0] + _dot(merged.astype(BF16), wo_ref[...])


def _merge(x, pn, ya, yc, gates, lg, lb, ws, bs_full, wa, wb, wc, wo, tm):
    bsz, seq, d = x.shape
    grid = (bsz, seq // tm)
    const2 = lambda b, i: (0, 0)
    return pl.pallas_call(
        functools.partial(_merge_kernel, tm=tm),
        grid=grid,
        in_specs=[
            pl.BlockSpec((1, tm, d), lambda b, i: (b, i, 0)),
            pl.BlockSpec((1, tm, WIDTH), lambda b, i: (b, i, 0)),
            pl.BlockSpec((1, tm, WIDTH), lambda b, i: (b, i, 1)),
            pl.BlockSpec((1, tm, WIDTH), lambda b, i: (b, i, 2)),
            pl.BlockSpec((1, tm, WIDTH), lambda b, i: (b, i, 0)),
            pl.BlockSpec((1, tm, 3 * d), lambda b, i: (b, i, 0)),
            pl.BlockSpec(lg.shape, const2),
            pl.BlockSpec(lb.shape, const2),
            pl.BlockSpec(ws.shape, lambda b, i: (0, 0, 0)),
            pl.BlockSpec(bs_full.shape, const2),
            pl.BlockSpec(wa.shape, const2),
            pl.BlockSpec(wb.shape, const2),
            pl.BlockSpec(wc.shape, const2),
            pl.BlockSpec(wo.shape, const2),
        ],
        out_specs=pl.BlockSpec((1, tm, d), lambda b, i: (b, i, 0)),
        out_shape=jax.ShapeDtypeStruct((bsz, seq, d), F32),
        scratch_shapes=[pltpu.VMEM((tm, B_WIDTH), BF16)],
        compiler_params=pltpu.CompilerParams(
            dimension_semantics=("parallel", "parallel"), vmem_limit_bytes=VMEM_LIMIT_BYTES),
        name="merge",
    )(x, ya, pn, pn, yc, gates, lg, lb, ws, bs_full, wa, wb, wc, wo)


def _ffn_kernel(x_ref, g_ref, w1_ref, w2_ref, fg_ref, o_ref, h_s, acc_s, *, final_norm):
    j = pl.program_id(1)

    @pl.when(j == 0)
    def _():
        x = x_ref[...]
        ms = jnp.mean(x * x, axis=-1, keepdims=True)
        h_s[...] = (x * lax.rsqrt(ms + EPS) * g_ref[...]).astype(BF16)
        acc_s[...] = jnp.zeros_like(acc_s)

    a = jnp.maximum(_dot(h_s[...], w1_ref[...]), 0.0)
    acc_s[...] += _dot((a * a).astype(BF16), w2_ref[...])

    @pl.when(j == pl.num_programs(1) - 1)
    def _():
        y = x_ref[...] + acc_s[...]
        if final_norm:
            ms = jnp.mean(y * y, axis=-1, keepdims=True)
            y = y * lax.rsqrt(ms + EPS) * fg_ref[...]
        o_ref[...] = y


def _ffn(x2d, g, w1, w2, fg, tm, tf, final_norm):
    m, d = x2d.shape
    dff = w1.shape[1]
    grid = (m // tm, dff // tf)
    return pl.pallas_call(
        functools.partial(_ffn_kernel, final_norm=final_norm),
        grid=grid,
        in_specs=[
            pl.BlockSpec((tm, d), lambda i, j: (i, 0)),
            pl.BlockSpec((1, d), lambda i, j: (0, 0)),
            pl.BlockSpec((d, tf), lambda i, j: (0, j)),
            pl.BlockSpec((tf, d), lambda i, j: (j, 0)),
            pl.BlockSpec((1, d), lambda i, j: (0, 0)),
        ],
        out_specs=pl.BlockSpec((tm, d), lambda i, j: (i, 0)),
        out_shape=jax.ShapeDtypeStruct((m, d), F32),
        scratch_shapes=[pltpu.VMEM((tm, d), BF16), pltpu.VMEM((tm, d), F32)],
        compiler_params=pltpu.CompilerParams(
            dimension_semantics=("parallel", "arbitrary"), vmem_limit_bytes=VMEM_LIMIT_BYTES),
        name="ffn",
    )(x2d, g, w1, w2, fg)


def _tile(n, pref):
    t = min(n, pref)
    assert n % t == 0, (n, t)
    return t


def kernel(x, attn_norm_g, w_in, idx_k_norm_g, idx_k_norm_b, sgu_norm_g, sgu_norm_b, sgu_w_s, sgu_b_s,
           diff_lambda, diff_subln_g, w_branch_a, w_branch_b, w_branch_c, w_out, mlp_norm_g, w_ff1,
           w_ff2, final_norm_g):
    bsz, seq, d = x.shape
    depth = w_in.shape[0]
    offs = [0]
    for s in IN_SIZES:
        offs.append(offs[-1] + s)
    (o_aq, o_ak, o_av, o_iq, o_ik, o_iw, o_buv, o_cq, o_ck, o_cv, o_g, o_end) = offs

    tm_proj = _tile(seq, PROJ_ROWS)
    tq = _tile(seq, DSA_QUERIES)
    tq_diff = _tile(seq, DIFF_QUERIES)
    tm_merge = _tile(seq, MERGE_ROWS)
    tm_ffn = _tile(bsz * seq, FFN_ROWS)
    tf = _tile(w_ff1.shape[2], FFN_HIDDEN_COLS)

    for l in range(depth):
        lambda_init = 0.8 - 0.6 * math.exp(-0.3 * l)
        w = w_in[l]
        cols = lambda a, b: w[:, a:b]
        wn = jnp.concatenate([cols(o_ak, o_av), cols(o_buv, o_cq), cols(o_ck, o_cv)], axis=1).astype(BF16)
        wt = jnp.concatenate([cols(o_aq, o_ak), cols(o_iq, o_ik), cols(o_cq, o_ck), cols(o_av, o_iq),
                              cols(o_cv, o_g), cols(o_iw, o_buv),
                              jnp.zeros((d, SUBLANES - IDX_HEADS), w.dtype)], axis=1).T.astype(BF16)
        wik = cols(o_ik, o_iw).astype(BF16)
        wg = cols(o_g, o_end).astype(BF16)

        pn, ikn, gates, qt, vt, iwt = _proj(
            x, attn_norm_g[l][None, :], wn, wik, wg, wt,
            idx_k_norm_g[l][None, :], idx_k_norm_b[l][None, :], tm_proj)

        ya = _dsa(qt, vt, pn, ikn, iwt, tq)
        sg_col = diff_subln_g[l][:, None]
        yc = _diff(qt, vt, pn, diff_lambda[l], sg_col, tq_diff, lambda_init)

        bs_full = jnp.repeat(sgu_b_s[l].T, B_WIDTH // B_GROUPS, axis=1)
        x = _merge(x, pn, ya, yc, gates, sgu_norm_g[l][None, :], sgu_norm_b[l][None, :], sgu_w_s[l],
                   bs_full, w_branch_a[l].astype(BF16), w_branch_b[l].astype(BF16),
                   w_branch_c[l].astype(BF16), w_out[l].astype(BF16), tm_merge)

        x = _ffn(x.reshape(bsz * seq, d), mlp_norm_g[l][None, :], w_ff1[l].astype(BF16),
                 w_ff2[l].astype(BF16), final_norm_g[None, :], tm_ffn, tf,
                 final_norm=(l == depth - 1)).reshape(bsz, seq, d)
    return x
```
